```python
import jax, jax.numpy as jnp
from jax import lax
import numpy as np

D_MODEL = 1024
BATCH = 8
SEQ = 2048
DEPTH = 1
DEC_BATCH = 128
DEC_SEQ = 4
PAST_LEN = 16384
PAGE_SIZE = 128

POOL_WIDTH = D_MODEL // 4
POOL_WINDOWS = (2, 4, 8, 16)
POOL_GROUPS = len(POOL_WINDOWS)
POOL_GDIM = POOL_WIDTH // POOL_GROUPS
POOL_BUF = max(POOL_WINDOWS) - 1
HG_HEADS = 4
HG_KDIM = 128
HG_WIDTH = HG_HEADS * HG_KDIM
HG_VDIM = D_MODEL // 2 // HG_HEADS
HG_VWIDTH = HG_HEADS * HG_VDIM
HG_CHUNK = 64
MEM_LEN = 256
MEM_HEADS = 4
MEM_HDIM = D_MODEL // 4 // MEM_HEADS
MEM_WIDTH = MEM_HEADS * MEM_HDIM
N_BRANCH = 3
D_FF = 2816
CONV_W = 3

LN_EPS = 1e-5
RMS_EPS = 1e-6
DN_ALPHA = (2 * DEPTH) ** 0.25
DN_BETA = (8 * DEPTH) ** -0.25

SPLIT_SIZES = (POOL_WIDTH, HG_WIDTH, HG_WIDTH, HG_VWIDTH, HG_VWIDTH, MEM_WIDTH)
SPLIT_POINTS = tuple(int(s) for s in np.cumsum(SPLIT_SIZES))
IN_COLS = sum(SPLIT_SIZES) + N_BRANCH * D_MODEL

kernel_name = 'hybrid_pool_hgrn2_memxattn_convffn_step'


def layer_norm(x, g, b):
    xf = x.astype(jnp.float32)
    mu = jnp.mean(xf, axis=-1, keepdims=True)
    var = jnp.mean(jnp.square(xf - mu), axis=-1, keepdims=True)
    return ((xf - mu) * lax.rsqrt(var + LN_EPS) * g + b).astype(x.dtype)


def rms_norm(x, g):
    xf = x.astype(jnp.float32)
    return xf * lax.rsqrt(jnp.mean(jnp.square(xf), axis=-1, keepdims=True) + RMS_EPS) * g


def multiscale_pool(u, buf, pos0, w_grp, scale):
    bsz, seq_len, _ = u.shape
    full = jnp.concatenate([buf.astype(u.dtype), u], axis=1)
    csum = jnp.cumsum(full.astype(jnp.float32), axis=1)
    csum = jnp.concatenate([jnp.zeros((bsz, 1, POOL_WIDTH), jnp.float32), csum], axis=1)
    end = csum[:, POOL_BUF + 1:POOL_BUF + 1 + seq_len]
    pos = pos0 + jnp.arange(seq_len)
    pooled = []
    for g, w in enumerate(POOL_WINDOWS):
        sl = slice(g * POOL_GDIM, (g + 1) * POOL_GDIM)
        start = csum[:, POOL_BUF + 1 - w:POOL_BUF + 1 - w + seq_len, sl]
        count = jnp.minimum(pos + 1, w).astype(jnp.float32)[None, :, None]
        pooled.append((end[..., sl] - start) / count)
    diff = (jnp.concatenate(pooled, axis=-1) - u.astype(jnp.float32)).astype(u.dtype)
    diff = diff.reshape(bsz, seq_len, POOL_GROUPS, POOL_GDIM)
    mixed = jnp.einsum('blgc,gcd->blgd', diff, w_grp).reshape(bsz, seq_len, POOL_WIDTH) * scale
    return mixed, full[:, -POOL_BUF:]


def hgrn2_chunkwise(q, logf, k, v, s0):
    bsz, seq_len, n_heads, _ = q.shape
    dv = v.shape[-1]
    chunk = min(HG_CHUNK, seq_len)
    n_chunks = -(-seq_len // chunk)
    pad = n_chunks * chunk - seq_len

    def to_chunks(a):
        a = jnp.pad(a, ((0, 0), (0, pad), (0, 0), (0, 0)))
        return a.reshape(bsz, n_chunks, chunk, n_heads, a.shape[-1]).transpose(1, 0, 3, 2, 4)

    qc, lfc, kc, vc = to_chunks(q), to_chunks(logf), to_chunks(k), to_chunks(v)
    causal = jnp.tril(jnp.ones((chunk, chunk), bool))[:, :, None]

    def step(state, inp):
        qi, lfi, ki, vi = inp
        b = jnp.cumsum(lfi, axis=2)
        gap = b[:, :, :, None, :] - b[:, :, None, :, :]
        decay = jnp.exp(jnp.where(causal, gap, -jnp.inf))
        scores = jnp.einsum('bhtd,bhsd,bhtsd->bhts', qi, ki, decay)
        out = jnp.einsum('bhts,bhse->bhte', scores, vi) + jnp.einsum('bhtd,bhde->bhte', qi * jnp.exp(b), state)
        b_last = b[:, :, -1:, :]
        new_state = jnp.exp(b_last[:, :, 0, :, None]) * state + jnp.einsum('bhsd,bhse->bhde', ki * jnp.exp(b_last - b), vi)
        return new_state, out

    s_final, out = lax.scan(step, s0, (qc, lfc, kc, vc))
    out = out.transpose(1, 0, 3, 2, 4).reshape(bsz, n_chunks * chunk, n_heads, dv)[:, :seq_len]
    return out, s_final


def memory_attention(q, mem_k, mem_v):
    s = jnp.einsum('blhd,bmhd->bhlm', q, mem_k).astype(jnp.float32) * (MEM_HDIM ** -0.5)
    p = jax.nn.softmax(s, axis=-1).astype(mem_v.dtype)
    return jnp.einsum('bhlm,bmhd->blhd', p, mem_v)


def conv_ffn(h, buf, w_gate, w_up, conv_w, conv_b, w_down):
    seq_len = h.shape[1]
    a = h @ w_gate
    u = h @ w_up
    full = jnp.concatenate([buf.astype(a.dtype), a], axis=1)
    c = conv_b + full[:, 0:seq_len] * conv_w[0]
    for j in range(1, CONV_W):
        c = c + full[:, j:j + seq_len] * conv_w[j]
    act = jax.nn.gelu(c.astype(jnp.float32), approximate=False).astype(h.dtype)
    return (act * u) @ w_down, full[:, -(CONV_W - 1):]


def decoder_layer(x, pool_buf, hg_state, conv_buf, mem_k, mem_v, pos0, lb, p):
    bsz, seq_len, _ = x.shape
    dt = x.dtype
    u_a, q_b, f_b, i_b, g_b, q_c, gate_pre = jnp.split(x @ p['w_in'], SPLIT_POINTS, axis=-1)
    y_a, new_pool = multiscale_pool(u_a, pool_buf, pos0, p['w_pool_grp'], p['pool_scale'])
    f_pre = f_b.astype(jnp.float32)
    f = lb + (1.0 - lb) * jax.nn.sigmoid(f_pre)
    k = (1.0 - lb) * jax.nn.sigmoid(-f_pre)
    ks = (bsz, seq_len, HG_HEADS, HG_KDIM)
    vs = (bsz, seq_len, HG_HEADS, HG_VDIM)
    o_b, s_new = hgrn2_chunkwise(jax.nn.silu(q_b.astype(jnp.float32)).reshape(ks), jnp.log(f).reshape(ks),
                                 k.reshape(ks), i_b.astype(jnp.float32).reshape(vs), hg_state.astype(jnp.float32))
    o_b = rms_norm(o_b, p['hg_norm_g'].reshape(HG_HEADS, HG_VDIM)).reshape(bsz, seq_len, HG_VWIDTH)
    y_b = (o_b * jax.nn.silu(g_b.astype(jnp.float32))).astype(dt)
    y_c = memory_attention(q_c.reshape(bsz, seq_len, MEM_HEADS, MEM_HDIM), mem_k, mem_v).reshape(bsz, seq_len, MEM_WIDTH)
    gates = jax.nn.sigmoid(gate_pre.astype(jnp.float32)).astype(dt).reshape(bsz, seq_len, N_BRANCH, D_MODEL)
    merged = (gates[:, :, 0] * (y_a @ p['w_br_pool'])
              + gates[:, :, 1] * (y_b @ p['w_br_hg'])
              + gates[:, :, 2] * (y_c @ p['w_br_mem']))
    h = layer_norm(DN_ALPHA * x + merged @ p['w_out'], p['ln1_g'], p['ln1_b'])
    ff, new_conv = conv_ffn(h, conv_buf, p['w_gate'], p['w_up'], p['conv_w'], p['conv_b'], p['w_down'])
    y = layer_norm(DN_ALPHA * h + ff, p['ln2_g'], p['ln2_b'])
    return y, new_pool, s_new.astype(hg_state.dtype), new_conv


def setup_inputs(seed: int = 0) -> dict:
    key = jax.random.key(seed)
    ks = jax.random.split(key, 32)

    def nrm(k, shape, s):
        return jax.random.normal(k, shape, jnp.float32) * s

    return {
        'x_prompt': nrm(ks[0], (BATCH, SEQ, D_MODEL), 1.0),
        'x_sample': nrm(ks[1], (DEC_BATCH, DEC_SEQ, D_MODEL), 1.0),
        'state_pool': nrm(ks[2], (DEPTH, DEC_BATCH, POOL_BUF, POOL_WIDTH), 1.0),
        'state_hgrn': nrm(ks[3], (DEPTH, DEC_BATCH, HG_HEADS, HG_KDIM, HG_VDIM), 0.5),
        'state_ffn_conv': nrm(ks[4], (DEPTH, DEC_BATCH, CONV_W - 1, D_FF), 1.0),
        'cache_mem_k': nrm(ks[5], (DEPTH, DEC_BATCH, MEM_LEN, MEM_HEADS, MEM_HDIM), 1.0),
        'cache_mem_v': nrm(ks[6], (DEPTH, DEC_BATCH, MEM_LEN, MEM_HEADS, MEM_HDIM), 1.0),
        'mem_prompt': nrm(ks[7], (BATCH, MEM_LEN, D_MODEL), 1.0),
        'lb_logits': nrm(ks[8], (DEPTH + 1, HG_WIDTH), 0.1),
        'w_in': nrm(ks[9], (DEPTH, D_MODEL, IN_COLS), D_MODEL ** -0.5),
        'w_pool_grp': nrm(ks[10], (DEPTH, POOL_GROUPS, POOL_GDIM, POOL_GDIM), POOL_GDIM ** -0.5),
        'pool_scale': 1.0 + nrm(ks[11], (DEPTH, POOL_WIDTH), 0.02),
        'hg_norm_g': 1.0 + nrm(ks[12], (DEPTH, HG_VWIDTH), 0.02),
        'w_mem_k': nrm(ks[13], (DEPTH, D_MODEL, MEM_WIDTH), D_MODEL ** -0.5),
        'w_mem_v': nrm(ks[14], (DEPTH, D_MODEL, MEM_WIDTH), D_MODEL ** -0.5),
        'w_br_pool': nrm(ks[15], (DEPTH, POOL_WIDTH, D_MODEL), POOL_WIDTH ** -0.5),
        'w_br_hg': nrm(ks[16], (DEPTH, HG_VWIDTH, D_MODEL), HG_VWIDTH ** -0.5),
        'w_br_mem': nrm(ks[17], (DEPTH, MEM_WIDTH, D_MODEL), MEM_WIDTH ** -0.5),
        'w_out': nrm(ks[18], (DEPTH, D_MODEL, D_MODEL), DN_BETA * D_MODEL ** -0.5),
        'ln1_g': 1.0 + nrm(ks[19], (DEPTH, D_MODEL), 0.02),
        'ln1_b': nrm(ks[20], (DEPTH, D_MODEL), 0.02),
        'w_gate': nrm(ks[21], (DEPTH, D_MODEL, D_FF), D_MODEL ** -0.5),
        'w_up': nrm(ks[22], (DEPTH, D_MODEL, D_FF), D_MODEL ** -0.5),
        'conv_w': nrm(ks[23], (DEPTH, CONV_W, D_FF), 0.5),
        'conv_b': nrm(ks[24], (DEPTH, D_FF), 0.02),
        'w_down': nrm(ks[25], (DEPTH, D_FF, D_MODEL), DN_BETA * D_FF ** -0.5),
        'ln2_g': 1.0 + nrm(ks[26], (DEPTH, D_MODEL), 0.02),
        'ln2_b': nrm(ks[27], (DEPTH, D_MODEL), 0.02),
    }


def reference(x_prompt, x_sample, state_pool, state_hgrn, state_ffn_conv, cache_mem_k, cache_mem_v, mem_prompt,
              lb_logits, w_in, w_pool_grp, pool_scale, hg_norm_g, w_mem_k, w_mem_v, w_br_pool, w_br_hg, w_br_mem,
              w_out, ln1_g, ln1_b, w_gate, w_up, conv_w, conv_b, w_down, ln2_g, ln2_b):
    lower_bounds = jnp.cumsum(jax.nn.softmax(lb_logits.astype(jnp.float32), axis=0), axis=0)
    dt = x_prompt.dtype
    n_prompt = x_prompt.shape[0]
    n_mem = mem_prompt.shape[1]
    hp, hs = x_prompt, x_sample
    pool_p, hgrn_p, conv_p, memk_p, memv_p = [], [], [], [], []
    pool_s, hgrn_s, conv_s = [], [], []
    for l in range(DEPTH):
        p = {'w_in': w_in[l], 'w_pool_grp': w_pool_grp[l], 'pool_scale': pool_scale[l], 'hg_norm_g': hg_norm_g[l],
             'w_br_pool': w_br_pool[l], 'w_br_hg': w_br_hg[l], 'w_br_mem': w_br_mem[l], 'w_out': w_out[l],
             'ln1_g': ln1_g[l], 'ln1_b': ln1_b[l], 'w_gate': w_gate[l], 'w_up': w_up[l], 'conv_w': conv_w[l],
             'conv_b': conv_b[l], 'w_down': w_down[l], 'ln2_g': ln2_g[l], 'ln2_b': ln2_b[l]}
        lb = lower_bounds[l]
        mk = (mem_prompt @ w_mem_k[l]).reshape(n_prompt, n_mem, MEM_HEADS, MEM_HDIM)
        mv = (mem_prompt @ w_mem_v[l]).reshape(n_prompt, n_mem, MEM_HEADS, MEM_HDIM)
        hp, np_pool, np_hgrn, np_conv = decoder_layer(
            hp, jnp.zeros((n_prompt, POOL_BUF, POOL_WIDTH), dt),
            jnp.zeros((n_prompt, HG_HEADS, HG_KDIM, HG_VDIM), dt),
            jnp.zeros((n_prompt, CONV_W - 1, D_FF), dt), mk, mv, 0, lb, p)
        pool_p.append(np_pool)
        hgrn_p.append(np_hgrn)
        conv_p.append(np_conv)
        memk_p.append(mk)
        memv_p.append(mv)
        hs, ns_pool, ns_hgrn, ns_conv = decoder_layer(
            hs, state_pool[l], state_hgrn[l], state_ffn_conv[l], cache_mem_k[l], cache_mem_v[l], PAST_LEN, lb, p)
        pool_s.append(ns_pool)
        hgrn_s.append(ns_hgrn)
        conv_s.append(ns_conv)
    return (hp, hs, jnp.stack(pool_p), jnp.stack(hgrn_p), jnp.stack(conv_p), jnp.stack(memk_p), jnp.stack(memv_p),
            jnp.stack(pool_s), jnp.stack(hgrn_s), jnp.stack(conv_s))
```

```python
import functools

import jax
import jax.numpy as jnp
from jax import lax
from jax.experimental import pallas as pl
from jax.experimental.pallas import tpu as pltpu

F32 = jnp.float32
BF16 = jnp.bfloat16

D_MODEL = 1024
POOL_WIDTH = 256
POOL_WINDOWS = (2, 4, 8, 16)
POOL_GDIM = 64
POOL_BUF = 15
HG_HEADS = 4
HG_KDIM = 128
HG_WIDTH = 512
HG_VDIM = 128
MEM_HEADS = 4
MEM_HDIM = 64
MEM_WIDTH = 256
N_BRANCH = 3
D_FF = 2816
CONV_W = 3
LN_EPS = 1e-5
RMS_EPS = 1e-6

C_UA = (0, 256)
C_QB = (256, 768)
C_FB = (768, 1280)
C_IB = (1280, 1792)
C_GB = (1792, 2304)
C_QC = (2304, 2560)
C_GATE = (2560, 5632)

SUBLANES = 8
CARRY_ROWS = 16
CONV_CARRY = 8
PROMPT_T = 256
HG_CHUNK = 64
SAMPLE_BB = 8
SAMPLE_FFN_BB = 32
VMEM_LIMIT = 56 * 1024 * 1024


def _mm(a, b):
    return jnp.dot(a.astype(BF16), b.astype(BF16), preferred_element_type=F32)


def _mm_nt(a, b):
    return lax.dot_general(a.astype(BF16), b.astype(BF16), (((1,), (1,)), ((), ())), preferred_element_type=F32)


def _mm_tn(a, b):
    return lax.dot_general(a.astype(BF16), b.astype(BF16), (((0,), (0,)), ((), ())), preferred_element_type=F32)


def _split3(x):
    h1 = x.astype(BF16)
    r1 = x - h1.astype(F32)
    h2 = r1.astype(BF16)
    h3 = (r1 - h2.astype(F32)).astype(BF16)
    return h1, h2, h3


def _select_mm(sel, x):
    h1, h2, h3 = _split3(x)
    d = lambda p: jnp.dot(sel, p, preferred_element_type=F32)
    return d(h1) + d(h2) + d(h3)


def _sigmoid(x):
    return 0.5 + 0.5 * jnp.tanh(0.5 * x)


def _layer_norm(x, g, b):
    mu = jnp.mean(x, axis=-1, keepdims=True)
    xc = x - mu
    var = jnp.mean(xc * xc, axis=-1, keepdims=True)
    return xc * lax.rsqrt(var + LN_EPS) * g + b


def _gelu(x):
    return 0.5 * x * (1.0 + lax.erf(x * (2.0 ** -0.5)))


def _lower_bound(lb_logits, layer):
    m = jnp.max(lb_logits, axis=0, keepdims=True)
    e = jnp.exp(lb_logits - m)
    sm = e / jnp.sum(e, axis=0, keepdims=True)
    return jnp.sum(sm[:layer + 1], axis=0, keepdims=True)


def _pool_branch(full, u_a, pos, bd, scale, tail):
    s2 = full + pltpu.roll(full, 1, 0)
    s4 = s2 + pltpu.roll(s2, 2, 0)
    s8 = s4 + pltpu.roll(s4, 4, 0)
    s16 = s8 + pltpu.roll(s8, 8, 0)
    grp = lax.broadcasted_iota(jnp.int32, u_a.shape, 1) // POOL_GDIM
    wsum = jnp.where(grp == 0, tail(s2), jnp.where(grp == 1, tail(s4), jnp.where(grp == 2, tail(s8), tail(s16))))
    wlen = jnp.where(grp == 0, 2, jnp.where(grp == 1, 4, jnp.where(grp == 2, 8, 16)))
    count = jnp.minimum(pos + 1, wlen).astype(F32)
    diff = wsum / count - u_a
    return _mm(diff, bd) * scale


def _forget_gates(fb, lb):
    th = jnp.tanh(0.5 * fb)
    f = lb + (1.0 - lb) * (0.5 + 0.5 * th)
    k = (1.0 - lb) * (0.5 - 0.5 * th)
    return jnp.log(f), k


def _chunk_tri(n, chunk):
    r = lax.broadcasted_iota(jnp.int32, (n, n), 0)
    c = lax.broadcasted_iota(jnp.int32, (n, n), 1)
    return jnp.where((r // chunk == c // chunk) & (c <= r), 1.0, 0.0).astype(BF16)


def _rms_head(o, g):
    return o * lax.rsqrt(jnp.mean(o * o, axis=-1, keepdims=True) + RMS_EPS) * g


def _merge_out(x, y_a, y_b, y_c, gate_pre, wbp, wbh, wbm, wout, g, b, alpha):
    gates = _sigmoid(gate_pre)
    merged = (gates[:, 0:D_MODEL] * _mm(y_a, wbp)
              + gates[:, D_MODEL:2 * D_MODEL] * _mm(y_b, wbh)
              + gates[:, 2 * D_MODEL:3 * D_MODEL] * _mm(y_c, wbm))
    return _layer_norm(alpha * x + _mm(merged, wout), g, b)


def _memkv_kernel(mem_ref, wk_ref, wv_ref, k_ref, v_ref):
    m = mem_ref[...].astype(BF16)
    k_ref[...] = jnp.dot(m, wk_ref[...], preferred_element_type=F32)
    v_ref[...] = jnp.dot(m, wv_ref[...], preferred_element_type=F32)


def _memkv(mem2d, wk, wv, n_mem):
    rows = mem2d.shape[0]
    const = lambda i: (0, 0)
    return pl.pallas_call(
        _memkv_kernel,
        grid=(rows // n_mem,),
        in_specs=[pl.BlockSpec((n_mem, D_MODEL), lambda i: (i, 0)),
                  pl.BlockSpec((D_MODEL, MEM_WIDTH), const),
                  pl.BlockSpec((D_MODEL, MEM_WIDTH), const)],
        out_specs=[pl.BlockSpec((n_mem, MEM_WIDTH), lambda i: (i, 0)),
                   pl.BlockSpec((n_mem, MEM_WIDTH), lambda i: (i, 0))],
        out_shape=[jax.ShapeDtypeStruct((rows, MEM_WIDTH), F32)] * 2,
        name="memkv",
    )(mem2d, wk, wv)


def _mixer_prompt_kernel(x_ref, mk_ref, mv_ref, pool0_ref, hg0_ref, lb_ref, win_ref, bd_ref, pscale_ref, hgg_ref,
                         wbp_ref, wbh_ref, wbm_ref, wout_ref, g_ref, b_ref,
                         h_ref, newpool_ref, newhg_ref, st_scr, pool_scr, *, layer, pos0, alpha):
    T, C = PROMPT_T, HG_CHUNK
    j = pl.program_id(1)

    @pl.when(j == 0)
    def _():
        pool_scr[...] = pool0_ref[0]
        for h in range(HG_HEADS):
            st_scr[h] = hg0_ref[0, h].T

    x = x_ref[0]
    xb = x.astype(BF16)
    proj = lambda c: jnp.dot(xb, win_ref[:, c[0]:c[1]], preferred_element_type=F32)

    u_a = proj(C_UA)
    full = jnp.concatenate([pool_scr[...], u_a], axis=0)
    pos = pos0 + j * T + lax.broadcasted_iota(jnp.int32, (T, POOL_WIDTH), 0)
    y_a = _pool_branch(full, u_a, pos, bd_ref[...], pscale_ref[...], lambda z: z[CARRY_ROWS:])
    pool_scr[...] = full[T:]

    lb = _lower_bound(lb_ref[...], layer)
    logf, k = _forget_gates(proj(C_FB), lb)
    qb = proj(C_QB)
    q = qb * _sigmoid(qb)
    v = proj(C_IB)
    bcum = _select_mm(_chunk_tri(T, C), logf)
    rr = lax.broadcasted_iota(jnp.int32, (C, C), 0)
    cc = lax.broadcasted_iota(jnp.int32, (C, C), 1)
    causal = cc <= rr
    hgg = hgg_ref[...]
    rows_out = []
    for c in range(T // C):
        rs = slice(c * C, (c + 1) * C)
        bc = bcum[rs]
        bm = bc[C // 2:C // 2 + 1]
        bl = bc[C - 1:C]
        qc_, kc, vc = q[rs], k[rs], v[rs]
        qin = qc_ * jnp.exp(bc)
        qd = qc_ * jnp.exp(bc - bm)
        kd = kc * jnp.exp(bm - bc)
        kl = kc * jnp.exp(bl - bc)
        dec = jnp.exp(bl)
        heads = []
        for h in range(HG_HEADS):
            sl = slice(h * HG_KDIM, (h + 1) * HG_KDIM)
            sc = jnp.where(causal, _mm_nt(qd[:, sl], kd[:, sl]), 0.0)
            st = st_scr[h]
            o = _mm(sc, vc[:, sl]) + _mm_nt(qin[:, sl], st)
            st_scr[h] = st * dec[:, sl] + _mm_tn(vc[:, sl], kl[:, sl])
            heads.append(_rms_head(o, hgg[:, sl]))
        rows_out.append(jnp.concatenate(heads, axis=1))
    gb = proj(C_GB)
    y_b = jnp.concatenate(rows_out, axis=0) * (gb * _sigmoid(gb))

    qc = proj(C_QC)
    mk = mk_ref[0].astype(BF16)
    mv = mv_ref[0].astype(BF16)
    head_of_lane = lax.broadcasted_iota(jnp.int32, (T, MEM_WIDTH), 1) // MEM_HDIM
    y_c = jnp.zeros((T, MEM_WIDTH), F32)
    for h in range(MEM_HEADS):
        mine = head_of_lane == h
        s = _mm_nt(jnp.where(mine, qc, 0.0), mk) * (MEM_HDIM ** -0.5)
        e = jnp.exp(s - jnp.max(s, axis=-1, keepdims=True))
        p = e / jnp.sum(e, axis=-1, keepdims=True)
        y_c = y_c + jnp.where(mine, _mm(p, mv), 0.0)

    h_ref[0] = _merge_out(x, y_a, y_b, y_c, proj(C_GATE), wbp_ref[...], wbh_ref[...], wbm_ref[...], wout_ref[...],
                          g_ref[...], b_ref[...], alpha)

    @pl.when(j == pl.num_programs(1) - 1)
    def _():
        newpool_ref[0] = pool_scr[...]
        for h in range(HG_HEADS):
            newhg_ref[0, h] = st_scr[h].T


def _const_spec(shape):
    n = len(shape)
    return pl.BlockSpec(shape, lambda *_: (0,) * n, pipeline_mode=pl.Buffered(1))


def _mixer_prompt(x, mk, mv, pool0, hg0, lb_logits, w, layer, pos0, alpha):
    bsz, seq, _ = x.shape
    T = PROMPT_T
    per_b3 = lambda b, j: (b, 0, 0)
    kern = functools.partial(_mixer_prompt_kernel, layer=layer, pos0=pos0, alpha=alpha)
    weights = [lb_logits, w['w_in'], w['bd_pool'], w['pool_scale'], w['hg_norm_g'], w['w_br_pool'], w['w_br_hg'],
               w['w_br_mem'], w['w_out'], w['ln1_g'], w['ln1_b']]
    return pl.pallas_call(
        kern,
        grid=(bsz, seq // T),
        in_specs=[pl.BlockSpec((1, T, D_MODEL), lambda b, j: (b, j, 0)),
                  pl.BlockSpec((1,) + mk.shape[1:], per_b3),
                  pl.BlockSpec((1,) + mv.shape[1:], per_b3),
                  pl.BlockSpec((1, CARRY_ROWS, POOL_WIDTH), per_b3),
                  pl.BlockSpec((1, HG_HEADS, HG_KDIM, HG_VDIM), lambda b, j: (b, 0, 0, 0))]
                 + [_const_spec(a.shape) for a in weights],
        out_specs=[pl.BlockSpec((1, T, D_MODEL), lambda b, j: (b, j, 0)),
                   pl.BlockSpec((1, CARRY_ROWS, POOL_WIDTH), per_b3),
                   pl.BlockSpec((1, HG_HEADS, HG_KDIM, HG_VDIM), lambda b, j: (b, 0, 0, 0))],
        out_shape=[jax.ShapeDtypeStruct((bsz, seq, D_MODEL), F32),
                   jax.ShapeDtypeStruct((bsz, CARRY_ROWS, POOL_WIDTH), F32),
                   jax.ShapeDtypeStruct((bsz, HG_HEADS, HG_KDIM, HG_VDIM), F32)],
        scratch_shapes=[pltpu.VMEM((HG_HEADS, HG_VDIM, HG_KDIM), F32),
                        pltpu.VMEM((CARRY_ROWS, POOL_WIDTH), F32)],
        compiler_params=pltpu.CompilerParams(dimension_semantics=("arbitrary", "arbitrary"),
                                             vmem_limit_bytes=VMEM_LIMIT),
        name="mixer_prompt",
    )(x, mk, mv, pool0, hg0, *weights)


def _conv_ffn(h, full, tail, wg_unused, u, cw, cb, wd):
    c = cb + pltpu.roll(full, 2, 0) * cw[0:1] + pltpu.roll(full, 1, 0) * cw[1:2] + full * cw[2:3]
    act = _gelu(tail(c))
    return _mm(act * u, wd)


def _ffn_prompt_kernel(h_ref, conv0_ref, wg_ref, wu_ref, cw_ref, cb_ref, wd_ref, g_ref, b_ref,
                       y_ref, newconv_ref, carry_scr, *, alpha):
    T = PROMPT_T
    j = pl.program_id(1)

    @pl.when(j == 0)
    def _():
        carry_scr[...] = conv0_ref[0]

    h = h_ref[0]
    hb = h.astype(BF16)
    a = jnp.dot(hb, wg_ref[...], preferred_element_type=F32)
    u = jnp.dot(hb, wu_ref[...], preferred_element_type=F32)
    full = jnp.concatenate([carry_scr[...], a], axis=0)
    ff = _conv_ffn(h, full, lambda z: z[CONV_CARRY:], None, u, cw_ref[...], cb_ref[...], wd_ref[...])
    y_ref[0] = _layer_norm(alpha * h + ff, g_ref[...], b_ref[...])
    carry_scr[...] = a[T - CONV_CARRY:]

    @pl.when(j == pl.num_programs(1) - 1)
    def _():
        newconv_ref[0] = carry_scr[...]


def _ffn_prompt(h, conv0, w, alpha):
    bsz, seq, _ = h.shape
    T = PROMPT_T
    weights = [w['w_gate'], w['w_up'], w['conv_w'], w['conv_b'], w['w_down'], w['ln2_g'], w['ln2_b']]
    return pl.pallas_call(
        functools.partial(_ffn_prompt_kernel, alpha=alpha),
        grid=(bsz, seq // T),
        in_specs=[pl.BlockSpec((1, T, D_MODEL), lambda b, j: (b, j, 0)),
                  pl.BlockSpec((1, CONV_CARRY, D_FF), lambda b, j: (b, 0, 0))]
                 + [_const_spec(a.shape) for a in weights],
        out_specs=[pl.BlockSpec((1, T, D_MODEL), lambda b, j: (b, j, 0)),
                   pl.BlockSpec((1, CONV_CARRY, D_FF), lambda b, j: (b, 0, 0))],
        out_shape=[jax.ShapeDtypeStruct((bsz, seq, D_MODEL), F32),
                   jax.ShapeDtypeStruct((bsz, CONV_CARRY, D_FF), F32)],
        scratch_shapes=[pltpu.VMEM((CONV_CARRY, D_FF), F32)],
        compiler_params=pltpu.CompilerParams(dimension_semantics=("arbitrary", "arbitrary"),
                                             vmem_limit_bytes=VMEM_LIMIT),
        name="ffn_prompt",
    )(h, conv0, *weights)


def _bmm(spec, a, b):
    return jnp.einsum(spec, a.astype(BF16), b.astype(BF16), preferred_element_type=F32)


def _mixer_sample_kernel(x_ref, mk_ref, mv_ref, pool_ref, hg_ref, lb_ref, win_ref, bd_ref, pscale_ref, hgg_ref,
                         wbp_ref, wbh_ref, wbm_ref, wout_ref, g_ref, b_ref,
                         h_ref, newpool_ref, newhg_ref, *, layer, pos0, alpha, seq):
    Bb, S = SAMPLE_BB, SUBLANES
    R = Bb * S
    x = x_ref[...]
    xb = x.astype(BF16)
    proj = lambda c: jnp.dot(xb, win_ref[:, c[0]:c[1]], preferred_element_type=F32)

    u_a = proj(C_UA)
    seg = CARRY_ROWS + S
    full = jnp.concatenate([pool_ref[...], u_a.reshape(Bb, S, POOL_WIDTH)], axis=1).reshape(Bb * seg, POOL_WIDTH)
    tail = lambda z: z.reshape(Bb, seg, POOL_WIDTH)[:, CARRY_ROWS:, :].reshape(R, POOL_WIDTH)
    pos = pos0 + lax.broadcasted_iota(jnp.int32, (R, POOL_WIDTH), 0) % S
    y_a = _pool_branch(full, u_a, pos, bd_ref[...], pscale_ref[...], tail)
    newpool_ref[...] = pltpu.roll(full, Bb * seg - seq, 0).reshape(Bb, seg, POOL_WIDTH)[:, :CARRY_ROWS, :]

    live = lax.broadcasted_iota(jnp.int32, (R, HG_WIDTH), 0) % S < seq
    lb = _lower_bound(lb_ref[...], layer)
    logf, k = _forget_gates(proj(C_FB), lb)
    logf = jnp.where(live, logf, 0.0)
    k = jnp.where(live, k, 0.0)
    qb = proj(C_QB)
    q = qb * _sigmoid(qb)
    v = proj(C_IB)
    to3 = lambda z: z.reshape(Bb, S, z.shape[-1])
    b3 = to3(_select_mm(_chunk_tri(R, S), logf))
    bm = b3[:, S // 2:S // 2 + 1, :]
    bl = b3[:, S - 1:S, :]
    q3, k3, v3 = to3(q), to3(k), to3(v)
    qin = q3 * jnp.exp(b3)
    qd = q3 * jnp.exp(b3 - bm)
    kd = k3 * jnp.exp(bm - b3)
    kl = k3 * jnp.exp(bl - b3)
    rr = lax.broadcasted_iota(jnp.int32, (Bb, S, S), 1)
    cc = lax.broadcasted_iota(jnp.int32, (Bb, S, S), 2)
    causal = cc <= rr
    p1 = bl.astype(BF16).astype(F32)
    p2 = (bl - p1).astype(BF16).astype(F32)
    p3 = (bl - p1) - p2
    r3 = lax.broadcasted_iota(jnp.int32, (Bb, S, HG_WIDTH), 1)
    pieces = jnp.where(r3 == 0, p1, jnp.where(r3 == 1, p2, jnp.where(r3 == 2, p3, 0.0)))
    ones = jnp.ones((Bb, S, HG_VDIM), BF16)
    hgg = hgg_ref[...]
    heads = []
    for h in range(HG_HEADS):
        sl = slice(h * HG_KDIM, (h + 1) * HG_KDIM)
        sc = jnp.where(causal, _bmm('bqd,bkd->bqk', qd[:, :, sl], kd[:, :, sl]), 0.0)
        s0 = hg_ref[:, h]
        o = _bmm('bqk,bke->bqe', sc, v3[:, :, sl]) + _bmm('bqd,bde->bqe', qin[:, :, sl], s0)
        logdec = _bmm('bkd,bke->bde', pieces[:, :, sl], ones)
        newhg_ref[:, h] = jnp.exp(logdec) * s0 + _bmm('bkd,bke->bde', kl[:, :, sl], v3[:, :, sl])
        heads.append(_rms_head(o, hgg[:, sl]).reshape(R, HG_VDIM))
    gb = proj(C_GB)
    y_b = jnp.concatenate(heads, axis=1) * (gb * _sigmoid(gb))

    qc3 = to3(proj(C_QC))
    head_of_lane = lax.broadcasted_iota(jnp.int32, (Bb, S, MEM_WIDTH), 2) // MEM_HDIM
    q4 = jnp.concatenate([jnp.where(head_of_lane == h, qc3, 0.0) for h in range(MEM_HEADS)], axis=1)
    s = _bmm('bqd,bmd->bqm', q4, mk_ref[...]) * (MEM_HDIM ** -0.5)
    e = jnp.exp(s - jnp.max(s, axis=-1, keepdims=True))
    p = e / jnp.sum(e, axis=-1, keepdims=True)
    o4 = _bmm('bqm,bmd->bqd', p, mv_ref[...])
    y_c = jnp.zeros((Bb, S, MEM_WIDTH), F32)
    for h in range(MEM_HEADS):
        y_c = y_c + jnp.where(head_of_lane == h, o4[:, h * S:(h + 1) * S, :], 0.0)
    y_c = y_c.reshape(R, MEM_WIDTH)

    h_ref[...] = _merge_out(x, y_a, y_b, y_c, proj(C_GATE), wbp_ref[...], wbh_ref[...], wbm_ref[...], wout_ref[...],
                            g_ref[...], b_ref[...], alpha)


def _mixer_sample(x2d, mk, mv, pool16, hg, lb_logits, w, layer, pos0, alpha, seq):
    Bb, S = SAMPLE_BB, SUBLANES
    bsz = hg.shape[0]
    n_mem = mk.shape[1]
    b3 = lambda i: (i, 0, 0)
    weights = [lb_logits, w['w_in'], w['bd_pool'], w['pool_scale'], w['hg_norm_g'], w['w_br_pool'], w['w_br_hg'],
               w['w_br_mem'], w['w_out'], w['ln1_g'], w['ln1_b']]
    return pl.pallas_call(
        functools.partial(_mixer_sample_kernel, layer=layer, pos0=pos0, alpha=alpha, seq=seq),
        grid=(bsz // Bb,),
        in_specs=[pl.BlockSpec((Bb * S, D_MODEL), lambda i: (i, 0)),
                  pl.BlockSpec((Bb, n_mem, MEM_WIDTH), b3),
                  pl.BlockSpec((Bb, n_mem, MEM_WIDTH), b3),
                  pl.BlockSpec((Bb, CARRY_ROWS, POOL_WIDTH), b3),
                  pl.BlockSpec((Bb, HG_HEADS, HG_KDIM, HG_VDIM), lambda i: (i, 0, 0, 0))]
                 + [_const_spec(a.shape) for a in weights],
        out_specs=[pl.BlockSpec((Bb * S, D_MODEL), lambda i: (i, 0)),
                   pl.BlockSpec((Bb, CARRY_ROWS, POOL_WIDTH), b3),
                   pl.BlockSpec((Bb, HG_HEADS, HG_KDIM, HG_VDIM), lambda i: (i, 0, 0, 0))],
        out_shape=[jax.ShapeDtypeStruct((bsz * S, D_MODEL), F32),
                   jax.ShapeDtypeStruct((bsz, CARRY_ROWS, POOL_WIDTH), F32),
                   jax.ShapeDtypeStruct((bsz, HG_HEADS, HG_KDIM, HG_VDIM), F32)],
        compiler_params=pltpu.CompilerParams(dimension_semantics=("arbitrary",), vmem_limit_bytes=VMEM_LIMIT),
        name="mixer_sample",
    )(x2d, mk, mv, pool16, hg, *weights)


def _ffn_sample_kernel(h_ref, conv_ref, wg_ref, wu_ref, cw_ref, cb_ref, wd_ref, g_ref, b_ref,
                       y_ref, newconv_ref, *, alpha, seq):
    Bb, S = SAMPLE_FFN_BB, SUBLANES
    R = Bb * S
    seg = CONV_CARRY + S
    h = h_ref[...]
    hb = h.astype(BF16)
    a = jnp.dot(hb, wg_ref[...], preferred_element_type=F32)
    u = jnp.dot(hb, wu_ref[...], preferred_element_type=F32)
    full = jnp.concatenate([conv_ref[...], a.reshape(Bb, S, D_FF)], axis=1).reshape(Bb * seg, D_FF)
    tail = lambda z: z.reshape(Bb, seg, D_FF)[:, CONV_CARRY:, :].reshape(R, D_FF)
    ff = _conv_ffn(h, full, tail, None, u, cw_ref[...], cb_ref[...], wd_ref[...])
    y_ref[...] = _layer_norm(alpha * h + ff, g_ref[...], b_ref[...])
    newconv_ref[...] = pltpu.roll(full, Bb * seg - seq, 0).reshape(Bb, seg, D_FF)[:, :CONV_CARRY, :]


def _ffn_sample(h2d, conv8, w, alpha, seq):
    Bb, S = SAMPLE_FFN_BB, SUBLANES
    bsz = conv8.shape[0]
    weights = [w['w_gate'], w['w_up'], w['conv_w'], w['conv_b'], w['w_down'], w['ln2_g'], w['ln2_b']]
    return pl.pallas_call(
        functools.partial(_ffn_sample_kernel, alpha=alpha, seq=seq),
        grid=(bsz // Bb,),
        in_specs=[pl.BlockSpec((Bb * S, D_MODEL), lambda i: (i, 0)),
                  pl.BlockSpec((Bb, CONV_CARRY, D_FF), lambda i: (i, 0, 0))]
                 + [_const_spec(a.shape) for a in weights],
        out_specs=[pl.BlockSpec((Bb * S, D_MODEL), lambda i: (i, 0)),
                   pl.BlockSpec((Bb, CONV_CARRY, D_FF), lambda i: (i, 0, 0))],
        out_shape=[jax.ShapeDtypeStruct((bsz * S, D_MODEL), F32),
                   jax.ShapeDtypeStruct((bsz, CONV_CARRY, D_FF), F32)],
        compiler_params=pltpu.CompilerParams(dimension_semantics=("arbitrary",), vmem_limit_bytes=VMEM_LIMIT),
        name="ffn_sample",
    )(h2d, conv8, *weights)


def _block_diag(w_grp):
    groups, gdim, _ = w_grp.shape
    out = jnp.zeros((groups * gdim, groups * gdim), w_grp.dtype)
    for g in range(groups):
        out = lax.dynamic_update_slice(out, w_grp[g], (g * gdim, g * gdim))
    return out


def _front_pad(a, rows):
    return jnp.pad(a, ((0, 0), (rows - a.shape[1], 0), (0, 0)))


def kernel(x_prompt, x_sample, state_pool, state_hgrn, state_ffn_conv, cache_mem_k, cache_mem_v, mem_prompt, lb_logits, w_in, w_pool_grp, pool_scale, hg_norm_g, w_mem_k, w_mem_v, w_br_pool, w_br_hg, w_br_mem, w_out, ln1_g, ln1_b, w_gate, w_up, conv_w, conv_b, w_down, ln2_g, ln2_b):
    depth = w_in.shape[0]
    alpha = (2 * depth) ** 0.25
    n_prompt, seq_p, _ = x_prompt.shape
    n_sample, seq_s, _ = x_sample.shape
    n_mem = mem_prompt.shape[1]
    past_len = 16384
    assert seq_p % PROMPT_T == 0 and seq_s <= SUBLANES and n_sample % SAMPLE_BB == 0 and n_sample % SAMPLE_FFN_BB == 0

    hp = x_prompt
    hs = jnp.pad(x_sample, ((0, 0), (0, SUBLANES - seq_s), (0, 0))).reshape(n_sample * SUBLANES, D_MODEL)
    mem2d = mem_prompt.reshape(n_prompt * n_mem, D_MODEL)
    row = lambda a: a.reshape(1, -1)
    outs = [[] for _ in range(8)]
    for l in range(depth):
        w = {'w_in': w_in[l].astype(BF16), 'bd_pool': _block_diag(w_pool_grp[l]).astype(BF16),
             'pool_scale': row(pool_scale[l]), 'hg_norm_g': row(hg_norm_g[l]),
             'w_br_pool': w_br_pool[l].astype(BF16), 'w_br_hg': w_br_hg[l].astype(BF16),
             'w_br_mem': w_br_mem[l].astype(BF16), 'w_out': w_out[l].astype(BF16),
             'ln1_g': row(ln1_g[l]), 'ln1_b': row(ln1_b[l]),
             'w_gate': w_gate[l].astype(BF16), 'w_up': w_up[l].astype(BF16), 'conv_w': conv_w[l],
             'conv_b': row(conv_b[l]), 'w_down': w_down[l].astype(BF16),
             'ln2_g': row(ln2_g[l]), 'ln2_b': row(ln2_b[l])}
        mk, mv = _memkv(mem2d, w_mem_k[l].astype(BF16), w_mem_v[l].astype(BF16), n_mem)
        mk3 = mk.reshape(n_prompt, n_mem, MEM_WIDTH)
        mv3 = mv.reshape(n_prompt, n_mem, MEM_WIDTH)
        h_mid, pool_p, hg_p = _mixer_prompt(
            hp, mk3, mv3, jnp.zeros((n_prompt, CARRY_ROWS, POOL_WIDTH), F32),
            jnp.zeros((n_prompt, HG_HEADS, HG_KDIM, HG_VDIM), F32), lb_logits, w, l, 0, alpha)
        hp, conv_p = _ffn_prompt(h_mid, jnp.zeros((n_prompt, CONV_CARRY, D_FF), F32), w, alpha)
        outs[0].append(pool_p[:, CARRY_ROWS - POOL_BUF:])
        outs[1].append(hg_p)
        outs[2].append(conv_p[:, CONV_CARRY - (CONV_W - 1):])
        outs[3].append(mk.reshape(n_prompt, n_mem, MEM_HEADS, MEM_HDIM))
        outs[4].append(mv.reshape(n_prompt, n_mem, MEM_HEADS, MEM_HDIM))
        hs_mid, pool_s, hg_s = _mixer_sample(
            hs, cache_mem_k[l].reshape(n_sample, n_mem, MEM_WIDTH), cache_mem_v[l].reshape(n_sample, n_mem, MEM_WIDTH),
            _front_pad(state_pool[l], CARRY_ROWS), state_hgrn[l], lb_logits, w, l, past_len, alpha, seq_s)
        hs, conv_s = _ffn_sample(hs_mid, _front_pad(state_ffn_conv[l], CONV_CARRY), w, alpha, seq_s)
        outs[5].append(pool_s[:, CARRY_ROWS - POOL_BUF:])
        outs[6].append(hg_s)
        outs[7].append(conv_s[:, CONV_CARRY - (CONV_W - 1):])
    y_sample = hs.reshape(n_sample, SUBLANES, D_MODEL)[:, :seq_s]
    return (hp, y_sample) + tuple(jnp.stack(o) for o in outs)
```

```python
import functools

import jax
import jax.numpy as jnp
from jax import lax
from jax.experimental import pallas as pl
from jax.experimental.pallas import tpu as pltpu

F32 = jnp.float32
BF16 = jnp.bfloat16

D_MODEL = 1024
POOL_WIDTH = 256
POOL_WINDOWS = (2, 4, 8, 16)
POOL_GDIM = 64
POOL_BUF = 15
HG_HEADS = 4
HG_KDIM = 128
HG_WIDTH = 512
HG_VDIM = 128
MEM_HEADS = 4
MEM_HDIM = 64
MEM_WIDTH = 256
N_BRANCH = 3
D_FF = 2816
CONV_W = 3
LN_EPS = 1e-5
RMS_EPS = 1e-6

C_UA = (0, 256)
C_QB = (256, 768)
C_FB = (768, 1280)
C_IB = (1280, 1792)
C_GB = (1792, 2304)
C_QC = (2304, 2560)
C_GATE = (2560, 5632)

SUBLANES = 8
CARRY_ROWS = 16
CONV_CARRY = 8
PROMPT_T = 256
HG_CHUNK = 64
SAMPLE_BB = 8
SAMPLE_FFN_BB = 32
VMEM_LIMIT = 56 * 1024 * 1024


def _mm(a, b):
    return jnp.dot(a.astype(BF16), b.astype(BF16), preferred_element_type=F32)


def _mm_nt(a, b):
    return lax.dot_general(a.astype(BF16), b.astype(BF16), (((1,), (1,)), ((), ())), preferred_element_type=F32)


def _mm_tn(a, b):
    return lax.dot_general(a.astype(BF16), b.astype(BF16), (((0,), (0,)), ((), ())), preferred_element_type=F32)


def _split3(x):
    h1 = x.astype(BF16)
    r1 = x - h1.astype(F32)
    h2 = r1.astype(BF16)
    h3 = (r1 - h2.astype(F32)).astype(BF16)
    return h1, h2, h3


def _select_mm(sel, x):
    h1, h2, h3 = _split3(x)
    d = lambda p: jnp.dot(sel, p, preferred_element_type=F32)
    return d(h1) + d(h2) + d(h3)


def _sigmoid(x):
    return 0.5 + 0.5 * jnp.tanh(0.5 * x)


def _layer_norm(x, g, b):
    mu = jnp.mean(x, axis=-1, keepdims=True)
    xc = x - mu
    var = jnp.mean(xc * xc, axis=-1, keepdims=True)
    return xc * lax.rsqrt(var + LN_EPS) * g + b


def _gelu(x):
    return 0.5 * x * (1.0 + lax.erf(x * (2.0 ** -0.5)))


def _lower_bound(lb_logits, layer):
    m = jnp.max(lb_logits, axis=0, keepdims=True)
    e = jnp.exp(lb_logits - m)
    sm = e / jnp.sum(e, axis=0, keepdims=True)
    return jnp.sum(sm[:layer + 1], axis=0, keepdims=True)


def _pool_diff(full, u_a, pos, tail):
    s2 = full + pltpu.roll(full, 1, 0)
    s4 = s2 + pltpu.roll(s2, 2, 0)
    s8 = s4 + pltpu.roll(s4, 4, 0)
    s16 = s8 + pltpu.roll(s8, 8, 0)
    grp = lax.broadcasted_iota(jnp.int32, u_a.shape, 1) // POOL_GDIM
    wsum = jnp.where(grp == 0, tail(s2), jnp.where(grp == 1, tail(s4), jnp.where(grp == 2, tail(s8), tail(s16))))
    wlen = jnp.where(grp == 0, 2, jnp.where(grp == 1, 4, jnp.where(grp == 2, 8, 16)))
    count = jnp.minimum(pos + 1, wlen).astype(F32)
    return wsum / count - u_a


def _forget_gates(fb, lb):
    th = jnp.tanh(0.5 * fb)
    f = lb + (1.0 - lb) * (0.5 + 0.5 * th)
    k = (1.0 - lb) * (0.5 - 0.5 * th)
    return jnp.log(f), k


def _chunk_tri(n, chunk):
    r = lax.broadcasted_iota(jnp.int32, (n, n), 0)
    c = lax.broadcasted_iota(jnp.int32, (n, n), 1)
    return (r // chunk == c // chunk) & (c <= r)


def _as_bf16(mask):
    return jnp.where(mask, 1.0, 0.0).astype(BF16)


def _rms_head(o, g):
    return o * lax.rsqrt(jnp.mean(o * o, axis=-1, keepdims=True) + RMS_EPS) * g


def _merge_out(x, y_a, y_b, y_c, gate_pre, wbp, wbh, wbm, wout, g, b, alpha):
    gates = _sigmoid(gate_pre)
    merged = (gates[:, 0:D_MODEL] * _mm(y_a, wbp)
              + gates[:, D_MODEL:2 * D_MODEL] * _mm(y_b, wbh)
              + gates[:, 2 * D_MODEL:3 * D_MODEL] * _mm(y_c, wbm))
    return _layer_norm(alpha * x + _mm(merged, wout), g, b)


def _memkv_kernel(mem_ref, wk_ref, wv_ref, k_ref, v_ref):
    m = mem_ref[...].astype(BF16)
    k_ref[...] = jnp.dot(m, wk_ref[...], preferred_element_type=F32)
    v_ref[...] = jnp.dot(m, wv_ref[...], preferred_element_type=F32)


def _memkv(mem2d, wk, wv, n_mem):
    rows = mem2d.shape[0]
    const = lambda i: (0, 0)
    return pl.pallas_call(
        _memkv_kernel,
        grid=(rows // n_mem,),
        in_specs=[pl.BlockSpec((n_mem, D_MODEL), lambda i: (i, 0)),
                  pl.BlockSpec((D_MODEL, MEM_WIDTH), const),
                  pl.BlockSpec((D_MODEL, MEM_WIDTH), const)],
        out_specs=[pl.BlockSpec((n_mem, MEM_WIDTH), lambda i: (i, 0)),
                   pl.BlockSpec((n_mem, MEM_WIDTH), lambda i: (i, 0))],
        out_shape=[jax.ShapeDtypeStruct((rows, MEM_WIDTH), F32)] * 2,
        name="memkv",
    )(mem2d, wk, wv)


def _mixer_prompt_kernel(x_ref, mk_ref, mv_ref, pool0_ref, hg0_ref, lb_ref, win_ref, bd_ref, pscale_ref, hgg_ref,
                         wbp_ref, wbh_ref, wbm_ref, wout_ref, g_ref, b_ref,
                         h_ref, newpool_ref, newhg_ref, st_scr, pool_scr, *, layer, pos0, alpha):
    T, C = PROMPT_T, HG_CHUNK
    j = pl.program_id(1)

    @pl.when(j == 0)
    def _():
        pool_scr[...] = pool0_ref[0]
        for h in range(HG_HEADS):
            st_scr[h] = hg0_ref[0, h].T

    x = x_ref[0]
    xb = x.astype(BF16)
    proj = lambda c: jnp.dot(xb, win_ref[:, c[0]:c[1]], preferred_element_type=F32)

    lb = _lower_bound(lb_ref[...], layer)
    logf, k = _forget_gates(proj(C_FB), lb)
    qb = proj(C_QB)
    v = proj(C_IB)
    u_a = proj(C_UA)
    qc = proj(C_QC)
    same_chunk_causal = _chunk_tri(T, C)
    bcum = _select_mm(_as_bf16(same_chunk_causal), logf)
    gb = proj(C_GB)
    gate_a = proj((C_GATE[0], C_GATE[0] + D_MODEL))

    full = jnp.concatenate([pool_scr[...], u_a], axis=0)
    pos = pos0 + j * T + lax.broadcasted_iota(jnp.int32, (T, POOL_WIDTH), 0)
    diff_a = _pool_diff(full, u_a, pos, lambda z: z[CARRY_ROWS:])
    pool_scr[...] = full[T:]

    mk = mk_ref[0].astype(BF16)
    mv = mv_ref[0].astype(BF16)
    head_of_lane = lax.broadcasted_iota(jnp.int32, (T, MEM_WIDTH), 1) // MEM_HDIM
    att = [_mm_nt(jnp.where(head_of_lane == h, qc, 0.0), mk) * (MEM_HDIM ** -0.5) for h in range(MEM_HEADS)]
    gate_b = proj((C_GATE[0] + D_MODEL, C_GATE[0] + 2 * D_MODEL))

    q = qb * _sigmoid(qb)
    n_chunks = T // C
    chunk_rows = [slice(c * C, (c + 1) * C) for c in range(n_chunks)]
    per_chunk = lambda row: jnp.concatenate(
        [jnp.broadcast_to(bcum[c * C + row:c * C + row + 1], (C, HG_WIDTH)) for c in range(n_chunks)], axis=0)
    bm = per_chunk(C // 2)
    bl = per_chunk(C - 1)
    qin = q * jnp.exp(bcum)
    qd = q * jnp.exp(bcum - bm)
    kd = k * jnp.exp(bm - bcum)
    kl = k * jnp.exp(bl - bcum)
    dec = [jnp.exp(bcum[c * C + C - 1:(c + 1) * C]) for c in range(n_chunks)]
    hgg = hgg_ref[...]
    head_lanes = [slice(h * HG_KDIM, (h + 1) * HG_KDIM) for h in range(HG_HEADS)]
    sc = [jnp.where(same_chunk_causal, _mm_nt(qd[:, sl], kd[:, sl]), 0.0) for sl in head_lanes]
    grow = [[_mm_tn(v[rs, sl], kl[rs, sl]) for rs in chunk_rows] for sl in head_lanes]
    y_a = _mm(diff_a, bd_ref[...]) * pscale_ref[...]
    gate_c = proj((C_GATE[0] + 2 * D_MODEL, C_GATE[1]))

    y_c = jnp.zeros((T, MEM_WIDTH), F32)
    for h in range(MEM_HEADS):
        e = jnp.exp(att[h] - jnp.max(att[h], axis=-1, keepdims=True))
        p = e / jnp.sum(e, axis=-1, keepdims=True)
        y_c = y_c + jnp.where(head_of_lane == h, _mm(p, mv), 0.0)
    m_a = _mm(y_a, wbp_ref[...])

    heads = []
    for h, sl in enumerate(head_lanes):
        o_intra = _mm(sc[h], v[:, sl])
        st = st_scr[h]
        o_inter = []
        for c, rs in enumerate(chunk_rows):
            o_inter.append(_mm_nt(qin[rs, sl], st))
            st = st * dec[c][:, sl] + grow[h][c]
        st_scr[h] = st
        heads.append(_rms_head(o_intra + jnp.concatenate(o_inter, axis=0), hgg[:, sl]))
    m_c = _mm(y_c, wbm_ref[...])
    y_b = jnp.concatenate(heads, axis=1) * (gb * _sigmoid(gb))
    merged = _sigmoid(gate_a) * m_a + _sigmoid(gate_c) * m_c + _sigmoid(gate_b) * _mm(y_b, wbh_ref[...])
    h_ref[0] = _layer_norm(alpha * x + _mm(merged, wout_ref[...]), g_ref[...], b_ref[...])

    @pl.when(j == pl.num_programs(1) - 1)
    def _():
        newpool_ref[0] = pool_scr[...]
        for h in range(HG_HEADS):
            newhg_ref[0, h] = st_scr[h].T


def _const_spec(shape):
    n = len(shape)
    return pl.BlockSpec(shape, lambda *_: (0,) * n, pipeline_mode=pl.Buffered(1))


def _mixer_prompt(x, mk, mv, pool0, hg0, lb_logits, w, layer, pos0, alpha):
    bsz, seq, _ = x.shape
    T = PROMPT_T
    per_b3 = lambda b, j: (b, 0, 0)
    kern = functools.partial(_mixer_prompt_kernel, layer=layer, pos0=pos0, alpha=alpha)
    weights = [lb_logits, w['w_in'], w['bd_pool'], w['pool_scale'], w['hg_norm_g'], w['w_br_pool'], w['w_br_hg'],
               w['w_br_mem'], w['w_out'], w['ln1_g'], w['ln1_b']]
    return pl.pallas_call(
        kern,
        grid=(bsz, seq // T),
        in_specs=[pl.BlockSpec((1, T, D_MODEL), lambda b, j: (b, j, 0)),
                  pl.BlockSpec((1,) + mk.shape[1:], per_b3),
                  pl.BlockSpec((1,) + mv.shape[1:], per_b3),
                  pl.BlockSpec((1, CARRY_ROWS, POOL_WIDTH), per_b3),
                  pl.BlockSpec((1, HG_HEADS, HG_KDIM, HG_VDIM), lambda b, j: (b, 0, 0, 0))]
                 + [_const_spec(a.shape) for a in weights],
        out_specs=[pl.BlockSpec((1, T, D_MODEL), lambda b, j: (b, j, 0)),
                   pl.BlockSpec((1, CARRY_ROWS, POOL_WIDTH), per_b3),
                   pl.BlockSpec((1, HG_HEADS, HG_KDIM, HG_VDIM), lambda b, j: (b, 0, 0, 0))],
        out_shape=[jax.ShapeDtypeStruct((bsz, seq, D_MODEL), F32),
                   jax.ShapeDtypeStruct((bsz, CARRY_ROWS, POOL_WIDTH), F32),
                   jax.ShapeDtypeStruct((bsz, HG_HEADS, HG_KDIM, HG_VDIM), F32)],
        scratch_shapes=[pltpu.VMEM((HG_HEADS, HG_VDIM, HG_KDIM), F32),
                        pltpu.VMEM((CARRY_ROWS, POOL_WIDTH), F32)],
        compiler_params=pltpu.CompilerParams(dimension_semantics=("arbitrary", "arbitrary"),
                                             vmem_limit_bytes=VMEM_LIMIT),
        name="mixer_prompt",
    )(x, mk, mv, pool0, hg0, *weights)


def _conv_ffn(h, full, tail, wg_unused, u, cw, cb, wd):
    c = cb + pltpu.roll(full, 2, 0) * cw[0:1] + pltpu.roll(full, 1, 0) * cw[1:2] + full * cw[2:3]
    act = _gelu(tail(c))
    return _mm(act * u, wd)


def _ffn_prompt_kernel(h_ref, conv0_ref, wg_ref, wu_ref, cw_ref, cb_ref, wd_ref, g_ref, b_ref,
                       y_ref, newconv_ref, carry_scr, *, alpha):
    T = PROMPT_T
    j = pl.program_id(1)

    @pl.when(j == 0)
    def _():
        carry_scr[...] = conv0_ref[0]

    h = h_ref[0]
    hb = h.astype(BF16)
    a = jnp.dot(hb, wg_ref[...], preferred_element_type=F32)
    u = jnp.dot(hb, wu_ref[...], preferred_element_type=F32)
    full = jnp.concatenate([carry_scr[...], a], axis=0)
    ff = _conv_ffn(h, full, lambda z: z[CONV_CARRY:], None, u, cw_ref[...], cb_ref[...], wd_ref[...])
    y_ref[0] = _layer_norm(alpha * h + ff, g_ref[...], b_ref[...])
    carry_scr[...] = a[T - CONV_CARRY:]

    @pl.when(j == pl.num_programs(1) - 1)
    def _():
        newconv_ref[0] = carry_scr[...]


def _ffn_prompt(h, conv0, w, alpha):
    bsz, seq, _ = h.shape
    T = PROMPT_T
    weights = [w['w_gate'], w['w_up'], w['conv_w'], w['conv_b'], w['w_down'], w['ln2_g'], w['ln2_b']]
    return pl.pallas_call(
        functools.partial(_ffn_prompt_kernel, alpha=alpha),
        grid=(bsz, seq // T),
        in_specs=[pl.BlockSpec((1, T, D_MODEL), lambda b, j: (b, j, 0)),
                  pl.BlockSpec((1, CONV_CARRY, D_FF), lambda b, j: (b, 0, 0))]
                 + [_const_spec(a.shape) for a in weights],
        out_specs=[pl.BlockSpec((1, T, D_MODEL), lambda b, j: (b, j, 0)),
                   pl.BlockSpec((1, CONV_CARRY, D_FF), lambda b, j: (b, 0, 0))],
        out_shape=[jax.ShapeDtypeStruct((bsz, seq, D_MODEL), F32),
                   jax.ShapeDtypeStruct((bsz, CONV_CARRY, D_FF), F32)],
        scratch_shapes=[pltpu.VMEM((CONV_CARRY, D_FF), F32)],
        compiler_params=pltpu.CompilerParams(dimension_semantics=("arbitrary", "arbitrary"),
                                             vmem_limit_bytes=VMEM_LIMIT),
        name="ffn_prompt",
    )(h, conv0, *weights)


def _bmm(spec, a, b):
    return jnp.einsum(spec, a.astype(BF16), b.astype(BF16), preferred_element_type=F32)


def _mixer_sample_kernel(x_ref, mk_ref, mv_ref, pool_ref, hg_ref, lb_ref, win_ref, bd_ref, pscale_ref, hgg_ref,
                         wbp_ref, wbh_ref, wbm_ref, wout_ref, g_ref, b_ref,
                         h_ref, newpool_ref, newhg_ref, *, layer, pos0, alpha, seq):
    Bb, S = SAMPLE_BB, SUBLANES
    R = Bb * S
    x = x_ref[...]
    xb = x.astype(BF16)
    proj = lambda c: jnp.dot(xb, win_ref[:, c[0]:c[1]], preferred_element_type=F32)

    u_a = proj(C_UA)
    seg = CARRY_ROWS + S
    full = jnp.concatenate([pool_ref[...], u_a.reshape(Bb, S, POOL_WIDTH)], axis=1).reshape(Bb * seg, POOL_WIDTH)
    tail = lambda z: z.reshape(Bb, seg, POOL_WIDTH)[:, CARRY_ROWS:, :].reshape(R, POOL_WIDTH)
    pos = pos0 + lax.broadcasted_iota(jnp.int32, (R, POOL_WIDTH), 0) % S
    y_a = _mm(_pool_diff(full, u_a, pos, tail), bd_ref[...]) * pscale_ref[...]
    newpool_ref[...] = pltpu.roll(full, Bb * seg - seq, 0).reshape(Bb, seg, POOL_WIDTH)[:, :CARRY_ROWS, :]

    live = lax.broadcasted_iota(jnp.int32, (R, HG_WIDTH), 0) % S < seq
    lb = _lower_bound(lb_ref[...], layer)
    logf, k = _forget_gates(proj(C_FB), lb)
    logf = jnp.where(live, logf, 0.0)
    k = jnp.where(live, k, 0.0)
    qb = proj(C_QB)
    q = qb * _sigmoid(qb)
    v = proj(C_IB)
    to3 = lambda z: z.reshape(Bb, S, z.shape[-1])
    b3 = to3(_select_mm(_as_bf16(_chunk_tri(R, S)), logf))
    bm = b3[:, S // 2:S // 2 + 1, :]
    bl = b3[:, S - 1:S, :]
    q3, k3, v3 = to3(q), to3(k), to3(v)
    qin = q3 * jnp.exp(b3)
    qd = q3 * jnp.exp(b3 - bm)
    kd = k3 * jnp.exp(bm - b3)
    kl = k3 * jnp.exp(bl - b3)
    rr = lax.broadcasted_iota(jnp.int32, (Bb, S, S), 1)
    cc = lax.broadcasted_iota(jnp.int32, (Bb, S, S), 2)
    causal = cc <= rr
    p1 = bl.astype(BF16).astype(F32)
    p2 = (bl - p1).astype(BF16).astype(F32)
    p3 = (bl - p1) - p2
    r3 = lax.broadcasted_iota(jnp.int32, (Bb, S, HG_WIDTH), 1)
    pieces = jnp.where(r3 == 0, p1, jnp.where(r3 == 1, p2, jnp.where(r3 == 2, p3, 0.0)))
    ones = jnp.ones((Bb, S, HG_VDIM), BF16)
    hgg = hgg_ref[...]
    heads = []
    for h in range(HG_HEADS):
        sl = slice(h * HG_KDIM, (h + 1) * HG_KDIM)
        sc = jnp.where(causal, _bmm('bqd,bkd->bqk', qd[:, :, sl], kd[:, :, sl]), 0.0)
        s0 = hg_ref[:, h]
        o = _bmm('bqk,bke->bqe', sc, v3[:, :, sl]) + _bmm('bqd,bde->bqe', qin[:, :, sl], s0)
        logdec = _bmm('bkd,bke->bde', pieces[:, :, sl], ones)
        newhg_ref[:, h] = jnp.exp(logdec) * s0 + _bmm('bkd,bke->bde', kl[:, :, sl], v3[:, :, sl])
        heads.append(_rms_head(o, hgg[:, sl]).reshape(R, HG_VDIM))
    gb = proj(C_GB)
    y_b = jnp.concatenate(heads, axis=1) * (gb * _sigmoid(gb))

    qc3 = to3(proj(C_QC))
    head_of_lane = lax.broadcasted_iota(jnp.int32, (Bb, S, MEM_WIDTH), 2) // MEM_HDIM
    q4 = jnp.concatenate([jnp.where(head_of_lane == h, qc3, 0.0) for h in range(MEM_HEADS)], axis=1)
    s = _bmm('bqd,bmd->bqm', q4, mk_ref[...]) * (MEM_HDIM ** -0.5)
    e = jnp.exp(s - jnp.max(s, axis=-1, keepdims=True))
    p = e / jnp.sum(e, axis=-1, keepdims=True)
    o4 = _bmm('bqm,bmd->bqd', p, mv_ref[...])
    y_c = jnp.zeros((Bb, S, MEM_WIDTH), F32)
    for h in range(MEM_HEADS):
        y_c = y_c + jnp.where(head_of_lane == h, o4[:, h * S:(h + 1) * S, :], 0.0)
    y_c = y_c.reshape(R, MEM_WIDTH)

    h_ref[...] = _merge_out(x, y_a, y_b, y_c, proj(C_GATE), wbp_ref[...], wbh_ref[...], wbm_ref[...], wout_ref[...],
                            g_ref[...], b_ref[...], alpha)


def _mixer_sample(x2d, mk, mv, pool16, hg, lb_logits, w, layer, pos0, alpha, seq):
    Bb, S = SAMPLE_BB, SUBLANES
    bsz = hg.shape[0]
    n_mem = mk.shape[1]
    b3 = lambda i: (i, 0, 0)
    weights = [lb_logits, w['w_in'], w['bd_pool'], w['pool_scale'], w['hg_norm_g'], w['w_br_pool'], w['w_br_hg'],
               w['w_br_mem'], w['w_out'], w['ln1_g'], w['ln1_b']]
    return pl.pallas_call(
        functools.partial(_mixer_sample_kernel, layer=layer, pos0=pos0, alpha=alpha, seq=seq),
        grid=(bsz // Bb,),
        in_specs=[pl.BlockSpec((Bb * S, D_MODEL), lambda i: (i, 0)),
                  pl.BlockSpec((Bb, n_mem, MEM_WIDTH), b3),
                  pl.BlockSpec((Bb, n_mem, MEM_WIDTH), b3),
                  pl.BlockSpec((Bb, CARRY_ROWS, POOL_WIDTH), b3),
                  pl.BlockSpec((Bb, HG_HEADS, HG_KDIM, HG_VDIM), lambda i: (i, 0, 0, 0))]
                 + [_const_spec(a.shape) for a in weights],
        out_specs=[pl.BlockSpec((Bb * S, D_MODEL), lambda i: (i, 0)),
                   pl.BlockSpec((Bb, CARRY_ROWS, POOL_WIDTH), b3),
                   pl.BlockSpec((Bb, HG_HEADS, HG_KDIM, HG_VDIM), lambda i: (i, 0, 0, 0))],
        out_shape=[jax.ShapeDtypeStruct((bsz * S, D_MODEL), F32),
                   jax.ShapeDtypeStruct((bsz, CARRY_ROWS, POOL_WIDTH), F32),
                   jax.ShapeDtypeStruct((bsz, HG_HEADS, HG_KDIM, HG_VDIM), F32)],
        compiler_params=pltpu.CompilerParams(dimension_semantics=("arbitrary",), vmem_limit_bytes=VMEM_LIMIT),
        name="mixer_sample",
    )(x2d, mk, mv, pool16, hg, *weights)


def _ffn_sample_kernel(h_ref, conv_ref, wg_ref, wu_ref, cw_ref, cb_ref, wd_ref, g_ref, b_ref,
                       y_ref, newconv_ref, *, alpha, seq):
    Bb, S = SAMPLE_FFN_BB, SUBLANES
    R = Bb * S
    seg = CONV_CARRY + S
    h = h_ref[...]
    hb = h.astype(BF16)
    a = jnp.dot(hb, wg_ref[...], preferred_element_type=F32)
    u = jnp.dot(hb, wu_ref[...], preferred_element_type=F32)
    full = jnp.concatenate([conv_ref[...], a.reshape(Bb, S, D_FF)], axis=1).reshape(Bb * seg, D_FF)
    tail = lambda z: z.reshape(Bb, seg, D_FF)[:, CONV_CARRY:, :].reshape(R, D_FF)
    ff = _conv_ffn(h, full, tail, None, u, cw_ref[...], cb_ref[...], wd_ref[...])
    y_ref[...] = _layer_norm(alpha * h + ff, g_ref[...], b_ref[...])
    newconv_ref[...] = pltpu.roll(full, Bb * seg - seq, 0).reshape(Bb, seg, D_FF)[:, :CONV_CARRY, :]


def _ffn_sample(h2d, conv8, w, alpha, seq):
    Bb, S = SAMPLE_FFN_BB, SUBLANES
    bsz = conv8.shape[0]
    weights = [w['w_gate'], w['w_up'], w['conv_w'], w['conv_b'], w['w_down'], w['ln2_g'], w['ln2_b']]
    return pl.pallas_call(
        functools.partial(_ffn_sample_kernel, alpha=alpha, seq=seq),
        grid=(bsz // Bb,),
        in_specs=[pl.BlockSpec((Bb * S, D_MODEL), lambda i: (i, 0)),
                  pl.BlockSpec((Bb, CONV_CARRY, D_FF), lambda i: (i, 0, 0))]
                 + [_const_spec(a.shape) for a in weights],
        out_specs=[pl.BlockSpec((Bb * S, D_MODEL), lambda i: (i, 0)),
                   pl.BlockSpec((Bb, CONV_CARRY, D_FF), lambda i: (i, 0, 0))],
        out_shape=[jax.ShapeDtypeStruct((bsz * S, D_MODEL), F32),
                   jax.ShapeDtypeStruct((bsz, CONV_CARRY, D_FF), F32)],
        compiler_params=pltpu.CompilerParams(dimension_semantics=("arbitrary",), vmem_limit_bytes=VMEM_LIMIT),
        name="ffn_sample",
    )(h2d, conv8, *weights)


def _block_diag(w_grp):
    groups, gdim, _ = w_grp.shape
    out = jnp.zeros((groups * gdim, groups * gdim), w_grp.dtype)
    for g in range(groups):
        out = lax.dynamic_update_slice(out, w_grp[g], (g * gdim, g * gdim))
    return out


def _front_pad(a, rows):
    return jnp.pad(a, ((0, 0), (rows - a.shape[1], 0), (0, 0)))


def kernel(x_prompt, x_sample, state_pool, state_hgrn, state_ffn_conv, cache_mem_k, cache_mem_v, mem_prompt, lb_logits, w_in, w_pool_grp, pool_scale, hg_norm_g, w_mem_k, w_mem_v, w_br_pool, w_br_hg, w_br_mem, w_out, ln1_g, ln1_b, w_gate, w_up, conv_w, conv_b, w_down, ln2_g, ln2_b):
    depth = w_in.shape[0]
    alpha = (2 * depth) ** 0.25
    n_prompt, seq_p, _ = x_prompt.shape
    n_sample, seq_s, _ = x_sample.shape
    n_mem = mem_prompt.shape[1]
    past_len = 16384
    assert seq_p % PROMPT_T == 0 and seq_s <= SUBLANES and n_sample % SAMPLE_BB == 0 and n_sample % SAMPLE_FFN_BB == 0

    hp = x_prompt
    hs = jnp.pad(x_sample, ((0, 0), (0, SUBLANES - seq_s), (0, 0))).reshape(n_sample * SUBLANES, D_MODEL)
    mem2d = mem_prompt.reshape(n_prompt * n_mem, D_MODEL)
    row = lambda a: a.reshape(1, -1)
    outs = [[] for _ in range(8)]
    for l in range(depth):
        w = {'w_in': w_in[l].astype(BF16), 'bd_pool': _block_diag(w_pool_grp[l]).astype(BF16),
             'pool_scale': row(pool_scale[l]), 'hg_norm_g': row(hg_norm_g[l]),
             'w_br_pool': w_br_pool[l].astype(BF16), 'w_br_hg': w_br_hg[l].astype(BF16),
             'w_br_mem': w_br_mem[l].astype(BF16), 'w_out': w_out[l].astype(BF16),
             'ln1_g': row(ln1_g[l]), 'ln1_b': row(ln1_b[l]),
             'w_gate': w_gate[l].astype(BF16), 'w_up': w_up[l].astype(BF16), 'conv_w': conv_w[l],
             'conv_b': row(conv_b[l]), 'w_down': w_down[l].astype(BF16),
             'ln2_g': row(ln2_g[l]), 'ln2_b': row(ln2_b[l])}
        mk, mv = _memkv(mem2d, w_mem_k[l].astype(BF16), w_mem_v[l].astype(BF16), n_mem)
        mk3 = mk.reshape(n_prompt, n_mem, MEM_WIDTH)
        mv3 = mv.reshape(n_prompt, n_mem, MEM_WIDTH)
        h_mid, pool_p, hg_p = _mixer_prompt(
            hp, mk3, mv3, jnp.zeros((n_prompt, CARRY_ROWS, POOL_WIDTH), F32),
            jnp.zeros((n_prompt, HG_HEADS, HG_KDIM, HG_VDIM), F32), lb_logits, w, l, 0, alpha)
        hp, conv_p = _ffn_prompt(h_mid, jnp.zeros((n_prompt, CONV_CARRY, D_FF), F32), w, alpha)
        outs[0].append(pool_p[:, CARRY_ROWS - POOL_BUF:])
        outs[1].append(hg_p)
        outs[2].append(conv_p[:, CONV_CARRY - (CONV_W - 1):])
        outs[3].append(mk.reshape(n_prompt, n_mem, MEM_HEADS, MEM_HDIM))
        outs[4].append(mv.reshape(n_prompt, n_mem, MEM_HEADS, MEM_HDIM))
        hs_mid, pool_s, hg_s = _mixer_sample(
            hs, cache_mem_k[l].reshape(n_sample, n_mem, MEM_WIDTH), cache_mem_v[l].reshape(n_sample, n_mem, MEM_WIDTH),
            _front_pad(state_pool[l], CARRY_ROWS), state_hgrn[l], lb_logits, w, l, past_len, alpha, seq_s)
        hs, conv_s = _ffn_sample(hs_mid, _front_pad(state_ffn_conv[l], CONV_CARRY), w, alpha, seq_s)
        outs[5].append(pool_s[:, CARRY_ROWS - POOL_BUF:])
        outs[6].append(hg_s)
        outs[7].append(conv_s[:, CONV_CARRY - (CONV_W - 1):])
    y_sample = hs.reshape(n_sample, SUBLANES, D_MODEL)[:, :seq_s]
    return (hp, y_sample) + tuple(jnp.stack(o) for o in outs)
```

```python
import functools

import jax
import jax.numpy as jnp
from jax import lax
from jax.experimental import pallas as pl
from jax.experimental.pallas import tpu as pltpu

F32 = jnp.float32
BF16 = jnp.bfloat16

D_MODEL = 1024
POOL_WIDTH = 256
POOL_WINDOWS = (2, 4, 8, 16)
POOL_GDIM = 64
POOL_BUF = 15
HG_HEADS = 4
HG_KDIM = 128
HG_WIDTH = 512
HG_VDIM = 128
MEM_HEADS = 4
MEM_HDIM = 64
MEM_WIDTH = 256
N_BRANCH = 3
D_FF = 2816
CONV_W = 3
LN_EPS = 1e-5
RMS_EPS = 1e-6

C_UA = (0, 256)
C_QB = (256, 768)
C_FB = (768, 1280)
C_IB = (1280, 1792)
C_GB = (1792, 2304)
C_QC = (2304, 2560)
C_GATE = (2560, 5632)

SUBLANES = 8
CARRY_ROWS = 16
CONV_CARRY = 8
PROMPT_T = 512
HG_BLOCK = 256
HG_CHUNK = 64
SAMPLE_BB = 16
SAMPLE_FFN_BB = 32
VMEM_LIMIT = 56 * 1024 * 1024


def _mm(a, b):
    return jnp.dot(a.astype(BF16), b.astype(BF16), preferred_element_type=F32)


def _mm_nt(a, b):
    return lax.dot_general(a.astype(BF16), b.astype(BF16), (((1,), (1,)), ((), ())), preferred_element_type=F32)


def _mm_tn(a, b):
    return lax.dot_general(a.astype(BF16), b.astype(BF16), (((0,), (0,)), ((), ())), preferred_element_type=F32)


def _split3(x):
    h1 = x.astype(BF16)
    r1 = x - h1.astype(F32)
    h2 = r1.astype(BF16)
    h3 = (r1 - h2.astype(F32)).astype(BF16)
    return h1, h2, h3


def _select_mm(sel, x):
    h1, h2, h3 = _split3(x)
    d = lambda p: jnp.dot(sel, p, preferred_element_type=F32)
    return d(h1) + d(h2) + d(h3)


def _sigmoid(x):
    return 0.5 + 0.5 * jnp.tanh(0.5 * x)


def _layer_norm(x, g, b):
    mu = jnp.mean(x, axis=-1, keepdims=True)
    xc = x - mu
    var = jnp.mean(xc * xc, axis=-1, keepdims=True)
    return xc * lax.rsqrt(var + LN_EPS) * g + b


def _gelu(x):
    return 0.5 * x * (1.0 + lax.erf(x * (2.0 ** -0.5)))


def _lower_bound(lb_logits, layer):
    m = jnp.max(lb_logits, axis=0, keepdims=True)
    e = jnp.exp(lb_logits - m)
    sm = e / jnp.sum(e, axis=0, keepdims=True)
    return jnp.sum(sm[:layer + 1], axis=0, keepdims=True)


def _pool_diff(full, u_a, pos, tail):
    s2 = full + pltpu.roll(full, 1, 0)
    s4 = s2 + pltpu.roll(s2, 2, 0)
    s8 = s4 + pltpu.roll(s4, 4, 0)
    s16 = s8 + pltpu.roll(s8, 8, 0)
    grp = lax.broadcasted_iota(jnp.int32, u_a.shape, 1) // POOL_GDIM
    wsum = jnp.where(grp == 0, tail(s2), jnp.where(grp == 1, tail(s4), jnp.where(grp == 2, tail(s8), tail(s16))))
    wlen = jnp.where(grp == 0, 2, jnp.where(grp == 1, 4, jnp.where(grp == 2, 8, 16)))
    count = jnp.minimum(pos + 1, wlen).astype(F32)
    return wsum / count - u_a


def _forget_gates(fb, lb):
    th = jnp.tanh(0.5 * fb)
    f = lb + (1.0 - lb) * (0.5 + 0.5 * th)
    k = (1.0 - lb) * (0.5 - 0.5 * th)
    return jnp.log(f), k


def _chunk_tri(n, chunk):
    r = lax.broadcasted_iota(jnp.int32, (n, n), 0)
    c = lax.broadcasted_iota(jnp.int32, (n, n), 1)
    return (r // chunk == c // chunk) & (c <= r)


def _as_bf16(mask):
    return jnp.where(mask, 1.0, 0.0).astype(BF16)


def _rms_head(o, g):
    return o * lax.rsqrt(jnp.mean(o * o, axis=-1, keepdims=True) + RMS_EPS) * g


def _merge_out(x, y_a, y_b, y_c, gate_pre, wbp, wbh, wbm, wout, g, b, alpha):
    gates = _sigmoid(gate_pre)
    merged = (gates[:, 0:D_MODEL] * _mm(y_a, wbp)
              + gates[:, D_MODEL:2 * D_MODEL] * _mm(y_b, wbh)
              + gates[:, 2 * D_MODEL:3 * D_MODEL] * _mm(y_c, wbm))
    return _layer_norm(alpha * x + _mm(merged, wout), g, b)


def _memkv_kernel(mem_ref, wkt_ref, wvt_ref, kt_ref, vt_ref):
    m = mem_ref[...].astype(BF16)
    kt_ref[0] = _mm_nt(wkt_ref[...], m)
    vt_ref[0] = _mm_nt(wvt_ref[...], m)


def _memkv(mem2d, wkt, wvt, n_mem):
    n_seq = mem2d.shape[0] // n_mem
    const = lambda i: (0, 0)
    per_seq = lambda i: (i, 0, 0)
    return pl.pallas_call(
        _memkv_kernel,
        grid=(n_seq,),
        in_specs=[pl.BlockSpec((n_mem, D_MODEL), lambda i: (i, 0)),
                  pl.BlockSpec((MEM_WIDTH, D_MODEL), const),
                  pl.BlockSpec((MEM_WIDTH, D_MODEL), const)],
        out_specs=[pl.BlockSpec((1, MEM_WIDTH, n_mem), per_seq),
                   pl.BlockSpec((1, MEM_WIDTH, n_mem), per_seq)],
        out_shape=[jax.ShapeDtypeStruct((n_seq, MEM_WIDTH, n_mem), F32)] * 2,
        name="memkv",
    )(mem2d, wkt, wvt)


def _feature_major(mem):
    return jnp.transpose(mem, (0, 2, 3, 1)).reshape(mem.shape[0], MEM_WIDTH, mem.shape[1])


def _token_major(mem_t):
    bsz, _, n_mem = mem_t.shape
    return jnp.transpose(mem_t.reshape(bsz, MEM_HEADS, MEM_HDIM, n_mem), (0, 3, 1, 2))


def _mixer_prompt_kernel(x_ref, mk_ref, mv_ref, pool0_ref, hg0_ref, lb_ref, win_ref, bd_ref, pscale_ref, hgg_ref,
                         wbp_ref, wbh_ref, wbm_ref, wout_ref, g_ref, b_ref,
                         h_ref, newpool_ref, newhg_ref, st_scr, pool_scr, *, layer, pos0, alpha):
    T, C = PROMPT_T, HG_CHUNK
    j = pl.program_id(1)

    @pl.when(j == 0)
    def _():
        pool_scr[...] = pool0_ref[0]
        for h in range(HG_HEADS):
            st_scr[h] = hg0_ref[0, h].T

    x = x_ref[0]
    xb = x.astype(BF16)
    proj = lambda c: jnp.dot(xb, win_ref[:, c[0]:c[1]], preferred_element_type=F32)

    lb = _lower_bound(lb_ref[...], layer)
    logf, k = _forget_gates(proj(C_FB), lb)
    qb = proj(C_QB)
    v = proj(C_IB)
    u_a = proj(C_UA)
    qc = proj(C_QC)
    same_chunk_causal = _chunk_tri(HG_BLOCK, C)
    tri = _as_bf16(same_chunk_causal)
    blocks = [slice(i * HG_BLOCK, (i + 1) * HG_BLOCK) for i in range(T // HG_BLOCK)]
    bcum = jnp.concatenate([_select_mm(tri, logf[bs]) for bs in blocks], axis=0)
    gb = proj(C_GB)
    gate_a = proj((C_GATE[0], C_GATE[0] + D_MODEL))

    full = jnp.concatenate([pool_scr[...], u_a], axis=0)
    pos = pos0 + j * T + lax.broadcasted_iota(jnp.int32, (T, POOL_WIDTH), 0)
    diff_a = _pool_diff(full, u_a, pos, lambda z: z[CARRY_ROWS:])
    pool_scr[...] = full[T:]

    mk = mk_ref[0].astype(BF16)
    mv = mv_ref[0].astype(BF16)
    head_of_lane = lax.broadcasted_iota(jnp.int32, (T, MEM_WIDTH), 1) // MEM_HDIM
    att = [_mm(jnp.where(head_of_lane == h, qc, 0.0), mk) * (MEM_HDIM ** -0.5) for h in range(MEM_HEADS)]
    gate_b = proj((C_GATE[0] + D_MODEL, C_GATE[0] + 2 * D_MODEL))

    q = qb * _sigmoid(qb)
    n_chunks = T // C
    chunk_rows = [slice(c * C, (c + 1) * C) for c in range(n_chunks)]
    per_chunk = lambda row: jnp.concatenate(
        [jnp.broadcast_to(bcum[c * C + row:c * C + row + 1], (C, HG_WIDTH)) for c in range(n_chunks)], axis=0)
    bm = per_chunk(C // 2)
    bl = per_chunk(C - 1)
    qin = q * jnp.exp(bcum)
    qd = q * jnp.exp(bcum - bm)
    kd = k * jnp.exp(bm - bcum)
    kl = k * jnp.exp(bl - bcum)
    dec = [jnp.exp(bcum[c * C + C - 1:(c + 1) * C]) for c in range(n_chunks)]
    hgg = hgg_ref[...]
    head_lanes = [slice(h * HG_KDIM, (h + 1) * HG_KDIM) for h in range(HG_HEADS)]
    sc = [[jnp.where(same_chunk_causal, _mm_nt(qd[bs, sl], kd[bs, sl]), 0.0) for bs in blocks] for sl in head_lanes]
    grow = [[_mm_tn(v[rs, sl], kl[rs, sl]) for rs in chunk_rows] for sl in head_lanes]
    y_a = _mm(diff_a, bd_ref[...]) * pscale_ref[...]
    gate_c = proj((C_GATE[0] + 2 * D_MODEL, C_GATE[1]))

    y_c = jnp.zeros((T, MEM_WIDTH), F32)
    for h in range(MEM_HEADS):
        e = jnp.exp(att[h] - jnp.max(att[h], axis=-1, keepdims=True))
        p = e / jnp.sum(e, axis=-1, keepdims=True)
        y_c = y_c + jnp.where(head_of_lane == h, _mm_nt(p, mv), 0.0)
    m_a = _mm(y_a, wbp_ref[...])

    heads = []
    for h, sl in enumerate(head_lanes):
        o_intra = jnp.concatenate([_mm(sc[h][i], v[bs, sl]) for i, bs in enumerate(blocks)], axis=0)
        st = st_scr[h]
        o_inter = []
        for c, rs in enumerate(chunk_rows):
            o_inter.append(_mm_nt(qin[rs, sl], st))
            st = st * dec[c][:, sl] + grow[h][c]
        st_scr[h] = st
        heads.append(_rms_head(o_intra + jnp.concatenate(o_inter, axis=0), hgg[:, sl]))
    m_c = _mm(y_c, wbm_ref[...])
    y_b = jnp.concatenate(heads, axis=1) * (gb * _sigmoid(gb))
    merged = _sigmoid(gate_a) * m_a + _sigmoid(gate_c) * m_c + _sigmoid(gate_b) * _mm(y_b, wbh_ref[...])
    h_ref[0] = _layer_norm(alpha * x + _mm(merged, wout_ref[...]), g_ref[...], b_ref[...])

    @pl.when(j == pl.num_programs(1) - 1)
    def _():
        newpool_ref[0] = pool_scr[...]
        for h in range(HG_HEADS):
            newhg_ref[0, h] = st_scr[h].T


def _const_spec(shape):
    n = len(shape)
    return pl.BlockSpec(shape, lambda *_: (0,) * n, pipeline_mode=pl.Buffered(1))


def _mixer_prompt(x, mk, mv, pool0, hg0, lb_logits, w, layer, pos0, alpha):
    bsz, seq, _ = x.shape
    T = PROMPT_T
    per_b3 = lambda b, j: (b, 0, 0)
    kern = functools.partial(_mixer_prompt_kernel, layer=layer, pos0=pos0, alpha=alpha)
    weights = [lb_logits, w['w_in'], w['bd_pool'], w['pool_scale'], w['hg_norm_g'], w['w_br_pool'], w['w_br_hg'],
               w['w_br_mem'], w['w_out'], w['ln1_g'], w['ln1_b']]
    return pl.pallas_call(
        kern,
        grid=(bsz, seq // T),
        in_specs=[pl.BlockSpec((1, T, D_MODEL), lambda b, j: (b, j, 0)),
                  pl.BlockSpec((1,) + mk.shape[1:], per_b3),
                  pl.BlockSpec((1,) + mv.shape[1:], per_b3),
                  pl.BlockSpec((1, CARRY_ROWS, POOL_WIDTH), per_b3),
                  pl.BlockSpec((1, HG_HEADS, HG_KDIM, HG_VDIM), lambda b, j: (b, 0, 0, 0))]
                 + [_const_spec(a.shape) for a in weights],
        out_specs=[pl.BlockSpec((1, T, D_MODEL), lambda b, j: (b, j, 0)),
                   pl.BlockSpec((1, CARRY_ROWS, POOL_WIDTH), per_b3),
                   pl.BlockSpec((1, HG_HEADS, HG_KDIM, HG_VDIM), lambda b, j: (b, 0, 0, 0))],
        out_shape=[jax.ShapeDtypeStruct((bsz, seq, D_MODEL), F32),
                   jax.ShapeDtypeStruct((bsz, CARRY_ROWS, POOL_WIDTH), F32),
                   jax.ShapeDtypeStruct((bsz, HG_HEADS, HG_KDIM, HG_VDIM), F32)],
        scratch_shapes=[pltpu.VMEM((HG_HEADS, HG_VDIM, HG_KDIM), F32),
                        pltpu.VMEM((CARRY_ROWS, POOL_WIDTH), F32)],
        compiler_params=pltpu.CompilerParams(dimension_semantics=("arbitrary", "arbitrary"),
                                             vmem_limit_bytes=VMEM_LIMIT),
        name="mixer_prompt",
    )(x, mk, mv, pool0, hg0, *weights)


def _conv_ffn(h, full, tail, wg_unused, u, cw, cb, wd):
    c = cb + pltpu.roll(full, 2, 0) * cw[0:1] + pltpu.roll(full, 1, 0) * cw[1:2] + full * cw[2:3]
    act = _gelu(tail(c))
    return _mm(act * u, wd)


def _ffn_prompt_kernel(h_ref, conv0_ref, wg_ref, wu_ref, cw_ref, cb_ref, wd_ref, g_ref, b_ref,
                       y_ref, newconv_ref, carry_scr, *, alpha):
    T = PROMPT_T
    j = pl.program_id(1)

    @pl.when(j == 0)
    def _():
        carry_scr[...] = conv0_ref[0]

    h = h_ref[0]
    hb = h.astype(BF16)
    a = jnp.dot(hb, wg_ref[...], preferred_element_type=F32)
    u = jnp.dot(hb, wu_ref[...], preferred_element_type=F32)
    full = jnp.concatenate([carry_scr[...], a], axis=0)
    ff = _conv_ffn(h, full, lambda z: z[CONV_CARRY:], None, u, cw_ref[...], cb_ref[...], wd_ref[...])
    y_ref[0] = _layer_norm(alpha * h + ff, g_ref[...], b_ref[...])
    carry_scr[...] = a[T - CONV_CARRY:]

    @pl.when(j == pl.num_programs(1) - 1)
    def _():
        newconv_ref[0] = carry_scr[...]


def _ffn_prompt(h, conv0, w, alpha):
    bsz, seq, _ = h.shape
    T = PROMPT_T
    weights = [w['w_gate'], w['w_up'], w['conv_w'], w['conv_b'], w['w_down'], w['ln2_g'], w['ln2_b']]
    return pl.pallas_call(
        functools.partial(_ffn_prompt_kernel, alpha=alpha),
        grid=(bsz, seq // T),
        in_specs=[pl.BlockSpec((1, T, D_MODEL), lambda b, j: (b, j, 0)),
                  pl.BlockSpec((1, CONV_CARRY, D_FF), lambda b, j: (b, 0, 0))]
                 + [_const_spec(a.shape) for a in weights],
        out_specs=[pl.BlockSpec((1, T, D_MODEL), lambda b, j: (b, j, 0)),
                   pl.BlockSpec((1, CONV_CARRY, D_FF), lambda b, j: (b, 0, 0))],
        out_shape=[jax.ShapeDtypeStruct((bsz, seq, D_MODEL), F32),
                   jax.ShapeDtypeStruct((bsz, CONV_CARRY, D_FF), F32)],
        scratch_shapes=[pltpu.VMEM((CONV_CARRY, D_FF), F32)],
        compiler_params=pltpu.CompilerParams(dimension_semantics=("arbitrary", "arbitrary"),
                                             vmem_limit_bytes=VMEM_LIMIT),
        name="ffn_prompt",
    )(h, conv0, *weights)


def _bmm(spec, a, b):
    return jnp.einsum(spec, a.astype(BF16), b.astype(BF16), preferred_element_type=F32)


def _mixer_sample_kernel(x_ref, mk_ref, mv_ref, pool_ref, hg_ref, lb_ref, win_ref, bd_ref, pscale_ref, hgg_ref,
                         wbp_ref, wbh_ref, wbm_ref, wout_ref, g_ref, b_ref,
                         h_ref, newpool_ref, newhg_ref, *, layer, pos0, alpha, seq):
    Bb, S = SAMPLE_BB, SUBLANES
    R = Bb * S
    x = x_ref[...]
    xb = x.astype(BF16)
    proj = lambda c: jnp.dot(xb, win_ref[:, c[0]:c[1]], preferred_element_type=F32)

    u_a = proj(C_UA)
    seg = CARRY_ROWS + S
    full = jnp.concatenate([pool_ref[...], u_a.reshape(Bb, S, POOL_WIDTH)], axis=1).reshape(Bb * seg, POOL_WIDTH)
    tail = lambda z: z.reshape(Bb, seg, POOL_WIDTH)[:, CARRY_ROWS:, :].reshape(R, POOL_WIDTH)
    pos = pos0 + lax.broadcasted_iota(jnp.int32, (R, POOL_WIDTH), 0) % S
    y_a = _mm(_pool_diff(full, u_a, pos, tail), bd_ref[...]) * pscale_ref[...]
    newpool_ref[...] = pltpu.roll(full, Bb * seg - seq, 0).reshape(Bb, seg, POOL_WIDTH)[:, :CARRY_ROWS, :]

    live = lax.broadcasted_iota(jnp.int32, (R, HG_WIDTH), 0) % S < seq
    lb = _lower_bound(lb_ref[...], layer)
    logf, k = _forget_gates(proj(C_FB), lb)
    logf = jnp.where(live, logf, 0.0)
    k = jnp.where(live, k, 0.0)
    qb = proj(C_QB)
    q = qb * _sigmoid(qb)
    v = proj(C_IB)
    to3 = lambda z: z.reshape(Bb, S, z.shape[-1])
    b3 = to3(_select_mm(_as_bf16(_chunk_tri(R, S)), logf))
    bm = b3[:, S // 2:S // 2 + 1, :]
    bl = b3[:, S - 1:S, :]
    q3, k3, v3 = to3(q), to3(k), to3(v)
    qin = q3 * jnp.exp(b3)
    qd = q3 * jnp.exp(b3 - bm)
    kd = k3 * jnp.exp(bm - b3)
    kl = k3 * jnp.exp(bl - b3)
    rr = lax.broadcasted_iota(jnp.int32, (Bb, S, S), 1)
    cc = lax.broadcasted_iota(jnp.int32, (Bb, S, S), 2)
    causal = cc <= rr
    p1 = bl.astype(BF16).astype(F32)
    p2 = (bl - p1).astype(BF16).astype(F32)
    p3 = (bl - p1) - p2
    r3 = lax.broadcasted_iota(jnp.int32, (Bb, S, HG_WIDTH), 1)
    pieces = jnp.where(r3 == 0, p1, jnp.where(r3 == 1, p2, jnp.where(r3 == 2, p3, 0.0)))
    ones = jnp.ones((Bb, S, HG_VDIM), BF16)
    hgg = hgg_ref[...]
    heads = []
    for h in range(HG_HEADS):
        sl = slice(h * HG_KDIM, (h + 1) * HG_KDIM)
        sc = jnp.where(causal, _bmm('bqd,bkd->bqk', qd[:, :, sl], kd[:, :, sl]), 0.0)
        s0 = hg_ref[:, h]
        o = _bmm('bqk,bke->bqe', sc, v3[:, :, sl]) + _bmm('bqd,bde->bqe', qin[:, :, sl], s0)
        logdec = _bmm('bkd,bke->bde', pieces[:, :, sl], ones)
        newhg_ref[:, h] = jnp.exp(logdec) * s0 + _bmm('bkd,bke->bde', kl[:, :, sl], v3[:, :, sl])
        heads.append(_rms_head(o, hgg[:, sl]).reshape(R, HG_VDIM))
    gb = proj(C_GB)
    y_b = jnp.concatenate(heads, axis=1) * (gb * _sigmoid(gb))

    qc3 = to3(proj(C_QC))
    head_of_lane = lax.broadcasted_iota(jnp.int32, (Bb, S, MEM_WIDTH), 2) // MEM_HDIM
    q4 = jnp.concatenate([jnp.where(head_of_lane == h, qc3, 0.0) for h in range(MEM_HEADS)], axis=1)
    s = _bmm('bqd,bdm->bqm', q4, mk_ref[...]) * (MEM_HDIM ** -0.5)
    e = jnp.exp(s - jnp.max(s, axis=-1, keepdims=True))
    p = e / jnp.sum(e, axis=-1, keepdims=True)
    o4 = _bmm('bqm,bdm->bqd', p, mv_ref[...])
    y_c = jnp.zeros((Bb, S, MEM_WIDTH), F32)
    for h in range(MEM_HEADS):
        y_c = y_c + jnp.where(head_of_lane == h, o4[:, h * S:(h + 1) * S, :], 0.0)
    y_c = y_c.reshape(R, MEM_WIDTH)

    h_ref[...] = _merge_out(x, y_a, y_b, y_c, proj(C_GATE), wbp_ref[...], wbh_ref[...], wbm_ref[...], wout_ref[...],
                            g_ref[...], b_ref[...], alpha)


def _mixer_sample(x2d, mk, mv, pool16, hg, lb_logits, w, layer, pos0, alpha, seq):
    Bb, S = SAMPLE_BB, SUBLANES
    bsz = hg.shape[0]
    n_mem = mk.shape[2]
    b3 = lambda i: (i, 0, 0)
    weights = [lb_logits, w['w_in'], w['bd_pool'], w['pool_scale'], w['hg_norm_g'], w['w_br_pool'], w['w_br_hg'],
               w['w_br_mem'], w['w_out'], w['ln1_g'], w['ln1_b']]
    return pl.pallas_call(
        functools.partial(_mixer_sample_kernel, layer=layer, pos0=pos0, alpha=alpha, seq=seq),
        grid=(bsz // Bb,),
        in_specs=[pl.BlockSpec((Bb * S, D_MODEL), lambda i: (i, 0)),
                  pl.BlockSpec((Bb, MEM_WIDTH, n_mem), b3),
                  pl.BlockSpec((Bb, MEM_WIDTH, n_mem), b3),
                  pl.BlockSpec((Bb, CARRY_ROWS, POOL_WIDTH), b3),
                  pl.BlockSpec((Bb, HG_HEADS, HG_KDIM, HG_VDIM), lambda i: (i, 0, 0, 0))]
                 + [_const_spec(a.shape) for a in weights],
        out_specs=[pl.BlockSpec((Bb * S, D_MODEL), lambda i: (i, 0)),
                   pl.BlockSpec((Bb, CARRY_ROWS, POOL_WIDTH), b3),
                   pl.BlockSpec((Bb, HG_HEADS, HG_KDIM, HG_VDIM), lambda i: (i, 0, 0, 0))],
        out_shape=[jax.ShapeDtypeStruct((bsz * S, D_MODEL), F32),
                   jax.ShapeDtypeStruct((bsz, CARRY_ROWS, POOL_WIDTH), F32),
                   jax.ShapeDtypeStruct((bsz, HG_HEADS, HG_KDIM, HG_VDIM), F32)],
        compiler_params=pltpu.CompilerParams(dimension_semantics=("arbitrary",), vmem_limit_bytes=VMEM_LIMIT),
        name="mixer_sample",
    )(x2d, mk, mv, pool16, hg, *weights)


def _ffn_sample_kernel(h_ref, conv_ref, wg_ref, wu_ref, cw_ref, cb_ref, wd_ref, g_ref, b_ref,
                       y_ref, newconv_ref, *, alpha, seq):
    Bb, S = SAMPLE_FFN_BB, SUBLANES
    R = Bb * S
    seg = CONV_CARRY + S
    h = h_ref[...]
    hb = h.astype(BF16)
    a = jnp.dot(hb, wg_ref[...], preferred_element_type=F32)
    u = jnp.dot(hb, wu_ref[...], preferred_element_type=F32)
    full = jnp.concatenate([conv_ref[...], a.reshape(Bb, S, D_FF)], axis=1).reshape(Bb * seg, D_FF)
    tail = lambda z: z.reshape(Bb, seg, D_FF)[:, CONV_CARRY:, :].reshape(R, D_FF)
    ff = _conv_ffn(h, full, tail, None, u, cw_ref[...], cb_ref[...], wd_ref[...])
    y_ref[...] = _layer_norm(alpha * h + ff, g_ref[...], b_ref[...])
    newconv_ref[...] = pltpu.roll(full, Bb * seg - seq, 0).reshape(Bb, seg, D_FF)[:, :CONV_CARRY, :]


def _ffn_sample(h2d, conv8, w, alpha, seq):
    Bb, S = SAMPLE_FFN_BB, SUBLANES
    bsz = conv8.shape[0]
    weights = [w['w_gate'], w['w_up'], w['conv_w'], w['conv_b'], w['w_down'], w['ln2_g'], w['ln2_b']]
    return pl.pallas_call(
        functools.partial(_ffn_sample_kernel, alpha=alpha, seq=seq),
        grid=(bsz // Bb,),
        in_specs=[pl.BlockSpec((Bb * S, D_MODEL), lambda i: (i, 0)),
                  pl.BlockSpec((Bb, CONV_CARRY, D_FF), lambda i: (i, 0, 0))]
                 + [_const_spec(a.shape) for a in weights],
        out_specs=[pl.BlockSpec((Bb * S, D_MODEL), lambda i: (i, 0)),
                   pl.BlockSpec((Bb, CONV_CARRY, D_FF), lambda i: (i, 0, 0))],
        out_shape=[jax.ShapeDtypeStruct((bsz * S, D_MODEL), F32),
                   jax.ShapeDtypeStruct((bsz, CONV_CARRY, D_FF), F32)],
        compiler_params=pltpu.CompilerParams(dimension_semantics=("arbitrary",), vmem_limit_bytes=VMEM_LIMIT),
        name="ffn_sample",
    )(h2d, conv8, *weights)


def _block_diag(w_grp):
    groups, gdim, _ = w_grp.shape
    out = jnp.zeros((groups * gdim, groups * gdim), w_grp.dtype)
    for g in range(groups):
        out = lax.dynamic_update_slice(out, w_grp[g], (g * gdim, g * gdim))
    return out


def _front_pad(a, rows):
    return jnp.pad(a, ((0, 0), (rows - a.shape[1], 0), (0, 0)))


def kernel(x_prompt, x_sample, state_pool, state_hgrn, state_ffn_conv, cache_mem_k, cache_mem_v, mem_prompt, lb_logits, w_in, w_pool_grp, pool_scale, hg_norm_g, w_mem_k, w_mem_v, w_br_pool, w_br_hg, w_br_mem, w_out, ln1_g, ln1_b, w_gate, w_up, conv_w, conv_b, w_down, ln2_g, ln2_b):
    depth = w_in.shape[0]
    alpha = (2 * depth) ** 0.25
    n_prompt, seq_p, _ = x_prompt.shape
    n_sample, seq_s, _ = x_sample.shape
    n_mem = mem_prompt.shape[1]
    past_len = 16384
    assert seq_p % PROMPT_T == 0 and seq_s <= SUBLANES and n_sample % SAMPLE_BB == 0 and n_sample % SAMPLE_FFN_BB == 0

    hp = x_prompt
    hs = jnp.pad(x_sample, ((0, 0), (0, SUBLANES - seq_s), (0, 0))).reshape(n_sample * SUBLANES, D_MODEL)
    mem2d = mem_prompt.reshape(n_prompt * n_mem, D_MODEL)
    row = lambda a: a.reshape(1, -1)
    outs = [[] for _ in range(8)]
    for l in range(depth):
        w = {'w_in': w_in[l].astype(BF16), 'bd_pool': _block_diag(w_pool_grp[l]).astype(BF16),
             'pool_scale': row(pool_scale[l]), 'hg_norm_g': row(hg_norm_g[l]),
             'w_br_pool': w_br_pool[l].astype(BF16), 'w_br_hg': w_br_hg[l].astype(BF16),
             'w_br_mem': w_br_mem[l].astype(BF16), 'w_out': w_out[l].astype(BF16),
             'ln1_g': row(ln1_g[l]), 'ln1_b': row(ln1_b[l]),
             'w_gate': w_gate[l].astype(BF16), 'w_up': w_up[l].astype(BF16), 'conv_w': conv_w[l],
             'conv_b': row(conv_b[l]), 'w_down': w_down[l].astype(BF16),
             'ln2_g': row(ln2_g[l]), 'ln2_b': row(ln2_b[l])}
        mkt, mvt = _memkv(mem2d, w_mem_k[l].T.astype(BF16), w_mem_v[l].T.astype(BF16), n_mem)
        h_mid, pool_p, hg_p = _mixer_prompt(
            hp, mkt, mvt, jnp.zeros((n_prompt, CARRY_ROWS, POOL_WIDTH), F32),
            jnp.zeros((n_prompt, HG_HEADS, HG_KDIM, HG_VDIM), F32), lb_logits, w, l, 0, alpha)
        hp, conv_p = _ffn_prompt(h_mid, jnp.zeros((n_prompt, CONV_CARRY, D_FF), F32), w, alpha)
        outs[0].append(pool_p[:, CARRY_ROWS - POOL_BUF:])
        outs[1].append(hg_p)
        outs[2].append(conv_p[:, CONV_CARRY - (CONV_W - 1):])
        outs[3].append(_token_major(mkt))
        outs[4].append(_token_major(mvt))
        hs_mid, pool_s, hg_s = _mixer_sample(
            hs, _feature_major(cache_mem_k[l]), _feature_major(cache_mem_v[l]),
            _front_pad(state_pool[l], CARRY_ROWS), state_hgrn[l], lb_logits, w, l, past_len, alpha, seq_s)
        hs, conv_s = _ffn_sample(hs_mid, _front_pad(state_ffn_conv[l], CONV_CARRY), w, alpha, seq_s)
        outs[5].append(pool_s[:, CARRY_ROWS - POOL_BUF:])
        outs[6].append(hg_s)
        outs[7].append(conv_s[:, CONV_CARRY - (CONV_W - 1):])
    y_sample = hs.reshape(n_sample, SUBLANES, D_MODEL)[:, :seq_s]
    return (hp, y_sample) + tuple(jnp.stack(o) for o in outs)
```

```python
import functools

import jax
import jax.numpy as jnp
from jax import lax
from jax.experimental import pallas as pl
from jax.experimental.pallas import tpu as pltpu

F32 = jnp.float32
BF16 = jnp.bfloat16

D_MODEL = 1024
POOL_WIDTH = 256
POOL_WINDOWS = (2, 4, 8, 16)
POOL_GDIM = 64
POOL_BUF = 15
HG_HEADS = 4
HG_KDIM = 128
HG_WIDTH = 512
HG_VDIM = 128
MEM_HEADS = 4
MEM_HDIM = 64
MEM_WIDTH = 256
N_BRANCH = 3
D_FF = 2816
CONV_W = 3
LN_EPS = 1e-5
RMS_EPS = 1e-6

C_UA = (0, 256)
C_QB = (256, 768)
C_FB = (768, 1280)
C_IB = (1280, 1792)
C_GB = (1792, 2304)
C_QC = (2304, 2560)
C_GATE = (2560, 5632)

SUBLANES = 8
BF16_TILE_ROWS = 16
CARRY_ROWS = 16
CONV_CARRY = 8
PROMPT_T = 512
FFN_T = 1024
NORM_ROWS = 256
HG_BLOCK = 256
HG_CHUNK = 64
SAMPLE_BB = 16
SAMPLE_FFN_BB = 32
VMEM_LIMIT = 56 * 1024 * 1024


def _mm(a, b):
    return jnp.dot(a.astype(BF16), b.astype(BF16), preferred_element_type=F32)


def _mm_nt(a, b):
    return lax.dot_general(a.astype(BF16), b.astype(BF16), (((1,), (1,)), ((), ())), preferred_element_type=F32)


def _mm_tn(a, b):
    return lax.dot_general(a.astype(BF16), b.astype(BF16), (((0,), (0,)), ((), ())), preferred_element_type=F32)


def _split3(x):
    h1 = x.astype(BF16)
    r1 = x - h1.astype(F32)
    h2 = r1.astype(BF16)
    h3 = (r1 - h2.astype(F32)).astype(BF16)
    return h1, h2, h3


def _select_mm(sel, x):
    h1, h2, h3 = _split3(x)
    d = lambda p: jnp.dot(sel, p, preferred_element_type=F32)
    return d(h1) + d(h2) + d(h3)


def _silu_from_half(p):
    return p + p * jnp.tanh(p)


def _gated(p, m):
    return m + jnp.tanh(p) * m


def _layer_norm(x, g, b):
    mu = jnp.mean(x, axis=-1, keepdims=True)
    xc = x - mu
    var = jnp.mean(xc * xc, axis=-1, keepdims=True)
    return xc * lax.rsqrt(var + LN_EPS) * g + b


def _gelu(x):
    return 0.5 * x * (1.0 + lax.erf(x * (2.0 ** -0.5)))


def _lower_bound(lb_logits, layer):
    m = jnp.max(lb_logits, axis=0, keepdims=True)
    e = jnp.exp(lb_logits - m)
    sm = e / jnp.sum(e, axis=0, keepdims=True)
    return jnp.sum(sm[:layer + 1], axis=0, keepdims=True)


def _pool_diff(full, u_a, pos, tail):
    s2 = full + pltpu.roll(full, 1, 0)
    s4 = s2 + pltpu.roll(s2, 2, 0)
    s8 = s4 + pltpu.roll(s4, 4, 0)
    s16 = s8 + pltpu.roll(s8, 8, 0)
    grp = lax.broadcasted_iota(jnp.int32, u_a.shape, 1) // POOL_GDIM
    wsum = jnp.where(grp == 0, tail(s2), jnp.where(grp == 1, tail(s4), jnp.where(grp == 2, tail(s8), tail(s16))))
    wlen = jnp.where(grp == 0, 2, jnp.where(grp == 1, 4, jnp.where(grp == 2, 8, 16)))
    count = jnp.minimum(pos + 1, wlen).astype(F32)
    return wsum / count - u_a


def _forget_gates(fb_half, lb):
    th = jnp.tanh(fb_half)
    f = lb + (1.0 - lb) * (0.5 + 0.5 * th)
    k = (1.0 - lb) * (0.5 - 0.5 * th)
    return jnp.log(f), k


def _chunk_tri(n, chunk):
    r = lax.broadcasted_iota(jnp.int32, (n, n), 0)
    c = lax.broadcasted_iota(jnp.int32, (n, n), 1)
    return (r // chunk == c // chunk) & (c <= r)


def _as_bf16(mask):
    return jnp.where(mask, 1.0, 0.0).astype(BF16)


def _rms_head(o, g):
    return o * lax.rsqrt(jnp.mean(o * o, axis=-1, keepdims=True) + RMS_EPS) * g


def _project_norm(out_ref, resid, lhs, w, g, b, alpha):
    n = resid.shape[0]
    step = min(n, NORM_ROWS)
    for rs in (slice(i, i + step) for i in range(0, n, step)):
        out_ref[rs, :] = _layer_norm(alpha * resid[rs] + _mm(lhs[rs], w), g, b)


def _merge_out(x, y_a, y_b, y_c, gate_pre, wbp, wbh, wbm, wout, g, b, alpha):
    merged2 = (_gated(gate_pre[:, 0:D_MODEL], _mm(y_a, wbp))
               + _gated(gate_pre[:, D_MODEL:2 * D_MODEL], _mm(y_b, wbh))
               + _gated(gate_pre[:, 2 * D_MODEL:3 * D_MODEL], _mm(y_c, wbm)))
    return _layer_norm(alpha * x + _mm(merged2, wout), g, b)


def _memkv_kernel(mem_ref, wkt_ref, wvt_ref, kt_ref, vt_ref):
    m = mem_ref[...].astype(BF16)
    kt_ref[0] = _mm_nt(wkt_ref[...], m)
    vt_ref[0] = _mm_nt(wvt_ref[...], m)


def _memkv(mem2d, wkt, wvt, n_mem):
    n_seq = mem2d.shape[0] // n_mem
    const = lambda i: (0, 0)
    per_seq = lambda i: (i, 0, 0)
    return pl.pallas_call(
        _memkv_kernel,
        grid=(n_seq,),
        in_specs=[pl.BlockSpec((n_mem, D_MODEL), lambda i: (i, 0)),
                  pl.BlockSpec((MEM_WIDTH, D_MODEL), const),
                  pl.BlockSpec((MEM_WIDTH, D_MODEL), const)],
        out_specs=[pl.BlockSpec((1, MEM_WIDTH, n_mem), per_seq),
                   pl.BlockSpec((1, MEM_WIDTH, n_mem), per_seq)],
        out_shape=[jax.ShapeDtypeStruct((n_seq, MEM_WIDTH, n_mem), F32)] * 2,
        name="memkv",
    )(mem2d, wkt, wvt)


def _feature_major(mem):
    return jnp.transpose(mem, (0, 2, 3, 1)).reshape(mem.shape[0], MEM_WIDTH, mem.shape[1])


def _token_major(mem_t):
    bsz, _, n_mem = mem_t.shape
    return jnp.transpose(mem_t.reshape(bsz, MEM_HEADS, MEM_HDIM, n_mem), (0, 3, 1, 2))


def _mixer_prompt_kernel(x_ref, mk_ref, mv_ref, pool0_ref, hg0_ref, lb_ref, win_ref, bd_ref, pscale_ref, hgg_ref,
                         wbp_ref, wbh_ref, wbm_ref, wout_ref, g_ref, b_ref, *rest, layer, pos0, alpha, cast_blocks):
    n_cast = len(cast_blocks)
    cast_in, (h_ref, newpool_ref, newhg_ref) = rest[:n_cast], rest[n_cast:n_cast + 3]
    cast_out, (st_scr, pool_scr) = rest[n_cast + 3:2 * n_cast + 3], rest[2 * n_cast + 3:]
    T, C = PROMPT_T, HG_CHUNK
    j = pl.program_id(1)

    step = pl.program_id(0) * pl.num_programs(1) + j
    for src, dst, n_blocks in zip(cast_in, cast_out, cast_blocks):
        @pl.when(step < n_blocks)
        def _(src=src, dst=dst):
            dst[...] = src[...].astype(BF16)

    @pl.when(j == 0)
    def _():
        pool_scr[...] = pool0_ref[0]
        for h in range(HG_HEADS):
            st_scr[h] = hg0_ref[0, h].T

    x = x_ref[0]
    xb = x.astype(BF16)
    proj = lambda c: jnp.dot(xb, win_ref[:, c[0]:c[1]], preferred_element_type=F32)

    lb = _lower_bound(lb_ref[...], layer)
    logf, k = _forget_gates(proj(C_FB), lb)
    qb = proj(C_QB)
    v = proj(C_IB)
    same_chunk_causal = _chunk_tri(HG_BLOCK, C)
    tri = _as_bf16(same_chunk_causal)
    blocks = [slice(i * HG_BLOCK, (i + 1) * HG_BLOCK) for i in range(T // HG_BLOCK)]
    bcum = jnp.concatenate([_select_mm(tri, logf[bs]) for bs in blocks], axis=0)
    u_a = proj(C_UA)
    qc = proj(C_QC)
    gb = proj(C_GB)
    gate_a = proj((C_GATE[0], C_GATE[0] + D_MODEL))

    full = jnp.concatenate([pool_scr[...], u_a], axis=0)
    pos = pos0 + j * T + lax.broadcasted_iota(jnp.int32, (T, POOL_WIDTH), 0)
    diff_a = _pool_diff(full, u_a, pos, lambda z: z[CARRY_ROWS:])
    pool_scr[...] = full[T:]

    mk = mk_ref[0].astype(BF16)
    mv = mv_ref[0].astype(BF16)
    head_of_lane = lax.broadcasted_iota(jnp.int32, (T, MEM_WIDTH), 1) // MEM_HDIM
    att = [_mm(jnp.where(head_of_lane == h, qc, 0.0), mk) * (MEM_HDIM ** -0.5) for h in range(MEM_HEADS)]
    gate_b = proj((C_GATE[0] + D_MODEL, C_GATE[0] + 2 * D_MODEL))

    q = _silu_from_half(qb)
    n_chunks = T // C
    chunk_rows = [slice(c * C, (c + 1) * C) for c in range(n_chunks)]
    per_chunk = lambda row: jnp.concatenate(
        [jnp.broadcast_to(bcum[c * C + row:c * C + row + 1], (C, HG_WIDTH)) for c in range(n_chunks)], axis=0)
    bm = per_chunk(C // 2)
    bl = per_chunk(C - 1)
    qin = q * jnp.exp(bcum)
    qd = q * jnp.exp(bcum - bm)
    kd = k * jnp.exp(bm - bcum)
    kl = k * jnp.exp(bl - bcum)
    dec = [jnp.exp(bcum[c * C + C - 1:(c + 1) * C]) for c in range(n_chunks)]
    hgg = hgg_ref[...]
    head_lanes = [slice(h * HG_KDIM, (h + 1) * HG_KDIM) for h in range(HG_HEADS)]
    sc = [[jnp.where(same_chunk_causal, _mm_nt(qd[bs, sl], kd[bs, sl]), 0.0) for bs in blocks] for sl in head_lanes]
    grow = [[_mm_tn(v[rs, sl], kl[rs, sl]) for rs in chunk_rows] for sl in head_lanes]
    y_a = _mm(diff_a, bd_ref[...]) * pscale_ref[...]
    gate_c = proj((C_GATE[0] + 2 * D_MODEL, C_GATE[1]))

    y_c = jnp.zeros((T, MEM_WIDTH), F32)
    for h in range(MEM_HEADS):
        e = jnp.exp(att[h] - jnp.max(att[h], axis=-1, keepdims=True))
        p = e / jnp.sum(e, axis=-1, keepdims=True)
        y_c = y_c + jnp.where(head_of_lane == h, _mm_nt(p, mv), 0.0)
    m_a = _mm(y_a, wbp_ref[...])

    heads = []
    for h, sl in enumerate(head_lanes):
        o_intra = jnp.concatenate([_mm(sc[h][i], v[bs, sl]) for i, bs in enumerate(blocks)], axis=0)
        st = st_scr[h]
        o_inter = []
        for c, rs in enumerate(chunk_rows):
            o_inter.append(_mm_nt(qin[rs, sl], st))
            st = st * dec[c][:, sl] + grow[h][c]
        st_scr[h] = st
        heads.append(_rms_head(o_intra + jnp.concatenate(o_inter, axis=0), hgg[:, sl]))
    m_c = _mm(y_c, wbm_ref[...])
    y_b = jnp.concatenate(heads, axis=1) * _silu_from_half(gb)
    merged2 = _gated(gate_a, m_a) + _gated(gate_c, m_c) + _gated(gate_b, _mm(y_b, wbh_ref[...]))
    _project_norm(h_ref.at[0], x, merged2, wout_ref[...], g_ref[...], b_ref[...], alpha)

    @pl.when(j == pl.num_programs(1) - 1)
    def _():
        newpool_ref[0] = pool_scr[...]
        for h in range(HG_HEADS):
            newhg_ref[0, h] = st_scr[h].T


def _const_spec(shape):
    n = len(shape)
    return pl.BlockSpec(shape, lambda *_: (0,) * n, pipeline_mode=pl.Buffered(1))


def _cast_block_count(rows, n_steps):
    return max(n for n in range(1, n_steps + 1) if rows % n == 0 and (rows // n) % BF16_TILE_ROWS == 0)


def _mixer_prompt(x, mk, mv, pool0, hg0, lb_logits, w, layer, pos0, alpha, to_cast):
    bsz, seq, _ = x.shape
    T = PROMPT_T
    n_j = seq // T
    per_b3 = lambda b, j: (b, 0, 0)
    cast_blocks = tuple(_cast_block_count(a.shape[0], bsz * n_j) for a in to_cast)
    cast_specs = [pl.BlockSpec((a.shape[0] // n, a.shape[1]), lambda b, j, n=n: (jnp.minimum(b * n_j + j, n - 1), 0))
                  for a, n in zip(to_cast, cast_blocks)]
    kern = functools.partial(_mixer_prompt_kernel, layer=layer, pos0=pos0, alpha=alpha, cast_blocks=cast_blocks)
    weights = [lb_logits, w['w_in'], w['bd_pool'], w['pool_scale'], w['hg_norm_g'], w['w_br_pool'], w['w_br_hg'],
               w['w_br_mem'], w['w_out'], w['ln1_g'], w['ln1_b']]
    outs = pl.pallas_call(
        kern,
        grid=(bsz, n_j),
        in_specs=[pl.BlockSpec((1, T, D_MODEL), lambda b, j: (b, j, 0)),
                  pl.BlockSpec((1,) + mk.shape[1:], per_b3),
                  pl.BlockSpec((1,) + mv.shape[1:], per_b3),
                  pl.BlockSpec((1, CARRY_ROWS, POOL_WIDTH), per_b3),
                  pl.BlockSpec((1, HG_HEADS, HG_KDIM, HG_VDIM), lambda b, j: (b, 0, 0, 0))]
                 + [_const_spec(a.shape) for a in weights] + cast_specs,
        out_specs=[pl.BlockSpec((1, T, D_MODEL), lambda b, j: (b, j, 0)),
                   pl.BlockSpec((1, CARRY_ROWS, POOL_WIDTH), per_b3),
                   pl.BlockSpec((1, HG_HEADS, HG_KDIM, HG_VDIM), lambda b, j: (b, 0, 0, 0))] + cast_specs,
        out_shape=[jax.ShapeDtypeStruct((bsz, seq, D_MODEL), F32),
                   jax.ShapeDtypeStruct((bsz, CARRY_ROWS, POOL_WIDTH), F32),
                   jax.ShapeDtypeStruct((bsz, HG_HEADS, HG_KDIM, HG_VDIM), F32)]
                  + [jax.ShapeDtypeStruct(a.shape, BF16) for a in to_cast],
        scratch_shapes=[pltpu.VMEM((HG_HEADS, HG_VDIM, HG_KDIM), F32),
                        pltpu.VMEM((CARRY_ROWS, POOL_WIDTH), F32)],
        compiler_params=pltpu.CompilerParams(dimension_semantics=("arbitrary", "arbitrary"),
                                             vmem_limit_bytes=VMEM_LIMIT),
        name="mixer_prompt",
    )(x, mk, mv, pool0, hg0, *weights, *to_cast)
    return outs[:3], outs[3:]


def _conv_gate(full, tail, u, cw, cb):
    c = cb + pltpu.roll(full, 2, 0) * cw[0:1] + pltpu.roll(full, 1, 0) * cw[1:2] + full * cw[2:3]
    return _gelu(tail(c)) * u


def _ffn_prompt_kernel(h_ref, conv0_ref, wg_ref, wu_ref, cw_ref, cb_ref, wd_ref, g_ref, b_ref,
                       y_ref, newconv_ref, carry_scr, *, alpha):
    T = FFN_T
    j = pl.program_id(1)

    @pl.when(j == 0)
    def _():
        carry_scr[...] = conv0_ref[0]

    h = h_ref[0]
    hb = h.astype(BF16)
    a = jnp.dot(hb, wg_ref[...], preferred_element_type=F32)
    u = jnp.dot(hb, wu_ref[...], preferred_element_type=F32)
    full = jnp.concatenate([carry_scr[...], a], axis=0)
    gated = _conv_gate(full, lambda z: z[CONV_CARRY:], u, cw_ref[...], cb_ref[...])
    _project_norm(y_ref.at[0], h, gated, wd_ref[...], g_ref[...], b_ref[...], alpha)
    carry_scr[...] = a[T - CONV_CARRY:]

    @pl.when(j == pl.num_programs(1) - 1)
    def _():
        newconv_ref[0] = carry_scr[...]


def _ffn_prompt(h, conv0, w, alpha):
    bsz, seq, _ = h.shape
    T = FFN_T
    weights = [w['w_gate'], w['w_up'], w['conv_w'], w['conv_b'], w['w_down'], w['ln2_g'], w['ln2_b']]
    return pl.pallas_call(
        functools.partial(_ffn_prompt_kernel, alpha=alpha),
        grid=(bsz, seq // T),
        in_specs=[pl.BlockSpec((1, T, D_MODEL), lambda b, j: (b, j, 0)),
                  pl.BlockSpec((1, CONV_CARRY, D_FF), lambda b, j: (b, 0, 0))]
                 + [_const_spec(a.shape) for a in weights],
        out_specs=[pl.BlockSpec((1, T, D_MODEL), lambda b, j: (b, j, 0)),
                   pl.BlockSpec((1, CONV_CARRY, D_FF), lambda b, j: (b, 0, 0))],
        out_shape=[jax.ShapeDtypeStruct((bsz, seq, D_MODEL), F32),
                   jax.ShapeDtypeStruct((bsz, CONV_CARRY, D_FF), F32)],
        scratch_shapes=[pltpu.VMEM((CONV_CARRY, D_FF), F32)],
        compiler_params=pltpu.CompilerParams(dimension_semantics=("arbitrary", "arbitrary"),
                                             vmem_limit_bytes=VMEM_LIMIT),
        name="ffn_prompt",
    )(h, conv0, *weights)


def _bmm(spec, a, b):
    return jnp.einsum(spec, a.astype(BF16), b.astype(BF16), preferred_element_type=F32)


def _mixer_sample_kernel(x_ref, mk_ref, mv_ref, pool_ref, hg_ref, lb_ref, win_ref, bd_ref, pscale_ref, hgg_ref,
                         wbp_ref, wbh_ref, wbm_ref, wout_ref, g_ref, b_ref,
                         h_ref, newpool_ref, newhg_ref, *, layer, pos0, alpha, seq):
    Bb, S = SAMPLE_BB, SUBLANES
    R = Bb * S
    x = x_ref[...]
    xb = x.astype(BF16)
    proj = lambda c: jnp.dot(xb, win_ref[:, c[0]:c[1]], preferred_element_type=F32)

    u_a = proj(C_UA)
    seg = CARRY_ROWS + S
    full = jnp.concatenate([pool_ref[...], u_a.reshape(Bb, S, POOL_WIDTH)], axis=1).reshape(Bb * seg, POOL_WIDTH)
    tail = lambda z: z.reshape(Bb, seg, POOL_WIDTH)[:, CARRY_ROWS:, :].reshape(R, POOL_WIDTH)
    pos = pos0 + lax.broadcasted_iota(jnp.int32, (R, POOL_WIDTH), 0) % S
    y_a = _mm(_pool_diff(full, u_a, pos, tail), bd_ref[...]) * pscale_ref[...]
    newpool_ref[...] = pltpu.roll(full, Bb * seg - seq, 0).reshape(Bb, seg, POOL_WIDTH)[:, :CARRY_ROWS, :]

    live = lax.broadcasted_iota(jnp.int32, (R, HG_WIDTH), 0) % S < seq
    lb = _lower_bound(lb_ref[...], layer)
    logf, k = _forget_gates(proj(C_FB), lb)
    logf = jnp.where(live, logf, 0.0)
    k = jnp.where(live, k, 0.0)
    qb = proj(C_QB)
    q = _silu_from_half(qb)
    v = proj(C_IB)
    to3 = lambda z: z.reshape(Bb, S, z.shape[-1])
    b3 = to3(_select_mm(_as_bf16(_chunk_tri(R, S)), logf))
    bm = b3[:, S // 2:S // 2 + 1, :]
    bl = b3[:, S - 1:S, :]
    q3, k3, v3 = to3(q), to3(k), to3(v)
    qin = q3 * jnp.exp(b3)
    qd = q3 * jnp.exp(b3 - bm)
    kd = k3 * jnp.exp(bm - b3)
    kl = k3 * jnp.exp(bl - b3)
    rr = lax.broadcasted_iota(jnp.int32, (Bb, S, S), 1)
    cc = lax.broadcasted_iota(jnp.int32, (Bb, S, S), 2)
    causal = cc <= rr
    p1 = bl.astype(BF16).astype(F32)
    p2 = (bl - p1).astype(BF16).astype(F32)
    p3 = (bl - p1) - p2
    r3 = lax.broadcasted_iota(jnp.int32, (Bb, S, HG_WIDTH), 1)
    pieces = jnp.where(r3 == 0, p1, jnp.where(r3 == 1, p2, jnp.where(r3 == 2, p3, 0.0)))
    ones = jnp.ones((Bb, S, HG_VDIM), BF16)
    hgg = hgg_ref[...]
    heads = []
    for h in range(HG_HEADS):
        sl = slice(h * HG_KDIM, (h + 1) * HG_KDIM)
        sc = jnp.where(causal, _bmm('bqd,bkd->bqk', qd[:, :, sl], kd[:, :, sl]), 0.0)
        s0 = hg_ref[:, h]
        o = _bmm('bqk,bke->bqe', sc, v3[:, :, sl]) + _bmm('bqd,bde->bqe', qin[:, :, sl], s0)
        logdec = _bmm('bkd,bke->bde', pieces[:, :, sl], ones)
        newhg_ref[:, h] = jnp.exp(logdec) * s0 + _bmm('bkd,bke->bde', kl[:, :, sl], v3[:, :, sl])
        heads.append(_rms_head(o, hgg[:, sl]).reshape(R, HG_VDIM))
    gb = proj(C_GB)
    y_b = jnp.concatenate(heads, axis=1) * _silu_from_half(gb)

    qc3 = to3(proj(C_QC))
    head_of_lane = lax.broadcasted_iota(jnp.int32, (Bb, S, MEM_WIDTH), 2) // MEM_HDIM
    q4 = jnp.concatenate([jnp.where(head_of_lane == h, qc3, 0.0) for h in range(MEM_HEADS)], axis=1)
    s = _bmm('bqd,bdm->bqm', q4, mk_ref[...]) * (MEM_HDIM ** -0.5)
    e = jnp.exp(s - jnp.max(s, axis=-1, keepdims=True))
    p = e / jnp.sum(e, axis=-1, keepdims=True)
    o4 = _bmm('bqm,bdm->bqd', p, mv_ref[...])
    y_c = jnp.zeros((Bb, S, MEM_WIDTH), F32)
    for h in range(MEM_HEADS):
        y_c = y_c + jnp.where(head_of_lane == h, o4[:, h * S:(h + 1) * S, :], 0.0)
    y_c = y_c.reshape(R, MEM_WIDTH)

    h_ref[...] = _merge_out(x, y_a, y_b, y_c, proj(C_GATE), wbp_ref[...], wbh_ref[...], wbm_ref[...], wout_ref[...],
                            g_ref[...], b_ref[...], alpha)


def _mixer_sample(x2d, mk, mv, pool16, hg, lb_logits, w, layer, pos0, alpha, seq):
    Bb, S = SAMPLE_BB, SUBLANES
    bsz = hg.shape[0]
    n_mem = mk.shape[2]
    b3 = lambda i: (i, 0, 0)
    weights = [lb_logits, w['w_in'], w['bd_pool'], w['pool_scale'], w['hg_norm_g'], w['w_br_pool'], w['w_br_hg'],
               w['w_br_mem'], w['w_out'], w['ln1_g'], w['ln1_b']]
    return pl.pallas_call(
        functools.partial(_mixer_sample_kernel, layer=layer, pos0=pos0, alpha=alpha, seq=seq),
        grid=(bsz // Bb,),
        in_specs=[pl.BlockSpec((Bb * S, D_MODEL), lambda i: (i, 0)),
                  pl.BlockSpec((Bb, MEM_WIDTH, n_mem), b3),
                  pl.BlockSpec((Bb, MEM_WIDTH, n_mem), b3),
                  pl.BlockSpec((Bb, CARRY_ROWS, POOL_WIDTH), b3),
                  pl.BlockSpec((Bb, HG_HEADS, HG_KDIM, HG_VDIM), lambda i: (i, 0, 0, 0))]
                 + [_const_spec(a.shape) for a in weights],
        out_specs=[pl.BlockSpec((Bb * S, D_MODEL), lambda i: (i, 0)),
                   pl.BlockSpec((Bb, CARRY_ROWS, POOL_WIDTH), b3),
                   pl.BlockSpec((Bb, HG_HEADS, HG_KDIM, HG_VDIM), lambda i: (i, 0, 0, 0))],
        out_shape=[jax.ShapeDtypeStruct((bsz * S, D_MODEL), F32),
                   jax.ShapeDtypeStruct((bsz, CARRY_ROWS, POOL_WIDTH), F32),
                   jax.ShapeDtypeStruct((bsz, HG_HEADS, HG_KDIM, HG_VDIM), F32)],
        compiler_params=pltpu.CompilerParams(dimension_semantics=("arbitrary",), vmem_limit_bytes=VMEM_LIMIT),
        name="mixer_sample",
    )(x2d, mk, mv, pool16, hg, *weights)


def _ffn_sample_kernel(h_ref, conv_ref, wg_ref, wu_ref, cw_ref, cb_ref, wd_ref, g_ref, b_ref,
                       y_ref, newconv_ref, *, alpha, seq):
    Bb, S = SAMPLE_FFN_BB, SUBLANES
    R = Bb * S
    seg = CONV_CARRY + S
    h = h_ref[...]
    hb = h.astype(BF16)
    a = jnp.dot(hb, wg_ref[...], preferred_element_type=F32)
    u = jnp.dot(hb, wu_ref[...], preferred_element_type=F32)
    full = jnp.concatenate([conv_ref[...], a.reshape(Bb, S, D_FF)], axis=1).reshape(Bb * seg, D_FF)
    tail = lambda z: z.reshape(Bb, seg, D_FF)[:, CONV_CARRY:, :].reshape(R, D_FF)
    gated = _conv_gate(full, tail, u, cw_ref[...], cb_ref[...])
    _project_norm(y_ref, h, gated, wd_ref[...], g_ref[...], b_ref[...], alpha)
    newconv_ref[...] = pltpu.roll(full, Bb * seg - seq, 0).reshape(Bb, seg, D_FF)[:, :CONV_CARRY, :]


def _ffn_sample(h2d, conv8, w, alpha, seq):
    Bb, S = SAMPLE_FFN_BB, SUBLANES
    bsz = conv8.shape[0]
    weights = [w['w_gate'], w['w_up'], w['conv_w'], w['conv_b'], w['w_down'], w['ln2_g'], w['ln2_b']]
    return pl.pallas_call(
        functools.partial(_ffn_sample_kernel, alpha=alpha, seq=seq),
        grid=(bsz // Bb,),
        in_specs=[pl.BlockSpec((Bb * S, D_MODEL), lambda i: (i, 0)),
                  pl.BlockSpec((Bb, CONV_CARRY, D_FF), lambda i: (i, 0, 0))]
                 + [_const_spec(a.shape) for a in weights],
        out_specs=[pl.BlockSpec((Bb * S, D_MODEL), lambda i: (i, 0)),
                   pl.BlockSpec((Bb, CONV_CARRY, D_FF), lambda i: (i, 0, 0))],
        out_shape=[jax.ShapeDtypeStruct((bsz * S, D_MODEL), F32),
                   jax.ShapeDtypeStruct((bsz, CONV_CARRY, D_FF), F32)],
        compiler_params=pltpu.CompilerParams(dimension_semantics=("arbitrary",), vmem_limit_bytes=VMEM_LIMIT),
        name="ffn_sample",
    )(h2d, conv8, *weights)


def _block_diag(w_grp):
    groups, gdim, _ = w_grp.shape
    out = jnp.zeros((groups * gdim, groups * gdim), w_grp.dtype)
    for g in range(groups):
        out = lax.dynamic_update_slice(out, w_grp[g], (g * gdim, g * gdim))
    return out


def _front_pad(a, rows):
    return jnp.pad(a, ((0, 0), (rows - a.shape[1], 0), (0, 0)))


def kernel(x_prompt, x_sample, state_pool, state_hgrn, state_ffn_conv, cache_mem_k, cache_mem_v, mem_prompt, lb_logits, w_in, w_pool_grp, pool_scale, hg_norm_g, w_mem_k, w_mem_v, w_br_pool, w_br_hg, w_br_mem, w_out, ln1_g, ln1_b, w_gate, w_up, conv_w, conv_b, w_down, ln2_g, ln2_b):
    depth = w_in.shape[0]
    alpha = (2 * depth) ** 0.25
    n_prompt, seq_p, _ = x_prompt.shape
    n_sample, seq_s, _ = x_sample.shape
    n_mem = mem_prompt.shape[1]
    past_len = 16384
    assert seq_p % PROMPT_T == 0 and seq_p % FFN_T == 0 and seq_s <= SUBLANES and n_sample % SAMPLE_BB == 0 and n_sample % SAMPLE_FFN_BB == 0

    hp = x_prompt
    hs = jnp.pad(x_sample, ((0, 0), (0, SUBLANES - seq_s), (0, 0))).reshape(n_sample * SUBLANES, D_MODEL)
    mem2d = mem_prompt.reshape(n_prompt * n_mem, D_MODEL)
    row = lambda a: a.reshape(1, -1)
    col = jnp.arange(w_in.shape[-1])
    in_range = lambda c: (col >= c[0]) & (col < c[1])
    half_cols = jnp.where(in_range(C_QB) | in_range(C_FB) | in_range(C_GB) | in_range(C_GATE), 0.5, 1.0).astype(F32)
    outs = [[] for _ in range(8)]
    for l in range(depth):
        w = {'w_in': (w_in[l] * half_cols).astype(BF16), 'bd_pool': _block_diag(w_pool_grp[l]).astype(BF16),
             'pool_scale': row(pool_scale[l]), 'hg_norm_g': row(hg_norm_g[l]),
             'w_br_pool': w_br_pool[l].astype(BF16), 'w_br_hg': w_br_hg[l].astype(BF16),
             'w_br_mem': w_br_mem[l].astype(BF16), 'w_out': (0.5 * w_out[l]).astype(BF16),
             'ln1_g': row(ln1_g[l]), 'ln1_b': row(ln1_b[l]),
             'conv_w': conv_w[l], 'conv_b': row(conv_b[l]), 'ln2_g': row(ln2_g[l]), 'ln2_b': row(ln2_b[l])}
        mkt, mvt = _memkv(mem2d, w_mem_k[l].T.astype(BF16), w_mem_v[l].T.astype(BF16), n_mem)
        (h_mid, pool_p, hg_p), (w['w_gate'], w['w_up'], w['w_down']) = _mixer_prompt(
            hp, mkt, mvt, jnp.zeros((n_prompt, CARRY_ROWS, POOL_WIDTH), F32),
            jnp.zeros((n_prompt, HG_HEADS, HG_KDIM, HG_VDIM), F32), lb_logits, w, l, 0, alpha,
            (w_gate[l], w_up[l], w_down[l]))
        hp, conv_p = _ffn_prompt(h_mid, jnp.zeros((n_prompt, CONV_CARRY, D_FF), F32), w, alpha)
        outs[0].append(pool_p[:, CARRY_ROWS - POOL_BUF:])
        outs[1].append(hg_p)
        outs[2].append(conv_p[:, CONV_CARRY - (CONV_W - 1):])
        outs[3].append(_token_major(mkt))
        outs[4].append(_token_major(mvt))
        hs_mid, pool_s, hg_s = _mixer_sample(
            hs, _feature_major(cache_mem_k[l]), _feature_major(cache_mem_v[l]),
            _front_pad(state_pool[l], CARRY_ROWS), state_hgrn[l], lb_logits, w, l, past_len, alpha, seq_s)
        hs, conv_s = _ffn_sample(hs_mid, _front_pad(state_ffn_conv[l], CONV_CARRY), w, alpha, seq_s)
        outs[5].append(pool_s[:, CARRY_ROWS - POOL_BUF:])
        outs[6].append(hg_s)
        outs[7].append(conv_s[:, CONV_CARRY - (CONV_W - 1):])
    y_sample = hs.reshape(n_sample, SUBLANES, D_MODEL)[:, :seq_s]
    return (hp, y_sample) + tuple(jnp.stack(o) for o in outs)
```

```python
import functools

import jax
import jax.numpy as jnp
from jax import lax
from jax.experimental import pallas as pl
from jax.experimental.pallas import tpu as pltpu

F32 = jnp.float32
BF16 = jnp.bfloat16

D_MODEL = 1024
POOL_WIDTH = 256
POOL_WINDOWS = (2, 4, 8, 16)
POOL_GDIM = 64
POOL_BUF = 15
HG_HEADS = 4
HG_KDIM = 128
HG_WIDTH = 512
HG_VDIM = 128
HG_VWIDTH = 512
MEM_HEADS = 4
MEM_HDIM = 64
MEM_WIDTH = 256
N_BRANCH = 3
D_FF = 2816
CONV_W = 3
LN_EPS = 1e-5
RMS_EPS = 1e-6

C_UA = (0, 256)
C_QB = (256, 768)
C_FB = (768, 1280)
C_IB = (1280, 1792)
C_GB = (1792, 2304)
C_QC = (2304, 2560)
C_GATE = (2560, 5632)

LANES = 128
SUBLANES = 8
BF16_TILE_ROWS = 16
CARRY_ROWS = 16
CONV_CARRY = 8
PROMPT_T = 512
FFN_T = 1024
NORM_ROWS = 256
HG_BLOCK = 256
HG_CHUNK = 64
SAMPLE_BB = 16
VMEM_LIMIT = 56 * 1024 * 1024


def _mm(a, b):
    return jnp.dot(a.astype(BF16), b.astype(BF16), preferred_element_type=F32)


def _mm_nt(a, b):
    return lax.dot_general(a.astype(BF16), b.astype(BF16), (((1,), (1,)), ((), ())), preferred_element_type=F32)


def _mm_tn(a, b):
    return lax.dot_general(a.astype(BF16), b.astype(BF16), (((0,), (0,)), ((), ())), preferred_element_type=F32)


def _split3(x):
    h1 = x.astype(BF16)
    r1 = x - h1.astype(F32)
    h2 = r1.astype(BF16)
    h3 = (r1 - h2.astype(F32)).astype(BF16)
    return h1, h2, h3


def _select_mm(sel, x):
    h1, h2, h3 = _split3(x)
    d = lambda p: jnp.dot(sel, p, preferred_element_type=F32)
    return d(h1) + d(h2) + d(h3)


def _silu_from_half(p):
    return p + p * jnp.tanh(p)


def _gated(p, m):
    return m + jnp.tanh(p) * m


def _layer_norm(x, g, b):
    mu = jnp.mean(x, axis=-1, keepdims=True)
    xc = x - mu
    var = jnp.mean(xc * xc, axis=-1, keepdims=True)
    return xc * lax.rsqrt(var + LN_EPS) * g + b


def _gelu(x):
    return 0.5 * x * (1.0 + lax.erf(x * (2.0 ** -0.5)))


def _lower_bound(lb_logits, layer):
    m = jnp.max(lb_logits, axis=0, keepdims=True)
    e = jnp.exp(lb_logits - m)
    sm = e / jnp.sum(e, axis=0, keepdims=True)
    return jnp.sum(sm[:layer + 1], axis=0, keepdims=True)


def _pool_diff(full, u_a, pos, tail):
    s2 = full + pltpu.roll(full, 1, 0)
    s4 = s2 + pltpu.roll(s2, 2, 0)
    s8 = s4 + pltpu.roll(s4, 4, 0)
    s16 = s8 + pltpu.roll(s8, 8, 0)
    grp = lax.broadcasted_iota(jnp.int32, u_a.shape, 1) // POOL_GDIM
    wsum = jnp.where(grp == 0, tail(s2), jnp.where(grp == 1, tail(s4), jnp.where(grp == 2, tail(s8), tail(s16))))
    wlen = jnp.where(grp == 0, 2, jnp.where(grp == 1, 4, jnp.where(grp == 2, 8, 16)))
    count = jnp.minimum(pos + 1, wlen).astype(F32)
    return wsum / count - u_a


def _forget_gates(fb_half, lb):
    th = jnp.tanh(fb_half)
    f = lb + (1.0 - lb) * (0.5 + 0.5 * th)
    k = (1.0 - lb) * (0.5 - 0.5 * th)
    return jnp.log(f), k


def _chunk_tri(n, chunk):
    r = lax.broadcasted_iota(jnp.int32, (n, n), 0)
    c = lax.broadcasted_iota(jnp.int32, (n, n), 1)
    return (r // chunk == c // chunk) & (c <= r)


def _as_bf16(mask):
    return jnp.where(mask, 1.0, 0.0).astype(BF16)


def _rms_head(o, g):
    return o * lax.rsqrt(jnp.mean(o * o, axis=-1, keepdims=True) + RMS_EPS) * g


def _project_norm(store, resid, lhs, w, g, b, alpha, group_rows=NORM_ROWS):
    n = resid.shape[0]
    step = min(n, group_rows)
    for i in range(0, n, step):
        store(i, _layer_norm(alpha * resid[i:i + step] + _mm(lhs[i:i + step], w), g, b))


def _rows_of(ref2d):
    def store(first_row, val):
        ref2d[first_row:first_row + val.shape[0], :] = val
    return store


def _merge_out(x, y_a, y_b, y_c, gate_pre, wbp, wbh, wbm, wout, g, b, alpha):
    merged2 = (_gated(gate_pre[:, 0:D_MODEL], _mm(y_a, wbp))
               + _gated(gate_pre[:, D_MODEL:2 * D_MODEL], _mm(y_b, wbh))
               + _gated(gate_pre[:, 2 * D_MODEL:3 * D_MODEL], _mm(y_c, wbm)))
    return _layer_norm(alpha * x + _mm(merged2, wout), g, b)


def _cast_block_count(rows, n_steps):
    return max(n for n in range(1, n_steps + 1) if rows % n == 0 and (rows // n) % BF16_TILE_ROWS == 0)


def _ride_along_casts(step, srcs, scales, dsts, block_counts):
    for src, scale, dst, n_blocks in zip(srcs, scales, dsts, block_counts):
        @pl.when(step < n_blocks)
        def _(src=src, scale=scale, dst=dst):
            val = src[...] if scale is None else src[...] * scale[...]
            dst[...] = val.astype(BF16)


def _memkv_kernel(mem_ref, wk_ref, wv_ref, *rest, cast_blocks, scaled):
    n_cast = len(cast_blocks)
    cast_in, scale_refs = rest[:n_cast], list(rest[n_cast:n_cast + sum(scaled)])
    kt_ref, vt_ref = rest[n_cast + sum(scaled):n_cast + sum(scaled) + 2]
    cast_out = rest[n_cast + sum(scaled) + 2:]
    mt = mem_ref[...].T
    kt_ref[0] = _mm_tn(wk_ref[...], mt)
    vt_ref[0] = _mm_tn(wv_ref[...], mt)
    scales = [scale_refs.pop(0) if s else None for s in scaled]
    _ride_along_casts(pl.program_id(0), cast_in, scales, cast_out, cast_blocks)


def _memkv(mem2d, wk, wv, n_mem, to_cast):
    n_seq = mem2d.shape[0] // n_mem
    const = lambda i: (0, 0)
    per_seq = lambda i: (i, 0, 0)
    arrays = [a for a, _ in to_cast]
    scale_rows = [s.reshape(1, -1) for _, s in to_cast if s is not None]
    cast_blocks = tuple(_cast_block_count(a.shape[0], n_seq) for a in arrays)
    cast_specs = [pl.BlockSpec((a.shape[0] // n, a.shape[1]), lambda i, n=n: (jnp.minimum(i, n - 1), 0))
                  for a, n in zip(arrays, cast_blocks)]
    outs = pl.pallas_call(
        functools.partial(_memkv_kernel, cast_blocks=cast_blocks, scaled=tuple(s is not None for _, s in to_cast)),
        grid=(n_seq,),
        in_specs=[pl.BlockSpec((n_mem, D_MODEL), lambda i: (i, 0)),
                  pl.BlockSpec((D_MODEL, MEM_WIDTH), const),
                  pl.BlockSpec((D_MODEL, MEM_WIDTH), const)]
                 + cast_specs + [pl.BlockSpec(s.shape, const) for s in scale_rows],
        out_specs=[pl.BlockSpec((1, MEM_WIDTH, n_mem), per_seq),
                   pl.BlockSpec((1, MEM_WIDTH, n_mem), per_seq)] + cast_specs,
        out_shape=[jax.ShapeDtypeStruct((n_seq, MEM_WIDTH, n_mem), F32)] * 2
                  + [jax.ShapeDtypeStruct(a.shape, BF16) for a in arrays],
        name="memkv",
    )(mem2d, wk, wv, *arrays, *scale_rows)
    return outs[0], outs[1], outs[2:]


def _feature_major(mem):
    return jnp.transpose(mem, (0, 2, 3, 1)).reshape(mem.shape[0], MEM_WIDTH, mem.shape[1])


def _token_major(mem_t):
    bsz, _, n_mem = mem_t.shape
    return jnp.transpose(mem_t.reshape(bsz, MEM_HEADS, MEM_HDIM, n_mem), (0, 3, 1, 2))


def _mixer_prompt_kernel(x_ref, mk_ref, mv_ref, lb_ref, win_ref, bd_ref, pscale_ref, hgg_ref,
                         wbp_ref, wbh_ref, wbm_ref, wout_ref, g_ref, b_ref, *rest, layer, pos0, alpha, cast_blocks):
    n_cast = len(cast_blocks)
    cast_in, (h_ref, newpool_ref, newhg_ref) = rest[:n_cast], rest[n_cast:n_cast + 3]
    cast_out, (st_scr, pool_scr) = rest[n_cast + 3:2 * n_cast + 3], rest[2 * n_cast + 3:]
    T, C = PROMPT_T, HG_CHUNK
    j = pl.program_id(1)

    _ride_along_casts(pl.program_id(0) * pl.num_programs(1) + j, cast_in, [None] * n_cast, cast_out, cast_blocks)

    @pl.when(j == 0)
    def _():
        pool_scr[...] = jnp.zeros_like(pool_scr)
        st_scr[...] = jnp.zeros_like(st_scr)

    x = x_ref[0]
    xb = x.astype(BF16)
    proj = lambda c: jnp.dot(xb, win_ref[:, c[0]:c[1]], preferred_element_type=F32)

    lb = _lower_bound(lb_ref[...], layer)
    logf, k = _forget_gates(proj(C_FB), lb)
    qb = proj(C_QB)
    v = proj(C_IB)
    same_chunk_causal = _chunk_tri(HG_BLOCK, C)
    tri = _as_bf16(same_chunk_causal)
    blocks = [slice(i * HG_BLOCK, (i + 1) * HG_BLOCK) for i in range(T // HG_BLOCK)]
    bcum = jnp.concatenate([_select_mm(tri, logf[bs]) for bs in blocks], axis=0)
    u_a = proj(C_UA)
    qc = proj(C_QC)
    gb = proj(C_GB)
    gate_a = proj((C_GATE[0], C_GATE[0] + D_MODEL))

    full = jnp.concatenate([pool_scr[...], u_a], axis=0)
    pos = pos0 + j * T + lax.broadcasted_iota(jnp.int32, (T, POOL_WIDTH), 0)
    diff_a = _pool_diff(full, u_a, pos, lambda z: z[CARRY_ROWS:])
    pool_scr[...] = full[T:]

    mk = mk_ref[0].astype(BF16)
    mv = mv_ref[0].astype(BF16)
    head_of_lane = lax.broadcasted_iota(jnp.int32, (T, MEM_WIDTH), 1) // MEM_HDIM
    att = [_mm(jnp.where(head_of_lane == h, qc, 0.0), mk) * (MEM_HDIM ** -0.5) for h in range(MEM_HEADS)]
    gate_b = proj((C_GATE[0] + D_MODEL, C_GATE[0] + 2 * D_MODEL))

    q = _silu_from_half(qb)
    n_chunks = T // C
    chunk_rows = [slice(c * C, (c + 1) * C) for c in range(n_chunks)]
    per_chunk = lambda row: jnp.concatenate(
        [jnp.broadcast_to(bcum[c * C + row:c * C + row + 1], (C, HG_WIDTH)) for c in range(n_chunks)], axis=0)
    bm = per_chunk(C // 2)
    bl = per_chunk(C - 1)
    qin = q * jnp.exp(bcum)
    qd = q * jnp.exp(bcum - bm)
    kd = k * jnp.exp(bm - bcum)
    kl = k * jnp.exp(bl - bcum)
    dec = [jnp.exp(bcum[c * C + C - 1:(c + 1) * C]) for c in range(n_chunks)]
    hgg = hgg_ref[...]
    head_lanes = [slice(h * HG_KDIM, (h + 1) * HG_KDIM) for h in range(HG_HEADS)]
    sc = [[jnp.where(same_chunk_causal, _mm_nt(qd[bs, sl], kd[bs, sl]), 0.0) for bs in blocks] for sl in head_lanes]
    grow = [[_mm_tn(v[rs, sl], kl[rs, sl]) for rs in chunk_rows] for sl in head_lanes]
    y_a = _mm(diff_a, bd_ref[...]) * pscale_ref[...]
    gate_c = proj((C_GATE[0] + 2 * D_MODEL, C_GATE[1]))

    y_c = jnp.zeros((T, MEM_WIDTH), F32)
    for h in range(MEM_HEADS):
        e = jnp.exp(att[h] - jnp.max(att[h], axis=-1, keepdims=True))
        p = e / jnp.sum(e, axis=-1, keepdims=True)
        y_c = y_c + jnp.where(head_of_lane == h, _mm_nt(p, mv), 0.0)
    m_a = _mm(y_a, wbp_ref[...])

    heads = []
    for h, sl in enumerate(head_lanes):
        o_intra = jnp.concatenate([_mm(sc[h][i], v[bs, sl]) for i, bs in enumerate(blocks)], axis=0)
        st = st_scr[h]
        o_inter = []
        for c, rs in enumerate(chunk_rows):
            o_inter.append(_mm_nt(qin[rs, sl], st))
            st = st * dec[c][:, sl] + grow[h][c]
        st_scr[h] = st
        heads.append(_rms_head(o_intra + jnp.concatenate(o_inter, axis=0), hgg[:, sl]))
    m_c = _mm(y_c, wbm_ref[...])
    y_b = jnp.concatenate(heads, axis=1) * _silu_from_half(gb)
    merged2 = _gated(gate_a, m_a) + _gated(gate_c, m_c) + _gated(gate_b, _mm(y_b, wbh_ref[...]))
    _project_norm(_rows_of(h_ref.at[0]), x, merged2, wout_ref[...], g_ref[...], b_ref[...], alpha)

    @pl.when(j == pl.num_programs(1) - 1)
    def _():
        newpool_ref[0] = pool_scr[...]
        for h in range(HG_HEADS):
            newhg_ref[0, h] = st_scr[h].T


def _const_spec(shape):
    n = len(shape)
    return pl.BlockSpec(shape, lambda *_: (0,) * n, pipeline_mode=pl.Buffered(1))


def _mixer_prompt(x, mk, mv, lb_logits, w, layer, pos0, alpha, to_cast):
    bsz, seq, _ = x.shape
    T = PROMPT_T
    n_j = seq // T
    per_b3 = lambda b, j: (b, 0, 0)
    cast_blocks = tuple(_cast_block_count(a.shape[0], bsz * n_j) for a in to_cast)
    cast_specs = [pl.BlockSpec((a.shape[0] // n, a.shape[1]), lambda b, j, n=n: (jnp.minimum(b * n_j + j, n - 1), 0))
                  for a, n in zip(to_cast, cast_blocks)]
    kern = functools.partial(_mixer_prompt_kernel, layer=layer, pos0=pos0, alpha=alpha, cast_blocks=cast_blocks)
    weights = [lb_logits, w['w_in'], w['bd_pool'], w['pool_scale'], w['hg_norm_g'], w['w_br_pool'], w['w_br_hg'],
               w['w_br_mem'], w['w_out'], w['ln1_g'], w['ln1_b']]
    outs = pl.pallas_call(
        kern,
        grid=(bsz, n_j),
        in_specs=[pl.BlockSpec((1, T, D_MODEL), lambda b, j: (b, j, 0)),
                  pl.BlockSpec((1,) + mk.shape[1:], per_b3),
                  pl.BlockSpec((1,) + mv.shape[1:], per_b3)]
                 + [_const_spec(a.shape) for a in weights] + cast_specs,
        out_specs=[pl.BlockSpec((1, T, D_MODEL), lambda b, j: (b, j, 0)),
                   pl.BlockSpec((1, CARRY_ROWS, POOL_WIDTH), per_b3),
                   pl.BlockSpec((1, HG_HEADS, HG_KDIM, HG_VDIM), lambda b, j: (b, 0, 0, 0))] + cast_specs,
        out_shape=[jax.ShapeDtypeStruct((bsz, seq, D_MODEL), F32),
                   jax.ShapeDtypeStruct((bsz, CARRY_ROWS, POOL_WIDTH), F32),
                   jax.ShapeDtypeStruct((bsz, HG_HEADS, HG_KDIM, HG_VDIM), F32)]
                  + [jax.ShapeDtypeStruct(a.shape, BF16) for a in to_cast],
        scratch_shapes=[pltpu.VMEM((HG_HEADS, HG_VDIM, HG_KDIM), F32),
                        pltpu.VMEM((CARRY_ROWS, POOL_WIDTH), F32)],
        compiler_params=pltpu.CompilerParams(dimension_semantics=("arbitrary", "arbitrary"),
                                             vmem_limit_bytes=VMEM_LIMIT),
        name="mixer_prompt",
    )(x, mk, mv, *weights, *to_cast)
    return outs[:3], outs[3:]


def _conv_gate(full, tail, u, cw, cb):
    c = cb + pltpu.roll(full, 2, 0) * cw[0:1] + pltpu.roll(full, 1, 0) * cw[1:2] + full * cw[2:3]
    return _gelu(tail(c)) * u


def _ffn_prompt_kernel(h_ref, wg_ref, wu_ref, cw_ref, cb_ref, wd_ref, g_ref, b_ref,
                       y_ref, newconv_ref, carry_scr, *, alpha):
    T = FFN_T
    j = pl.program_id(1)

    @pl.when(j == 0)
    def _():
        carry_scr[...] = jnp.zeros_like(carry_scr)

    h = h_ref[0]
    hb = h.astype(BF16)
    a = jnp.dot(hb, wg_ref[...], preferred_element_type=F32)
    u = jnp.dot(hb, wu_ref[...], preferred_element_type=F32)
    full = jnp.concatenate([carry_scr[...], a], axis=0)
    gated = _conv_gate(full, lambda z: z[CONV_CARRY:], u, cw_ref[...], cb_ref[...])
    _project_norm(_rows_of(y_ref.at[0]), h, gated, wd_ref[...], g_ref[...], b_ref[...], alpha)
    carry_scr[...] = a[T - CONV_CARRY:]

    @pl.when(j == pl.num_programs(1) - 1)
    def _():
        newconv_ref[0] = carry_scr[...]


def _ffn_prompt(h, w, alpha):
    bsz, seq, _ = h.shape
    T = FFN_T
    weights = [w['w_gate'], w['w_up'], w['conv_w'], w['conv_b'], w['w_down'], w['ln2_g'], w['ln2_b']]
    return pl.pallas_call(
        functools.partial(_ffn_prompt_kernel, alpha=alpha),
        grid=(bsz, seq // T),
        in_specs=[pl.BlockSpec((1, T, D_MODEL), lambda b, j: (b, j, 0))]
                 + [_const_spec(a.shape) for a in weights],
        out_specs=[pl.BlockSpec((1, T, D_MODEL), lambda b, j: (b, j, 0)),
                   pl.BlockSpec((1, CONV_CARRY, D_FF), lambda b, j: (b, 0, 0))],
        out_shape=[jax.ShapeDtypeStruct((bsz, seq, D_MODEL), F32),
                   jax.ShapeDtypeStruct((bsz, CONV_CARRY, D_FF), F32)],
        scratch_shapes=[pltpu.VMEM((CONV_CARRY, D_FF), F32)],
        compiler_params=pltpu.CompilerParams(dimension_semantics=("arbitrary", "arbitrary"),
                                             vmem_limit_bytes=VMEM_LIMIT),
        name="ffn_prompt",
    )(h, *weights)


def _bmm(spec, a, b):
    return jnp.einsum(spec, a.astype(BF16), b.astype(BF16), preferred_element_type=F32)


def _mixer_sample_kernel(x_ref, mk_ref, mv_ref, pool_ref, hg_ref, lb_ref, win_ref, bd_ref, pscale_ref, hgg_ref,
                         wbp_ref, wbh_ref, wbm_ref, wout_ref, g_ref, b_ref,
                         h_ref, newpool_ref, newhg_ref, seq_scr, out_scr, *, layer, pos0, alpha, seq):
    Bb, S = SAMPLE_BB, SUBLANES
    R = Bb * S
    M = seq * Bb
    slab = lambda z, t: z[t * Bb:(t + 1) * Bb]
    x = jnp.concatenate([x_ref[:, t, :] for t in range(seq)], axis=0)
    xb = x.astype(BF16)
    proj = lambda c: jnp.dot(xb, win_ref[:, c[0]:c[1]], preferred_element_type=F32)

    u_a = proj(C_UA)
    rows = [pool_ref[i] for i in range(POOL_BUF)] + [slab(u_a, t) for t in range(seq)]
    n_rows = len(rows)
    sums = {1: dict(enumerate(rows))}
    for w in POOL_WINDOWS:
        half = sums[w // 2]
        sums[w] = {i: half[i] + half[i - w // 2] for i in range(n_rows) if i - w // 2 in half and i in half}
    grp = lax.broadcasted_iota(jnp.int32, (Bb, POOL_WIDTH), 1) // POOL_GDIM
    diffs = []
    for t in range(seq):
        i = POOL_BUF + t
        pooled = [sums[w][i] / float(min(pos0 + t + 1, w)) for w in POOL_WINDOWS]
        mean = jnp.where(grp == 0, pooled[0], jnp.where(grp == 1, pooled[1], jnp.where(grp == 2, pooled[2], pooled[3])))
        diffs.append(mean - rows[i])
    y_a = _mm(jnp.concatenate(diffs, axis=0), bd_ref[...]) * pscale_ref[...]
    for i in range(POOL_BUF):
        newpool_ref[i] = rows[n_rows - POOL_BUF + i]

    lb = _lower_bound(lb_ref[...], layer)
    logf_t, k_t = _forget_gates(proj(C_FB), lb)
    per_seq_in = jnp.concatenate([logf_t, k_t, _silu_from_half(proj(C_QB)), proj(C_IB), proj(C_QC)], axis=1)
    seq_scr[...] = jnp.zeros_like(seq_scr)
    for c in range(seq_scr.shape[0]):
        for t in range(seq):
            seq_scr[c, pl.ds(t, Bb, stride=S), :] = slab(per_seq_in, t)[:, c * LANES:(c + 1) * LANES]
    cols = lambda lo, hi: jnp.concatenate([seq_scr[c] for c in range(lo // LANES, hi // LANES)], axis=1)
    logf = cols(0, HG_WIDTH)
    k = cols(HG_WIDTH, 2 * HG_WIDTH)
    q = cols(2 * HG_WIDTH, 3 * HG_WIDTH)
    v = cols(3 * HG_WIDTH, 3 * HG_WIDTH + HG_VWIDTH)
    qc = cols(3 * HG_WIDTH + HG_VWIDTH, 3 * HG_WIDTH + HG_VWIDTH + MEM_WIDTH)
    to3 = lambda z: z.reshape(Bb, S, z.shape[-1])
    b3 = to3(_select_mm(_as_bf16(_chunk_tri(R, S)), logf))
    bm = b3[:, S // 2:S // 2 + 1, :]
    bl = b3[:, S - 1:S, :]
    q3, k3, v3 = to3(q), to3(k), to3(v)
    qin = q3 * jnp.exp(b3)
    qd = q3 * jnp.exp(b3 - bm)
    kd = k3 * jnp.exp(bm - b3)
    kl = k3 * jnp.exp(bl - b3)
    rr = lax.broadcasted_iota(jnp.int32, (Bb, S, S), 1)
    cc = lax.broadcasted_iota(jnp.int32, (Bb, S, S), 2)
    causal = cc <= rr
    p1 = bl.astype(BF16).astype(F32)
    p2 = (bl - p1).astype(BF16).astype(F32)
    p3 = (bl - p1) - p2
    r3 = lax.broadcasted_iota(jnp.int32, (Bb, S, HG_WIDTH), 1)
    pieces = jnp.where(r3 == 0, p1, jnp.where(r3 == 1, p2, jnp.where(r3 == 2, p3, 0.0)))
    ones = jnp.ones((Bb, S, HG_VDIM), BF16)
    hgg = hgg_ref[...]
    heads = []
    for h in range(HG_HEADS):
        sl = slice(h * HG_KDIM, (h + 1) * HG_KDIM)
        sc = jnp.where(causal, _bmm('bqd,bkd->bqk', qd[:, :, sl], kd[:, :, sl]), 0.0)
        s0 = hg_ref[:, h]
        o = _bmm('bqk,bke->bqe', sc, v3[:, :, sl]) + _bmm('bqd,bde->bqe', qin[:, :, sl], s0)
        logdec = _bmm('bkd,bke->bde', pieces[:, :, sl], ones)
        newhg_ref[:, h] = jnp.exp(logdec) * s0 + _bmm('bkd,bke->bde', kl[:, :, sl], v3[:, :, sl])
        heads.append(_rms_head(o, hgg[:, sl]).reshape(R, HG_VDIM))

    qc3 = to3(qc)
    head_of_lane = lax.broadcasted_iota(jnp.int32, (Bb, S, MEM_WIDTH), 2) // MEM_HDIM
    q4 = jnp.concatenate([jnp.where(head_of_lane == h, qc3, 0.0) for h in range(MEM_HEADS)], axis=1)
    s = _bmm('bqd,bdm->bqm', q4, mk_ref[...]) * (MEM_HDIM ** -0.5)
    e = jnp.exp(s - jnp.max(s, axis=-1, keepdims=True))
    p = e / jnp.sum(e, axis=-1, keepdims=True)
    o4 = _bmm('bqm,bdm->bqd', p, mv_ref[...])
    y_c = jnp.zeros((Bb, S, MEM_WIDTH), F32)
    for h in range(MEM_HEADS):
        y_c = y_c + jnp.where(head_of_lane == h, o4[:, h * S:(h + 1) * S, :], 0.0)

    staged = jnp.concatenate(heads + [y_c.reshape(R, MEM_WIDTH)], axis=1)
    for c in range(out_scr.shape[0]):
        out_scr[c] = staged[:, c * LANES:(c + 1) * LANES]
    per_seq_out = jnp.concatenate(
        [jnp.concatenate([out_scr[c, pl.ds(t, Bb, stride=S), :] for c in range(out_scr.shape[0])], axis=1)
         for t in range(seq)], axis=0)
    y_b = per_seq_out[:, :HG_VWIDTH] * _silu_from_half(proj(C_GB))
    y_c = per_seq_out[:, HG_VWIDTH:]
    h = _merge_out(x, y_a, y_b, y_c, proj(C_GATE), wbp_ref[...], wbh_ref[...], wbm_ref[...], wout_ref[...],
                   g_ref[...], b_ref[...], alpha)
    h_ref[...] = h.reshape(seq, Bb, D_MODEL)


def _mixer_sample(x, mk, mv, pool_t, hg, lb_logits, w, layer, pos0, alpha):
    Bb, S = SAMPLE_BB, SUBLANES
    bsz, seq, _ = x.shape
    n_mem = mk.shape[2]
    b3 = lambda i: (i, 0, 0)
    tb3 = lambda i: (0, i, 0)
    weights = [lb_logits, w['w_in'], w['bd_pool'], w['pool_scale'], w['hg_norm_g'], w['w_br_pool'], w['w_br_hg'],
               w['w_br_mem'], w['w_out'], w['ln1_g'], w['ln1_b']]
    per_seq_in = 3 * HG_WIDTH + HG_VWIDTH + MEM_WIDTH
    per_seq_out = HG_VWIDTH + MEM_WIDTH
    return pl.pallas_call(
        functools.partial(_mixer_sample_kernel, layer=layer, pos0=pos0, alpha=alpha, seq=seq),
        grid=(bsz // Bb,),
        in_specs=[pl.BlockSpec((Bb, seq, D_MODEL), b3),
                  pl.BlockSpec((Bb, MEM_WIDTH, n_mem), b3),
                  pl.BlockSpec((Bb, MEM_WIDTH, n_mem), b3),
                  pl.BlockSpec((POOL_BUF, Bb, POOL_WIDTH), tb3),
                  pl.BlockSpec((Bb, HG_HEADS, HG_KDIM, HG_VDIM), lambda i: (i, 0, 0, 0))]
                 + [_const_spec(a.shape) for a in weights],
        out_specs=[pl.BlockSpec((seq, Bb, D_MODEL), tb3),
                   pl.BlockSpec((POOL_BUF, Bb, POOL_WIDTH), tb3),
                   pl.BlockSpec((Bb, HG_HEADS, HG_KDIM, HG_VDIM), lambda i: (i, 0, 0, 0))],
        out_shape=[jax.ShapeDtypeStruct((seq, bsz, D_MODEL), F32),
                   jax.ShapeDtypeStruct((POOL_BUF, bsz, POOL_WIDTH), F32),
                   jax.ShapeDtypeStruct((bsz, HG_HEADS, HG_KDIM, HG_VDIM), F32)],
        scratch_shapes=[pltpu.VMEM((per_seq_in // LANES, Bb * S, LANES), F32),
                        pltpu.VMEM((per_seq_out // LANES, Bb * S, LANES), F32)],
        compiler_params=pltpu.CompilerParams(dimension_semantics=("arbitrary",), vmem_limit_bytes=VMEM_LIMIT),
        name="mixer_sample",
    )(x, mk, mv, pool_t, hg, *weights)


def _ffn_sample_kernel(h_ref, conv_ref, wg_ref, wu_ref, cw_ref, cb_ref, wd_ref, g_ref, b_ref,
                       y_ref, newconv_ref, *, alpha, seq):
    bsz = conv_ref.shape[0]
    h = h_ref[...]
    hb = h.astype(BF16)
    a = jnp.dot(hb, wg_ref[...], preferred_element_type=F32)
    u = jnp.dot(hb, wu_ref[...], preferred_element_type=F32)
    rows = [conv_ref[:, i, :] for i in range(CONV_W - 1)] + [a[t * bsz:(t + 1) * bsz] for t in range(seq)]
    cw = cw_ref[...]
    c = jnp.concatenate([cb_ref[...] + sum(rows[t + i] * cw[i:i + 1] for i in range(CONV_W)) for t in range(seq)],
                        axis=0)

    def store(first_row, val):
        for t in range(val.shape[0] // bsz):
            y_ref[:, first_row // bsz + t, :] = val[t * bsz:(t + 1) * bsz]

    _project_norm(store, h, _gelu(c) * u, wd_ref[...], g_ref[...], b_ref[...], alpha,
                  group_rows=bsz * max(1, NORM_ROWS // bsz))
    for i in range(CONV_W - 1):
        newconv_ref[:, i, :] = rows[seq + i]


def _ffn_sample(h_t2d, conv, w, alpha):
    n_rows = h_t2d.shape[0]
    bsz = conv.shape[0]
    weights = [w['w_gate'], w['w_up'], w['conv_w'], w['conv_b'], w['w_down'], w['ln2_g'], w['ln2_b']]
    full = lambda shape: pl.BlockSpec(shape, lambda i: (0,) * len(shape))
    y_shape = (bsz, n_rows // bsz, D_MODEL)
    return pl.pallas_call(
        functools.partial(_ffn_sample_kernel, alpha=alpha, seq=n_rows // bsz),
        grid=(1,),
        in_specs=[full(h_t2d.shape), full(conv.shape)] + [_const_spec(a.shape) for a in weights],
        out_specs=[full(y_shape), full(conv.shape)],
        out_shape=[jax.ShapeDtypeStruct(y_shape, F32), jax.ShapeDtypeStruct(conv.shape, F32)],
        compiler_params=pltpu.CompilerParams(dimension_semantics=("arbitrary",), vmem_limit_bytes=VMEM_LIMIT),
        name="ffn_sample",
    )(h_t2d, conv, *weights)


def _block_diag(w_grp):
    groups, gdim, _ = w_grp.shape
    out = jnp.zeros((groups * gdim, groups * gdim), w_grp.dtype)
    for g in range(groups):
        out = lax.dynamic_update_slice(out, w_grp[g], (g * gdim, g * gdim))
    return out


def kernel(x_prompt, x_sample, state_pool, state_hgrn, state_ffn_conv, cache_mem_k, cache_mem_v, mem_prompt, lb_logits, w_in, w_pool_grp, pool_scale, hg_norm_g, w_mem_k, w_mem_v, w_br_pool, w_br_hg, w_br_mem, w_out, ln1_g, ln1_b, w_gate, w_up, conv_w, conv_b, w_down, ln2_g, ln2_b):
    depth = w_in.shape[0]
    alpha = (2 * depth) ** 0.25
    n_prompt, seq_p, _ = x_prompt.shape
    n_sample, seq_s, _ = x_sample.shape
    n_mem = mem_prompt.shape[1]
    past_len = 16384
    assert seq_p % PROMPT_T == 0 and seq_p % FFN_T == 0 and seq_s <= SUBLANES and n_sample % SAMPLE_BB == 0

    hp = x_prompt
    time_major = lambda a: jnp.transpose(a, (1, 0, 2))
    hs = x_sample
    mem2d = mem_prompt.reshape(n_prompt * n_mem, D_MODEL)
    row = lambda a: a.reshape(1, -1)
    col = jnp.arange(w_in.shape[-1])
    in_range = lambda c: (col >= c[0]) & (col < c[1])
    half_cols = jnp.where(in_range(C_QB) | in_range(C_FB) | in_range(C_GB) | in_range(C_GATE), 0.5, 1.0).astype(F32)
    outs = [[] for _ in range(8)]
    for l in range(depth):
        w = {'bd_pool': _block_diag(w_pool_grp[l]).astype(BF16),
             'pool_scale': row(pool_scale[l]), 'hg_norm_g': row(hg_norm_g[l]),
             'ln1_g': row(ln1_g[l]), 'ln1_b': row(ln1_b[l]),
             'conv_w': conv_w[l], 'conv_b': row(conv_b[l]), 'ln2_g': row(ln2_g[l]), 'ln2_b': row(ln2_b[l])}
        mkt, mvt, (w['w_in'], w['w_out'], w['w_br_pool'], w['w_br_hg'], w['w_br_mem']) = _memkv(
            mem2d, w_mem_k[l], w_mem_v[l], n_mem,
            [(w_in[l], half_cols), (w_out[l], jnp.full((D_MODEL,), 0.5, F32)), (w_br_pool[l], None),
             (w_br_hg[l], None), (w_br_mem[l], None)])
        (h_mid, pool_p, hg_p), (w['w_gate'], w['w_up'], w['w_down']) = _mixer_prompt(
            hp, mkt, mvt, lb_logits, w, l, 0, alpha, (w_gate[l], w_up[l], w_down[l]))
        hp, conv_p = _ffn_prompt(h_mid, w, alpha)
        outs[0].append(pool_p[:, CARRY_ROWS - POOL_BUF:])
        outs[1].append(hg_p)
        outs[2].append(conv_p[:, CONV_CARRY - (CONV_W - 1):])
        outs[3].append(_token_major(mkt))
        outs[4].append(_token_major(mvt))
        hs_mid, pool_s, hg_s = _mixer_sample(
            hs, _feature_major(cache_mem_k[l]), _feature_major(cache_mem_v[l]),
            time_major(state_pool[l]), state_hgrn[l], lb_logits, w, l, past_len, alpha)
        hs, conv_s = _ffn_sample(hs_mid.reshape(seq_s * n_sample, D_MODEL), state_ffn_conv[l], w, alpha)
        outs[5].append(time_major(pool_s))
        outs[6].append(hg_s)
        outs[7].append(conv_s)
    return (hp, hs) + tuple(jnp.stack(o) for o in outs)
```

```python
import functools

import jax
import jax.numpy as jnp
from jax import lax
from jax.experimental import pallas as pl
from jax.experimental.pallas import tpu as pltpu

F32 = jnp.float32
BF16 = jnp.bfloat16

D_MODEL = 1024
POOL_WIDTH = 256
POOL_WINDOWS = (2, 4, 8, 16)
POOL_GDIM = 64
POOL_BUF = 15
HG_HEADS = 4
HG_KDIM = 128
HG_WIDTH = 512
HG_VDIM = 128
HG_VWIDTH = 512
MEM_HEADS = 4
MEM_HDIM = 64
MEM_WIDTH = 256
N_BRANCH = 3
D_FF = 2816
CONV_W = 3
LN_EPS = 1e-5
RMS_EPS = 1e-6

C_UA = (0, 256)
C_QB = (256, 768)
C_FB = (768, 1280)
C_IB = (1280, 1792)
C_GB = (1792, 2304)
C_QC = (2304, 2560)
C_GATE = (2560, 5632)

LANES = 128
SUBLANES = 8
BF16_TILE_ROWS = 16
CARRY_ROWS = 16
CONV_CARRY = 8
PROMPT_T = 512
FFN_T = 1024
NORM_ROWS = 256
HG_BLOCK = 256
HG_CHUNK = 64
SAMPLE_BB = 16
VMEM_LIMIT = 56 * 1024 * 1024


def _mm(a, b):
    return jnp.dot(a.astype(BF16), b.astype(BF16), preferred_element_type=F32)


def _mm_nt(a, b):
    return lax.dot_general(a.astype(BF16), b.astype(BF16), (((1,), (1,)), ((), ())), preferred_element_type=F32)


def _mm_tn(a, b):
    return lax.dot_general(a.astype(BF16), b.astype(BF16), (((0,), (0,)), ((), ())), preferred_element_type=F32)


def _split3(x):
    h1 = x.astype(BF16)
    r1 = x - h1.astype(F32)
    h2 = r1.astype(BF16)
    h3 = (r1 - h2.astype(F32)).astype(BF16)
    return h1, h2, h3


def _select_mm(sel, x):
    h1, h2, h3 = _split3(x)
    d = lambda p: jnp.dot(sel, p, preferred_element_type=F32)
    return d(h1) + d(h2) + d(h3)


def _silu_from_half(p):
    return p + p * jnp.tanh(p)


def _gated(p, m):
    return m + jnp.tanh(p) * m


def _layer_norm(x, g, b):
    mu = jnp.mean(x, axis=-1, keepdims=True)
    xc = x - mu
    var = jnp.mean(xc * xc, axis=-1, keepdims=True)
    return xc * lax.rsqrt(var + LN_EPS) * g + b


def _gelu(x):
    return 0.5 * x * (1.0 + lax.erf(x * (2.0 ** -0.5)))


def _lower_bound(lb_logits, layer):
    m = jnp.max(lb_logits, axis=0, keepdims=True)
    e = jnp.exp(lb_logits - m)
    sm = e / jnp.sum(e, axis=0, keepdims=True)
    return jnp.sum(sm[:layer + 1], axis=0, keepdims=True)


def _pool_diff(full, u_a, pos, tail):
    s2 = full + pltpu.roll(full, 1, 0)
    s4 = s2 + pltpu.roll(s2, 2, 0)
    s8 = s4 + pltpu.roll(s4, 4, 0)
    s16 = s8 + pltpu.roll(s8, 8, 0)
    grp = lax.broadcasted_iota(jnp.int32, u_a.shape, 1) // POOL_GDIM
    wsum = jnp.where(grp == 0, tail(s2), jnp.where(grp == 1, tail(s4), jnp.where(grp == 2, tail(s8), tail(s16))))
    wlen = jnp.where(grp == 0, 2, jnp.where(grp == 1, 4, jnp.where(grp == 2, 8, 16)))
    count = jnp.minimum(pos + 1, wlen).astype(F32)
    return wsum / count - u_a


def _forget_gates(fb_half, lb):
    th = jnp.tanh(fb_half)
    f = lb + (1.0 - lb) * (0.5 + 0.5 * th)
    k = (1.0 - lb) * (0.5 - 0.5 * th)
    return jnp.log(f), k


def _chunk_tri(n, chunk):
    r = lax.broadcasted_iota(jnp.int32, (n, n), 0)
    c = lax.broadcasted_iota(jnp.int32, (n, n), 1)
    return (r // chunk == c // chunk) & (c <= r)


def _as_bf16(mask):
    return jnp.where(mask, 1.0, 0.0).astype(BF16)


def _rms_head(o, g):
    return o * lax.rsqrt(jnp.mean(o * o, axis=-1, keepdims=True) + RMS_EPS) * g


def _project_norm(store, resid, lhs, w, g, b, alpha, group_rows=NORM_ROWS):
    n = resid.shape[0]
    step = min(n, group_rows)
    for i in range(0, n, step):
        store(i, _layer_norm(alpha * resid[i:i + step] + _mm(lhs[i:i + step], w), g, b))


def _rows_of(ref2d):
    def store(first_row, val):
        ref2d[first_row:first_row + val.shape[0], :] = val
    return store


def _merge_out(x, y_a, y_b, y_c, gate_pre, wbp, wbh, wbm, wout, g, b, alpha):
    merged2 = (_gated(gate_pre[:, 0:D_MODEL], _mm(y_a, wbp))
               + _gated(gate_pre[:, D_MODEL:2 * D_MODEL], _mm(y_b, wbh))
               + _gated(gate_pre[:, 2 * D_MODEL:3 * D_MODEL], _mm(y_c, wbm)))
    return _layer_norm(alpha * x + _mm(merged2, wout), g, b)


def _cast_block_count(rows, n_steps):
    return max(n for n in range(1, n_steps + 1) if rows % n == 0 and (rows // n) % BF16_TILE_ROWS == 0)


def _ride_along_casts(step, srcs, scales, dsts, block_counts):
    for src, scale, dst, n_blocks in zip(srcs, scales, dsts, block_counts):
        @pl.when(step < n_blocks)
        def _(src=src, scale=scale, dst=dst):
            val = src[...] if scale is None else src[...] * scale[...]
            dst[...] = val.astype(BF16)


def _memkv_kernel(mem_ref, wk_ref, wv_ref, *rest, cast_blocks, scaled):
    n_cast = len(cast_blocks)
    cast_in, scale_refs = rest[:n_cast], list(rest[n_cast:n_cast + sum(scaled)])
    kt_ref, vt_ref = rest[n_cast + sum(scaled):n_cast + sum(scaled) + 2]
    cast_out = rest[n_cast + sum(scaled) + 2:]
    mt = mem_ref[...].T
    kt_ref[0] = _mm_tn(wk_ref[...], mt)
    vt_ref[0] = _mm_tn(wv_ref[...], mt)
    scales = [scale_refs.pop(0) if s else None for s in scaled]
    _ride_along_casts(pl.program_id(0), cast_in, scales, cast_out, cast_blocks)


def _memkv(mem2d, wk, wv, n_mem, to_cast):
    n_seq = mem2d.shape[0] // n_mem
    const = lambda i: (0, 0)
    per_seq = lambda i: (i, 0, 0)
    arrays = [a for a, _ in to_cast]
    scale_rows = [s.reshape(1, -1) for _, s in to_cast if s is not None]
    cast_blocks = tuple(_cast_block_count(a.shape[0], n_seq) for a in arrays)
    cast_specs = [pl.BlockSpec((a.shape[0] // n, a.shape[1]), lambda i, n=n: (jnp.minimum(i, n - 1), 0))
                  for a, n in zip(arrays, cast_blocks)]
    outs = pl.pallas_call(
        functools.partial(_memkv_kernel, cast_blocks=cast_blocks, scaled=tuple(s is not None for _, s in to_cast)),
        grid=(n_seq,),
        in_specs=[pl.BlockSpec((n_mem, D_MODEL), lambda i: (i, 0)),
                  pl.BlockSpec((D_MODEL, MEM_WIDTH), const),
                  pl.BlockSpec((D_MODEL, MEM_WIDTH), const)]
                 + cast_specs + [pl.BlockSpec(s.shape, const) for s in scale_rows],
        out_specs=[pl.BlockSpec((1, MEM_WIDTH, n_mem), per_seq),
                   pl.BlockSpec((1, MEM_WIDTH, n_mem), per_seq)] + cast_specs,
        out_shape=[jax.ShapeDtypeStruct((n_seq, MEM_WIDTH, n_mem), F32)] * 2
                  + [jax.ShapeDtypeStruct(a.shape, BF16) for a in arrays],
        name="memkv",
    )(mem2d, wk, wv, *arrays, *scale_rows)
    return outs[0], outs[1], outs[2:]


def _feature_major(mem):
    return jnp.transpose(mem, (0, 2, 3, 1)).reshape(mem.shape[0], MEM_WIDTH, mem.shape[1])


def _token_major(mem_t):
    bsz, _, n_mem = mem_t.shape
    return jnp.transpose(mem_t.reshape(bsz, MEM_HEADS, MEM_HDIM, n_mem), (0, 3, 1, 2))


def _mixer_prompt_kernel(x_ref, mk_ref, mv_ref, lb_ref, win_ref, bd_ref, pscale_ref, hgg_ref,
                         wbp_ref, wbh_ref, wbm_ref, wout_ref, g_ref, b_ref, *rest, layer, pos0, alpha, cast_blocks):
    n_cast = len(cast_blocks)
    cast_in, (h_ref, newpool_ref, newhg_ref) = rest[:n_cast], rest[n_cast:n_cast + 3]
    cast_out, (st_scr, pool_scr) = rest[n_cast + 3:2 * n_cast + 3], rest[2 * n_cast + 3:]
    T, C = PROMPT_T, HG_CHUNK
    j = pl.program_id(1)

    _ride_along_casts(pl.program_id(0) * pl.num_programs(1) + j, cast_in, [None] * n_cast, cast_out, cast_blocks)

    @pl.when(j == 0)
    def _():
        pool_scr[...] = jnp.zeros_like(pool_scr)
        st_scr[...] = jnp.zeros_like(st_scr)

    x = x_ref[0]
    xb = x.astype(BF16)
    proj = lambda c: jnp.dot(xb, win_ref[:, c[0]:c[1]], preferred_element_type=F32)

    lb = _lower_bound(lb_ref[...], layer)
    logf, k = _forget_gates(proj(C_FB), lb)
    qb = proj(C_QB)
    v = proj(C_IB)
    same_chunk_causal = _chunk_tri(HG_BLOCK, C)
    tri = _as_bf16(same_chunk_causal)
    blocks = [slice(i * HG_BLOCK, (i + 1) * HG_BLOCK) for i in range(T // HG_BLOCK)]
    bcum = jnp.concatenate([_select_mm(tri, logf[bs]) for bs in blocks], axis=0)
    u_a = proj(C_UA)
    qc = proj(C_QC)
    gb = proj(C_GB)
    gate_a = proj((C_GATE[0], C_GATE[0] + D_MODEL))

    full = jnp.concatenate([pool_scr[...], u_a], axis=0)
    pos = pos0 + j * T + lax.broadcasted_iota(jnp.int32, (T, POOL_WIDTH), 0)
    diff_a = _pool_diff(full, u_a, pos, lambda z: z[CARRY_ROWS:])
    pool_scr[...] = full[T:]

    mk = mk_ref[0].astype(BF16)
    mv = mv_ref[0].astype(BF16)
    head_of_lane = lax.broadcasted_iota(jnp.int32, (T, MEM_WIDTH), 1) // MEM_HDIM
    att = [_mm(jnp.where(head_of_lane == h, qc, 0.0), mk) * (MEM_HDIM ** -0.5) for h in range(MEM_HEADS)]
    gate_b = proj((C_GATE[0] + D_MODEL, C_GATE[0] + 2 * D_MODEL))

    q = _silu_from_half(qb)
    n_chunks = T // C
    chunk_rows = [slice(c * C, (c + 1) * C) for c in range(n_chunks)]
    chunk_row = lambda c, row: bcum[c * C + row:c * C + row + 1]
    per_chunk = lambda fn: jnp.concatenate([jnp.broadcast_to(fn(c), (C, HG_WIDTH)) for c in range(n_chunks)], axis=0)
    bm = per_chunk(lambda c: chunk_row(c, C // 2))
    qin = q * jnp.exp(bcum)
    qd = qin * per_chunk(lambda c: jnp.exp(-chunk_row(c, C // 2)))
    kd = k * jnp.exp(bm - bcum)
    kl = kd * per_chunk(lambda c: jnp.exp(chunk_row(c, C - 1) - chunk_row(c, C // 2)))
    dec = [jnp.exp(chunk_row(c, C - 1)) for c in range(n_chunks)]
    hgg = hgg_ref[...]
    head_lanes = [slice(h * HG_KDIM, (h + 1) * HG_KDIM) for h in range(HG_HEADS)]
    sc = [[jnp.where(same_chunk_causal, _mm_nt(qd[bs, sl], kd[bs, sl]), 0.0) for bs in blocks] for sl in head_lanes]
    grow = [[_mm_tn(v[rs, sl], kl[rs, sl]) for rs in chunk_rows] for sl in head_lanes]
    y_a = _mm(diff_a, bd_ref[...]) * pscale_ref[...]
    gate_c = proj((C_GATE[0] + 2 * D_MODEL, C_GATE[1]))

    y_c = jnp.zeros((T, MEM_WIDTH), F32)
    inv_sum = jnp.zeros((T, MEM_WIDTH), F32)
    for h in range(MEM_HEADS):
        e = jnp.exp(att[h] - jnp.max(att[h], axis=-1, keepdims=True))
        y_c = y_c + jnp.where(head_of_lane == h, _mm_nt(e, mv), 0.0)
        inv_sum = inv_sum + jnp.where(head_of_lane == h, 1.0 / jnp.sum(e, axis=-1, keepdims=True), 0.0)
    y_c = y_c * inv_sum
    m_a = _mm(y_a, wbp_ref[...])

    heads = []
    for h, sl in enumerate(head_lanes):
        o_intra = jnp.concatenate([_mm(sc[h][i], v[bs, sl]) for i, bs in enumerate(blocks)], axis=0)
        st = st_scr[h]
        o_inter = []
        for c, rs in enumerate(chunk_rows):
            o_inter.append(_mm_nt(qin[rs, sl], st))
            st = st * dec[c][:, sl] + grow[h][c]
        st_scr[h] = st
        heads.append(_rms_head(o_intra + jnp.concatenate(o_inter, axis=0), hgg[:, sl]))
    m_c = _mm(y_c, wbm_ref[...])
    y_b = jnp.concatenate(heads, axis=1) * _silu_from_half(gb)
    merged2 = _gated(gate_a, m_a) + _gated(gate_c, m_c) + _gated(gate_b, _mm(y_b, wbh_ref[...]))
    _project_norm(_rows_of(h_ref.at[0]), x, merged2, wout_ref[...], g_ref[...], b_ref[...], alpha)

    @pl.when(j == pl.num_programs(1) - 1)
    def _():
        newpool_ref[0] = pool_scr[...]
        for h in range(HG_HEADS):
            newhg_ref[0, h] = st_scr[h].T


def _const_spec(shape):
    n = len(shape)
    return pl.BlockSpec(shape, lambda *_: (0,) * n, pipeline_mode=pl.Buffered(1))


def _mixer_prompt(x, mk, mv, lb_logits, w, layer, pos0, alpha, to_cast):
    bsz, seq, _ = x.shape
    T = PROMPT_T
    n_j = seq // T
    per_b3 = lambda b, j: (b, 0, 0)
    cast_blocks = tuple(_cast_block_count(a.shape[0], bsz * n_j) for a in to_cast)
    cast_specs = [pl.BlockSpec((a.shape[0] // n, a.shape[1]), lambda b, j, n=n: (jnp.minimum(b * n_j + j, n - 1), 0))
                  for a, n in zip(to_cast, cast_blocks)]
    kern = functools.partial(_mixer_prompt_kernel, layer=layer, pos0=pos0, alpha=alpha, cast_blocks=cast_blocks)
    weights = [lb_logits, w['w_in'], w['bd_pool'], w['pool_scale'], w['hg_norm_g'], w['w_br_pool'], w['w_br_hg'],
               w['w_br_mem'], w['w_out'], w['ln1_g'], w['ln1_b']]
    outs = pl.pallas_call(
        kern,
        grid=(bsz, n_j),
        in_specs=[pl.BlockSpec((1, T, D_MODEL), lambda b, j: (b, j, 0)),
                  pl.BlockSpec((1,) + mk.shape[1:], per_b3),
                  pl.BlockSpec((1,) + mv.shape[1:], per_b3)]
                 + [_const_spec(a.shape) for a in weights] + cast_specs,
        out_specs=[pl.BlockSpec((1, T, D_MODEL), lambda b, j: (b, j, 0)),
                   pl.BlockSpec((1, CARRY_ROWS, POOL_WIDTH), per_b3),
                   pl.BlockSpec((1, HG_HEADS, HG_KDIM, HG_VDIM), lambda b, j: (b, 0, 0, 0))] + cast_specs,
        out_shape=[jax.ShapeDtypeStruct((bsz, seq, D_MODEL), F32),
                   jax.ShapeDtypeStruct((bsz, CARRY_ROWS, POOL_WIDTH), F32),
                   jax.ShapeDtypeStruct((bsz, HG_HEADS, HG_KDIM, HG_VDIM), F32)]
                  + [jax.ShapeDtypeStruct(a.shape, BF16) for a in to_cast],
        scratch_shapes=[pltpu.VMEM((HG_HEADS, HG_VDIM, HG_KDIM), F32),
                        pltpu.VMEM((CARRY_ROWS, POOL_WIDTH), F32)],
        compiler_params=pltpu.CompilerParams(dimension_semantics=("arbitrary", "arbitrary"),
                                             vmem_limit_bytes=VMEM_LIMIT),
        name="mixer_prompt",
    )(x, mk, mv, *weights, *to_cast)
    return outs[:3], outs[3:]


def _conv_gate(full, tail, u, cw, cb):
    c = cb + pltpu.roll(full, 2, 0) * cw[0:1] + pltpu.roll(full, 1, 0) * cw[1:2] + full * cw[2:3]
    return _gelu(tail(c)) * u


def _ffn_prompt_kernel(h_ref, wg_ref, wu_ref, cw_ref, cb_ref, wd_ref, g_ref, b_ref,
                       y_ref, newconv_ref, carry_scr, *, alpha):
    T = FFN_T
    j = pl.program_id(1)

    @pl.when(j == 0)
    def _():
        carry_scr[...] = jnp.zeros_like(carry_scr)

    h = h_ref[0]
    hb = h.astype(BF16)
    a = jnp.dot(hb, wg_ref[...], preferred_element_type=F32)
    u = jnp.dot(hb, wu_ref[...], preferred_element_type=F32)
    full = jnp.concatenate([carry_scr[...], a], axis=0)
    gated = _conv_gate(full, lambda z: z[CONV_CARRY:], u, cw_ref[...], cb_ref[...])
    _project_norm(_rows_of(y_ref.at[0]), h, gated, wd_ref[...], g_ref[...], b_ref[...], alpha)
    carry_scr[...] = a[T - CONV_CARRY:]

    @pl.when(j == pl.num_programs(1) - 1)
    def _():
        newconv_ref[0] = carry_scr[...]


def _ffn_prompt(h, w, alpha):
    bsz, seq, _ = h.shape
    T = FFN_T
    weights = [w['w_gate'], w['w_up'], w['conv_w'], w['conv_b'], w['w_down'], w['ln2_g'], w['ln2_b']]
    return pl.pallas_call(
        functools.partial(_ffn_prompt_kernel, alpha=alpha),
        grid=(bsz, seq // T),
        in_specs=[pl.BlockSpec((1, T, D_MODEL), lambda b, j: (b, j, 0))]
                 + [_const_spec(a.shape) for a in weights],
        out_specs=[pl.BlockSpec((1, T, D_MODEL), lambda b, j: (b, j, 0)),
                   pl.BlockSpec((1, CONV_CARRY, D_FF), lambda b, j: (b, 0, 0))],
        out_shape=[jax.ShapeDtypeStruct((bsz, seq, D_MODEL), F32),
                   jax.ShapeDtypeStruct((bsz, CONV_CARRY, D_FF), F32)],
        scratch_shapes=[pltpu.VMEM((CONV_CARRY, D_FF), F32)],
        compiler_params=pltpu.CompilerParams(dimension_semantics=("arbitrary", "arbitrary"),
                                             vmem_limit_bytes=VMEM_LIMIT),
        name="ffn_prompt",
    )(h, *weights)


def _bmm(spec, a, b):
    return jnp.einsum(spec, a.astype(BF16), b.astype(BF16), preferred_element_type=F32)


def _mixer_sample_kernel(x_ref, mk_ref, mv_ref, pool_ref, hg_ref, lb_ref, win_ref, bd_ref, pscale_ref, hgg_ref,
                         wbp_ref, wbh_ref, wbm_ref, wout_ref, g_ref, b_ref,
                         h_ref, newpool_ref, newhg_ref, seq_scr, out_scr, *, layer, pos0, alpha, seq):
    Bb, S = SAMPLE_BB, SUBLANES
    R = Bb * S
    M = seq * Bb
    slab = lambda z, t: z[t * Bb:(t + 1) * Bb]
    x = jnp.concatenate([x_ref[:, t, :] for t in range(seq)], axis=0)
    xb = x.astype(BF16)
    proj = lambda c: jnp.dot(xb, win_ref[:, c[0]:c[1]], preferred_element_type=F32)

    u_a = proj(C_UA)
    rows = [pool_ref[i] for i in range(POOL_BUF)] + [slab(u_a, t) for t in range(seq)]
    n_rows = len(rows)
    sums = {1: dict(enumerate(rows))}
    for w in POOL_WINDOWS:
        half = sums[w // 2]
        sums[w] = {i: half[i] + half[i - w // 2] for i in range(n_rows) if i - w // 2 in half and i in half}
    grp = lax.broadcasted_iota(jnp.int32, (Bb, POOL_WIDTH), 1) // POOL_GDIM
    diffs = []
    for t in range(seq):
        i = POOL_BUF + t
        pooled = [sums[w][i] / float(min(pos0 + t + 1, w)) for w in POOL_WINDOWS]
        mean = jnp.where(grp == 0, pooled[0], jnp.where(grp == 1, pooled[1], jnp.where(grp == 2, pooled[2], pooled[3])))
        diffs.append(mean - rows[i])
    y_a = _mm(jnp.concatenate(diffs, axis=0), bd_ref[...]) * pscale_ref[...]
    for i in range(POOL_BUF):
        newpool_ref[i] = rows[n_rows - POOL_BUF + i]

    lb = _lower_bound(lb_ref[...], layer)
    logf_t, k_t = _forget_gates(proj(C_FB), lb)
    per_seq_in = jnp.concatenate([logf_t, k_t, _silu_from_half(proj(C_QB)), proj(C_IB), proj(C_QC)], axis=1)
    seq_scr[...] = jnp.zeros_like(seq_scr)
    for c in range(seq_scr.shape[0]):
        for t in range(seq):
            seq_scr[c, pl.ds(t, Bb, stride=S), :] = slab(per_seq_in, t)[:, c * LANES:(c + 1) * LANES]
    cols = lambda lo, hi: jnp.concatenate([seq_scr[c] for c in range(lo // LANES, hi // LANES)], axis=1)
    logf = cols(0, HG_WIDTH)
    k = cols(HG_WIDTH, 2 * HG_WIDTH)
    q = cols(2 * HG_WIDTH, 3 * HG_WIDTH)
    v = cols(3 * HG_WIDTH, 3 * HG_WIDTH + HG_VWIDTH)
    qc = cols(3 * HG_WIDTH + HG_VWIDTH, 3 * HG_WIDTH + HG_VWIDTH + MEM_WIDTH)
    to3 = lambda z: z.reshape(Bb, S, z.shape[-1])
    b3 = to3(_select_mm(_as_bf16(_chunk_tri(R, S)), logf))
    bm = b3[:, S // 2:S // 2 + 1, :]
    bl = b3[:, S - 1:S, :]
    q3, k3, v3 = to3(q), to3(k), to3(v)
    qin = q3 * jnp.exp(b3)
    qd = q3 * jnp.exp(b3 - bm)
    kd = k3 * jnp.exp(bm - b3)
    kl = k3 * jnp.exp(bl - b3)
    rr = lax.broadcasted_iota(jnp.int32, (Bb, S, S), 1)
    cc = lax.broadcasted_iota(jnp.int32, (Bb, S, S), 2)
    causal = cc <= rr
    p1 = bl.astype(BF16).astype(F32)
    p2 = (bl - p1).astype(BF16).astype(F32)
    p3 = (bl - p1) - p2
    r3 = lax.broadcasted_iota(jnp.int32, (Bb, S, HG_WIDTH), 1)
    pieces = jnp.where(r3 == 0, p1, jnp.where(r3 == 1, p2, jnp.where(r3 == 2, p3, 0.0)))
    ones = jnp.ones((Bb, S, HG_VDIM), BF16)
    hgg = hgg_ref[...]
    heads = []
    for h in range(HG_HEADS):
        sl = slice(h * HG_KDIM, (h + 1) * HG_KDIM)
        sc = jnp.where(causal, _bmm('bqd,bkd->bqk', qd[:, :, sl], kd[:, :, sl]), 0.0)
        s0 = hg_ref[:, h]
        o = _bmm('bqk,bke->bqe', sc, v3[:, :, sl]) + _bmm('bqd,bde->bqe', qin[:, :, sl], s0)
        logdec = _bmm('bkd,bke->bde', pieces[:, :, sl], ones)
        newhg_ref[:, h] = jnp.exp(logdec) * s0 + _bmm('bkd,bke->bde', kl[:, :, sl], v3[:, :, sl])
        heads.append(_rms_head(o, hgg[:, sl]).reshape(R, HG_VDIM))

    qc3 = to3(qc)
    head_of_lane = lax.broadcasted_iota(jnp.int32, (Bb, S, MEM_WIDTH), 2) // MEM_HDIM
    q4 = jnp.concatenate([jnp.where(head_of_lane == h, qc3, 0.0) for h in range(MEM_HEADS)], axis=1)
    s = _bmm('bqd,bdm->bqm', q4, mk_ref[...]) * (MEM_HDIM ** -0.5)
    e = jnp.exp(s - jnp.max(s, axis=-1, keepdims=True))
    p = e / jnp.sum(e, axis=-1, keepdims=True)
    o4 = _bmm('bqm,bdm->bqd', p, mv_ref[...])
    y_c = jnp.zeros((Bb, S, MEM_WIDTH), F32)
    for h in range(MEM_HEADS):
        y_c = y_c + jnp.where(head_of_lane == h, o4[:, h * S:(h + 1) * S, :], 0.0)

    staged = jnp.concatenate(heads + [y_c.reshape(R, MEM_WIDTH)], axis=1)
    for c in range(out_scr.shape[0]):
        out_scr[c] = staged[:, c * LANES:(c + 1) * LANES]
    per_seq_out = jnp.concatenate(
        [jnp.concatenate([out_scr[c, pl.ds(t, Bb, stride=S), :] for c in range(out_scr.shape[0])], axis=1)
         for t in range(seq)], axis=0)
    y_b = per_seq_out[:, :HG_VWIDTH] * _silu_from_half(proj(C_GB))
    y_c = per_seq_out[:, HG_VWIDTH:]
    h = _merge_out(x, y_a, y_b, y_c, proj(C_GATE), wbp_ref[...], wbh_ref[...], wbm_ref[...], wout_ref[...],
                   g_ref[...], b_ref[...], alpha)
    h_ref[...] = h.reshape(seq, Bb, D_MODEL)


def _mixer_sample(x, mk, mv, pool_t, hg, lb_logits, w, layer, pos0, alpha):
    Bb, S = SAMPLE_BB, SUBLANES
    bsz, seq, _ = x.shape
    n_mem = mk.shape[2]
    b3 = lambda i: (i, 0, 0)
    tb3 = lambda i: (0, i, 0)
    weights = [lb_logits, w['w_in'], w['bd_pool'], w['pool_scale'], w['hg_norm_g'], w['w_br_pool'], w['w_br_hg'],
               w['w_br_mem'], w['w_out'], w['ln1_g'], w['ln1_b']]
    per_seq_in = 3 * HG_WIDTH + HG_VWIDTH + MEM_WIDTH
    per_seq_out = HG_VWIDTH + MEM_WIDTH
    return pl.pallas_call(
        functools.partial(_mixer_sample_kernel, layer=layer, pos0=pos0, alpha=alpha, seq=seq),
        grid=(bsz // Bb,),
        in_specs=[pl.BlockSpec((Bb, seq, D_MODEL), b3),
                  pl.BlockSpec((Bb, MEM_WIDTH, n_mem), b3),
                  pl.BlockSpec((Bb, MEM_WIDTH, n_mem), b3),
                  pl.BlockSpec((POOL_BUF, Bb, POOL_WIDTH), tb3),
                  pl.BlockSpec((Bb, HG_HEADS, HG_KDIM, HG_VDIM), lambda i: (i, 0, 0, 0))]
                 + [_const_spec(a.shape) for a in weights],
        out_specs=[pl.BlockSpec((seq, Bb, D_MODEL), tb3),
                   pl.BlockSpec((POOL_BUF, Bb, POOL_WIDTH), tb3),
                   pl.BlockSpec((Bb, HG_HEADS, HG_KDIM, HG_VDIM), lambda i: (i, 0, 0, 0))],
        out_shape=[jax.ShapeDtypeStruct((seq, bsz, D_MODEL), F32),
                   jax.ShapeDtypeStruct((POOL_BUF, bsz, POOL_WIDTH), F32),
                   jax.ShapeDtypeStruct((bsz, HG_HEADS, HG_KDIM, HG_VDIM), F32)],
        scratch_shapes=[pltpu.VMEM((per_seq_in // LANES, Bb * S, LANES), F32),
                        pltpu.VMEM((per_seq_out // LANES, Bb * S, LANES), F32)],
        compiler_params=pltpu.CompilerParams(dimension_semantics=("arbitrary",), vmem_limit_bytes=VMEM_LIMIT),
        name="mixer_sample",
    )(x, mk, mv, pool_t, hg, *weights)


def _ffn_sample_kernel(h_ref, conv_ref, wg_ref, wu_ref, cw_ref, cb_ref, wd_ref, g_ref, b_ref,
                       y_ref, newconv_ref, *, alpha, seq):
    bsz = conv_ref.shape[0]
    h = h_ref[...]
    hb = h.astype(BF16)
    a = jnp.dot(hb, wg_ref[...], preferred_element_type=F32)
    u = jnp.dot(hb, wu_ref[...], preferred_element_type=F32)
    rows = [conv_ref[:, i, :] for i in range(CONV_W - 1)] + [a[t * bsz:(t + 1) * bsz] for t in range(seq)]
    cw = cw_ref[...]
    c = jnp.concatenate([cb_ref[...] + sum(rows[t + i] * cw[i:i + 1] for i in range(CONV_W)) for t in range(seq)],
                        axis=0)

    def store(first_row, val):
        for t in range(val.shape[0] // bsz):
            y_ref[:, first_row // bsz + t, :] = val[t * bsz:(t + 1) * bsz]

    _project_norm(store, h, _gelu(c) * u, wd_ref[...], g_ref[...], b_ref[...], alpha,
                  group_rows=bsz * max(1, NORM_ROWS // bsz))
    for i in range(CONV_W - 1):
        newconv_ref[:, i, :] = rows[seq + i]


def _ffn_sample(h_t2d, conv, w, alpha):
    n_rows = h_t2d.shape[0]
    bsz = conv.shape[0]
    weights = [w['w_gate'], w['w_up'], w['conv_w'], w['conv_b'], w['w_down'], w['ln2_g'], w['ln2_b']]
    full = lambda shape: pl.BlockSpec(shape, lambda i: (0,) * len(shape))
    y_shape = (bsz, n_rows // bsz, D_MODEL)
    return pl.pallas_call(
        functools.partial(_ffn_sample_kernel, alpha=alpha, seq=n_rows // bsz),
        grid=(1,),
        in_specs=[full(h_t2d.shape), full(conv.shape)] + [_const_spec(a.shape) for a in weights],
        out_specs=[full(y_shape), full(conv.shape)],
        out_shape=[jax.ShapeDtypeStruct(y_shape, F32), jax.ShapeDtypeStruct(conv.shape, F32)],
        compiler_params=pltpu.CompilerParams(dimension_semantics=("arbitrary",), vmem_limit_bytes=VMEM_LIMIT),
        name="ffn_sample",
    )(h_t2d, conv, *weights)


def _block_diag(w_grp):
    groups, gdim, _ = w_grp.shape
    out = jnp.zeros((groups * gdim, groups * gdim), w_grp.dtype)
    for g in range(groups):
        out = lax.dynamic_update_slice(out, w_grp[g], (g * gdim, g * gdim))
    return out


def kernel(x_prompt, x_sample, state_pool, state_hgrn, state_ffn_conv, cache_mem_k, cache_mem_v, mem_prompt, lb_logits, w_in, w_pool_grp, pool_scale, hg_norm_g, w_mem_k, w_mem_v, w_br_pool, w_br_hg, w_br_mem, w_out, ln1_g, ln1_b, w_gate, w_up, conv_w, conv_b, w_down, ln2_g, ln2_b):
    depth = w_in.shape[0]
    alpha = (2 * depth) ** 0.25
    n_prompt, seq_p, _ = x_prompt.shape
    n_sample, seq_s, _ = x_sample.shape
    n_mem = mem_prompt.shape[1]
    past_len = 16384
    assert seq_p % PROMPT_T == 0 and seq_p % FFN_T == 0 and seq_s <= SUBLANES and n_sample % SAMPLE_BB == 0

    hp = x_prompt
    time_major = lambda a: jnp.transpose(a, (1, 0, 2))
    hs = x_sample
    mem2d = mem_prompt.reshape(n_prompt * n_mem, D_MODEL)
    row = lambda a: a.reshape(1, -1)
    col = jnp.arange(w_in.shape[-1])
    in_range = lambda c: (col >= c[0]) & (col < c[1])
    half_cols = jnp.where(in_range(C_QB) | in_range(C_FB) | in_range(C_GB) | in_range(C_GATE), 0.5, 1.0).astype(F32)
    outs = [[] for _ in range(8)]
    for l in range(depth):
        w = {'bd_pool': _block_diag(w_pool_grp[l]).astype(BF16),
             'pool_scale': row(pool_scale[l]), 'hg_norm_g': row(hg_norm_g[l]),
             'ln1_g': row(ln1_g[l]), 'ln1_b': row(ln1_b[l]),
             'conv_w': conv_w[l], 'conv_b': row(conv_b[l]), 'ln2_g': row(ln2_g[l]), 'ln2_b': row(ln2_b[l])}
        mkt, mvt, (w['w_in'], w['w_out'], w['w_br_pool'], w['w_br_hg'], w['w_br_mem']) = _memkv(
            mem2d, w_mem_k[l], w_mem_v[l], n_mem,
            [(w_in[l], half_cols), (w_out[l], jnp.full((D_MODEL,), 0.5, F32)), (w_br_pool[l], None),
             (w_br_hg[l], None), (w_br_mem[l], None)])
        (h_mid, pool_p, hg_p), (w['w_gate'], w['w_up'], w['w_down']) = _mixer_prompt(
            hp, mkt, mvt, lb_logits, w, l, 0, alpha, (w_gate[l], w_up[l], w_down[l]))
        hp, conv_p = _ffn_prompt(h_mid, w, alpha)
        outs[0].append(pool_p[:, CARRY_ROWS - POOL_BUF:])
        outs[1].append(hg_p)
        outs[2].append(conv_p[:, CONV_CARRY - (CONV_W - 1):])
        outs[3].append(_token_major(mkt))
        outs[4].append(_token_major(mvt))
        hs_mid, pool_s, hg_s = _mixer_sample(
            hs, _feature_major(cache_mem_k[l]), _feature_major(cache_mem_v[l]),
            time_major(state_pool[l]), state_hgrn[l], lb_logits, w, l, past_len, alpha)
        hs, conv_s = _ffn_sample(hs_mid.reshape(seq_s * n_sample, D_MODEL), state_ffn_conv[l], w, alpha)
        outs[5].append(time_major(pool_s))
        outs[6].append(hg_s)
        outs[7].append(conv_s)
    return (hp, hs) + tuple(jnp.stack(o) for o in outs)
```

```python
import functools

import jax
import jax.numpy as jnp
from jax import lax
from jax.experimental import pallas as pl
from jax.experimental.pallas import tpu as pltpu

F32 = jnp.float32
BF16 = jnp.bfloat16

D_MODEL = 1024
POOL_WIDTH = 256
POOL_WINDOWS = (2, 4, 8, 16)
POOL_GDIM = 64
POOL_BUF = 15
HG_HEADS = 4
HG_KDIM = 128
HG_WIDTH = 512
HG_VDIM = 128
HG_VWIDTH = 512
MEM_HEADS = 4
MEM_HDIM = 64
MEM_WIDTH = 256
N_BRANCH = 3
D_FF = 2816
CONV_W = 3
LN_EPS = 1e-5
RMS_EPS = 1e-6

C_UA = (0, 256)
C_QB = (256, 768)
C_FB = (768, 1280)
C_IB = (1280, 1792)
C_GB = (1792, 2304)
C_QC = (2304, 2560)
C_GATE = (2560, 5632)

LANES = 128
SUBLANES = 8
BF16_TILE_ROWS = 16
CARRY_ROWS = 16
CONV_CARRY = 8
PROMPT_T = 512
FFN_T = 1024
NORM_ROWS = 256
HG_BLOCK = 256
HG_CHUNK = 64
SAMPLE_BB = 16
V7X_VMEM_BYTES = 64 * 1024 * 1024
VMEM_LIMIT = V7X_VMEM_BYTES - 8 * 1024 * 1024
PAST_LEN = 16384


def _mm(a, b):
    return jnp.dot(a.astype(BF16), b.astype(BF16), preferred_element_type=F32)


def _mm_nt(a, b):
    return lax.dot_general(a.astype(BF16), b.astype(BF16), (((1,), (1,)), ((), ())), preferred_element_type=F32)


def _mm_tn(a, b):
    return lax.dot_general(a.astype(BF16), b.astype(BF16), (((0,), (0,)), ((), ())), preferred_element_type=F32)


def _split3(x):
    h1 = x.astype(BF16)
    r1 = x - h1.astype(F32)
    h2 = r1.astype(BF16)
    h3 = (r1 - h2.astype(F32)).astype(BF16)
    return h1, h2, h3


def _select_mm(sel, x):
    h1, h2, h3 = _split3(x)
    d = lambda p: jnp.dot(sel, p, preferred_element_type=F32)
    return d(h1) + d(h2) + d(h3)


def _silu_from_half(p):
    return p + p * jnp.tanh(p)


def _gated(p, m):
    return m + jnp.tanh(p) * m


def _layer_norm(x, g, b):
    mu = jnp.mean(x, axis=-1, keepdims=True)
    xc = x - mu
    var = jnp.mean(xc * xc, axis=-1, keepdims=True)
    return xc * lax.rsqrt(var + LN_EPS) * g + b


def _gelu(x):
    return 0.5 * x * (1.0 + lax.erf(x * (2.0 ** -0.5)))


def _lower_bound(lb_logits, layer):
    m = jnp.max(lb_logits, axis=0, keepdims=True)
    e = jnp.exp(lb_logits - m)
    sm = e / jnp.sum(e, axis=0, keepdims=True)
    return jnp.sum(sm[:layer + 1], axis=0, keepdims=True)


def _pool_diff(full, u_a, pos, tail):
    s2 = full + pltpu.roll(full, 1, 0)
    s4 = s2 + pltpu.roll(s2, 2, 0)
    s8 = s4 + pltpu.roll(s4, 4, 0)
    s16 = s8 + pltpu.roll(s8, 8, 0)
    grp = lax.broadcasted_iota(jnp.int32, u_a.shape, 1) // POOL_GDIM
    wsum = jnp.where(grp == 0, tail(s2), jnp.where(grp == 1, tail(s4), jnp.where(grp == 2, tail(s8), tail(s16))))
    wlen = jnp.where(grp == 0, 2, jnp.where(grp == 1, 4, jnp.where(grp == 2, 8, 16)))
    count = jnp.minimum(pos + 1, wlen).astype(F32)
    return wsum / count - u_a


def _forget_gates(fb_half, lb):
    th = jnp.tanh(fb_half)
    f = lb + (1.0 - lb) * (0.5 + 0.5 * th)
    k = (1.0 - lb) * (0.5 - 0.5 * th)
    return jnp.log(f), k


def _chunk_tri(n, chunk):
    r = lax.broadcasted_iota(jnp.int32, (n, n), 0)
    c = lax.broadcasted_iota(jnp.int32, (n, n), 1)
    return (r // chunk == c // chunk) & (c <= r)


def _as_bf16(mask):
    return jnp.where(mask, 1.0, 0.0).astype(BF16)


def _rms_head(o, g):
    return o * lax.rsqrt(jnp.mean(o * o, axis=-1, keepdims=True) + RMS_EPS) * g


def _project_norm(store, resid, lhs, w, g, b, alpha, group_rows=NORM_ROWS):
    n = resid.shape[0]
    step = min(n, group_rows)
    for i in range(0, n, step):
        store(i, _layer_norm(alpha * resid[i:i + step] + _mm(lhs[i:i + step], w), g, b))


def _rows_of(ref2d):
    def store(first_row, val):
        ref2d[first_row:first_row + val.shape[0], :] = val
    return store


def _merge_out(x, y_a, y_b, y_c, gate_pre, wbp, wbh, wbm, wout, g, b, alpha):
    merged2 = (_gated(gate_pre[:, 0:D_MODEL], _mm(y_a, wbp))
               + _gated(gate_pre[:, D_MODEL:2 * D_MODEL], _mm(y_b, wbh))
               + _gated(gate_pre[:, 2 * D_MODEL:3 * D_MODEL], _mm(y_c, wbm)))
    return _layer_norm(alpha * x + _mm(merged2, wout), g, b)


def _cast_block_count(rows, n_steps):
    return max(n for n in range(1, n_steps + 1) if rows % n == 0 and (rows // n) % BF16_TILE_ROWS == 0)


def _ride_along_casts(step, srcs, scales, dsts, block_counts):
    for src, scale, dst, n_blocks in zip(srcs, scales, dsts, block_counts):
        @pl.when(step < n_blocks)
        def _(src=src, scale=scale, dst=dst):
            val = src[...] if scale is None else src[...] * scale[...]
            dst[...] = val.astype(BF16)


def _memkv_kernel(mem_ref, wk_ref, wv_ref, *rest, cast_blocks, scaled):
    n_cast = len(cast_blocks)
    cast_in, scale_refs = rest[:n_cast], list(rest[n_cast:n_cast + sum(scaled)])
    kt_ref, vt_ref = rest[n_cast + sum(scaled):n_cast + sum(scaled) + 2]
    cast_out = rest[n_cast + sum(scaled) + 2:]
    mt = mem_ref[...].T
    kt_ref[0] = _mm_tn(wk_ref[...], mt)
    vt_ref[0] = _mm_tn(wv_ref[...], mt)
    scales = [scale_refs.pop(0) if s else None for s in scaled]
    _ride_along_casts(pl.program_id(0), cast_in, scales, cast_out, cast_blocks)


def _memkv(mem2d, wk, wv, n_mem, to_cast):
    n_seq = mem2d.shape[0] // n_mem
    const = lambda i: (0, 0)
    per_seq = lambda i: (i, 0, 0)
    arrays = [a for a, _ in to_cast]
    scale_rows = [s.reshape(1, -1) for _, s in to_cast if s is not None]
    cast_blocks = tuple(_cast_block_count(a.shape[0], n_seq) for a in arrays)
    cast_specs = [pl.BlockSpec((a.shape[0] // n, a.shape[1]), lambda i, n=n: (jnp.minimum(i, n - 1), 0))
                  for a, n in zip(arrays, cast_blocks)]
    outs = pl.pallas_call(
        functools.partial(_memkv_kernel, cast_blocks=cast_blocks, scaled=tuple(s is not None for _, s in to_cast)),
        grid=(n_seq,),
        in_specs=[pl.BlockSpec((n_mem, D_MODEL), lambda i: (i, 0)),
                  pl.BlockSpec((D_MODEL, MEM_WIDTH), const),
                  pl.BlockSpec((D_MODEL, MEM_WIDTH), const)]
                 + cast_specs + [pl.BlockSpec(s.shape, const) for s in scale_rows],
        out_specs=[pl.BlockSpec((1, MEM_WIDTH, n_mem), per_seq),
                   pl.BlockSpec((1, MEM_WIDTH, n_mem), per_seq)] + cast_specs,
        out_shape=[jax.ShapeDtypeStruct((n_seq, MEM_WIDTH, n_mem), F32)] * 2
                  + [jax.ShapeDtypeStruct(a.shape, BF16) for a in arrays],
        name="memkv",
    )(mem2d, wk, wv, *arrays, *scale_rows)
    return outs[0], outs[1], outs[2:]


def _feature_major(mem):
    return jnp.transpose(mem, (0, 2, 3, 1)).reshape(mem.shape[0], MEM_WIDTH, mem.shape[1])


def _token_major(mem_t):
    bsz, _, n_mem = mem_t.shape
    return jnp.transpose(mem_t.reshape(bsz, MEM_HEADS, MEM_HDIM, n_mem), (0, 3, 1, 2))


def _mixer_prompt_kernel(x_ref, mk_ref, mv_ref, lb_ref, win_ref, bd_ref, pscale_ref, hgg_ref,
                         wbp_ref, wbh_ref, wbm_ref, wout_ref, g_ref, b_ref, *rest, layer, pos0, alpha, cast_blocks):
    n_cast = len(cast_blocks)
    cast_in, (h_ref, newpool_ref, newhg_ref) = rest[:n_cast], rest[n_cast:n_cast + 3]
    cast_out, (st_scr, pool_scr) = rest[n_cast + 3:2 * n_cast + 3], rest[2 * n_cast + 3:]
    T, C = PROMPT_T, HG_CHUNK
    j = pl.program_id(1)

    _ride_along_casts(pl.program_id(0) * pl.num_programs(1) + j, cast_in, [None] * n_cast, cast_out, cast_blocks)

    @pl.when(j == 0)
    def _():
        pool_scr[...] = jnp.zeros_like(pool_scr)
        st_scr[...] = jnp.zeros_like(st_scr)

    x = x_ref[0]
    xb = x.astype(BF16)
    proj = lambda c: jnp.dot(xb, win_ref[:, c[0]:c[1]], preferred_element_type=F32)

    lb = _lower_bound(lb_ref[...], layer)
    logf, k = _forget_gates(proj(C_FB), lb)
    qb = proj(C_QB)
    v = proj(C_IB)
    same_chunk_causal = _chunk_tri(HG_BLOCK, C)
    tri = _as_bf16(same_chunk_causal)
    blocks = [slice(i * HG_BLOCK, (i + 1) * HG_BLOCK) for i in range(T // HG_BLOCK)]
    bcum = jnp.concatenate([_select_mm(tri, logf[bs]) for bs in blocks], axis=0)
    u_a = proj(C_UA)
    qc = proj(C_QC)
    gb = proj(C_GB)
    gate_a = proj((C_GATE[0], C_GATE[0] + D_MODEL))

    full = jnp.concatenate([pool_scr[...], u_a], axis=0)
    pos = pos0 + j * T + lax.broadcasted_iota(jnp.int32, (T, POOL_WIDTH), 0)
    diff_a = _pool_diff(full, u_a, pos, lambda z: z[CARRY_ROWS:])
    pool_scr[...] = full[T:]

    mk = mk_ref[0].astype(BF16)
    mv = mv_ref[0].astype(BF16)
    head_of_lane = lax.broadcasted_iota(jnp.int32, (T, MEM_WIDTH), 1) // MEM_HDIM
    att = [_mm(jnp.where(head_of_lane == h, qc, 0.0), mk) * (MEM_HDIM ** -0.5) for h in range(MEM_HEADS)]
    gate_b = proj((C_GATE[0] + D_MODEL, C_GATE[0] + 2 * D_MODEL))

    q = _silu_from_half(qb)
    n_chunks = T // C
    chunk_rows = [slice(c * C, (c + 1) * C) for c in range(n_chunks)]
    chunk_row = lambda c, row: bcum[c * C + row:c * C + row + 1]
    per_chunk = lambda fn: jnp.concatenate([jnp.broadcast_to(fn(c), (C, HG_WIDTH)) for c in range(n_chunks)], axis=0)
    bm = per_chunk(lambda c: chunk_row(c, C // 2))
    qin = q * jnp.exp(bcum)
    qd = qin * per_chunk(lambda c: jnp.exp(-chunk_row(c, C // 2)))
    kd = k * jnp.exp(bm - bcum)
    kl = kd * per_chunk(lambda c: jnp.exp(chunk_row(c, C - 1) - chunk_row(c, C // 2)))
    dec = [jnp.exp(chunk_row(c, C - 1)) for c in range(n_chunks)]
    hgg = hgg_ref[...]
    head_lanes = [slice(h * HG_KDIM, (h + 1) * HG_KDIM) for h in range(HG_HEADS)]
    sc = [[jnp.where(same_chunk_causal, _mm_nt(qd[bs, sl], kd[bs, sl]), 0.0) for bs in blocks] for sl in head_lanes]
    grow = [[_mm_tn(v[rs, sl], kl[rs, sl]) for rs in chunk_rows] for sl in head_lanes]
    y_a = _mm(diff_a, bd_ref[...]) * pscale_ref[...]
    gate_c = proj((C_GATE[0] + 2 * D_MODEL, C_GATE[1]))

    y_c = jnp.zeros((T, MEM_WIDTH), F32)
    inv_sum = jnp.zeros((T, MEM_WIDTH), F32)
    for h in range(MEM_HEADS):
        e = jnp.exp(att[h] - jnp.max(att[h], axis=-1, keepdims=True))
        y_c = y_c + jnp.where(head_of_lane == h, _mm_nt(e, mv), 0.0)
        inv_sum = inv_sum + jnp.where(head_of_lane == h, 1.0 / jnp.sum(e, axis=-1, keepdims=True), 0.0)
    y_c = y_c * inv_sum
    m_a = _mm(y_a, wbp_ref[...])

    heads = []
    for h, sl in enumerate(head_lanes):
        o_intra = jnp.concatenate([_mm(sc[h][i], v[bs, sl]) for i, bs in enumerate(blocks)], axis=0)
        st = st_scr[h]
        o_inter = []
        for c, rs in enumerate(chunk_rows):
            o_inter.append(_mm_nt(qin[rs, sl], st))
            st = st * dec[c][:, sl] + grow[h][c]
        st_scr[h] = st
        heads.append(_rms_head(o_intra + jnp.concatenate(o_inter, axis=0), hgg[:, sl]))
    m_c = _mm(y_c, wbm_ref[...])
    y_b = jnp.concatenate(heads, axis=1) * _silu_from_half(gb)
    merged2 = _gated(gate_a, m_a) + _gated(gate_c, m_c) + _gated(gate_b, _mm(y_b, wbh_ref[...]))
    _project_norm(_rows_of(h_ref.at[0]), x, merged2, wout_ref[...], g_ref[...], b_ref[...], alpha)

    @pl.when(j == pl.num_programs(1) - 1)
    def _():
        newpool_ref[0] = pool_scr[...]
        for h in range(HG_HEADS):
            newhg_ref[0, h] = st_scr[h].T


def _const_spec(shape):
    n = len(shape)
    return pl.BlockSpec(shape, lambda *_: (0,) * n, pipeline_mode=pl.Buffered(1))


def _mixer_prompt(x, mk, mv, lb_logits, w, layer, pos0, alpha, to_cast):
    bsz, seq, _ = x.shape
    T = PROMPT_T
    n_j = seq // T
    per_b3 = lambda b, j: (b, 0, 0)
    cast_blocks = tuple(_cast_block_count(a.shape[0], bsz * n_j) for a in to_cast)
    cast_specs = [pl.BlockSpec((a.shape[0] // n, a.shape[1]), lambda b, j, n=n: (jnp.minimum(b * n_j + j, n - 1), 0))
                  for a, n in zip(to_cast, cast_blocks)]
    kern = functools.partial(_mixer_prompt_kernel, layer=layer, pos0=pos0, alpha=alpha, cast_blocks=cast_blocks)
    weights = [lb_logits, w['w_in'], w['bd_pool'], w['pool_scale'], w['hg_norm_g'], w['w_br_pool'], w['w_br_hg'],
               w['w_br_mem'], w['w_out'], w['ln1_g'], w['ln1_b']]
    outs = pl.pallas_call(
        kern,
        grid=(bsz, n_j),
        in_specs=[pl.BlockSpec((1, T, D_MODEL), lambda b, j: (b, j, 0)),
                  pl.BlockSpec((1,) + mk.shape[1:], per_b3),
                  pl.BlockSpec((1,) + mv.shape[1:], per_b3)]
                 + [_const_spec(a.shape) for a in weights] + cast_specs,
        out_specs=[pl.BlockSpec((1, T, D_MODEL), lambda b, j: (b, j, 0)),
                   pl.BlockSpec((1, CARRY_ROWS, POOL_WIDTH), per_b3),
                   pl.BlockSpec((1, HG_HEADS, HG_KDIM, HG_VDIM), lambda b, j: (b, 0, 0, 0))] + cast_specs,
        out_shape=[jax.ShapeDtypeStruct((bsz, seq, D_MODEL), F32),
                   jax.ShapeDtypeStruct((bsz, CARRY_ROWS, POOL_WIDTH), F32),
                   jax.ShapeDtypeStruct((bsz, HG_HEADS, HG_KDIM, HG_VDIM), F32)]
                  + [jax.ShapeDtypeStruct(a.shape, BF16) for a in to_cast],
        scratch_shapes=[pltpu.VMEM((HG_HEADS, HG_VDIM, HG_KDIM), F32),
                        pltpu.VMEM((CARRY_ROWS, POOL_WIDTH), F32)],
        compiler_params=pltpu.CompilerParams(dimension_semantics=("arbitrary", "arbitrary"),
                                             vmem_limit_bytes=VMEM_LIMIT),
        name="mixer_prompt",
    )(x, mk, mv, *weights, *to_cast)
    return outs[:3], outs[3:]


def _conv_gate(full, tail, u, cw, cb):
    c = cb + pltpu.roll(full, 2, 0) * cw[0:1] + pltpu.roll(full, 1, 0) * cw[1:2] + full * cw[2:3]
    return _gelu(tail(c)) * u


def _ffn_prompt_kernel(h_ref, wg_ref, wu_ref, cw_ref, cb_ref, wd_ref, g_ref, b_ref,
                       y_ref, newconv_ref, carry_scr, *, alpha):
    T = FFN_T
    j = pl.program_id(1)

    @pl.when(j == 0)
    def _():
        carry_scr[...] = jnp.zeros_like(carry_scr)

    h = h_ref[0]
    hb = h.astype(BF16)
    a = jnp.dot(hb, wg_ref[...], preferred_element_type=F32)
    u = jnp.dot(hb, wu_ref[...], preferred_element_type=F32)
    full = jnp.concatenate([carry_scr[...], a], axis=0)
    gated = _conv_gate(full, lambda z: z[CONV_CARRY:], u, cw_ref[...], cb_ref[...])
    _project_norm(_rows_of(y_ref.at[0]), h, gated, wd_ref[...], g_ref[...], b_ref[...], alpha)
    carry_scr[...] = a[T - CONV_CARRY:]

    @pl.when(j == pl.num_programs(1) - 1)
    def _():
        newconv_ref[0] = carry_scr[...]


def _ffn_prompt(h, w, alpha):
    bsz, seq, _ = h.shape
    T = FFN_T
    weights = [w['w_gate'], w['w_up'], w['conv_w'], w['conv_b'], w['w_down'], w['ln2_g'], w['ln2_b']]
    return pl.pallas_call(
        functools.partial(_ffn_prompt_kernel, alpha=alpha),
        grid=(bsz, seq // T),
        in_specs=[pl.BlockSpec((1, T, D_MODEL), lambda b, j: (b, j, 0))]
                 + [_const_spec(a.shape) for a in weights],
        out_specs=[pl.BlockSpec((1, T, D_MODEL), lambda b, j: (b, j, 0)),
                   pl.BlockSpec((1, CONV_CARRY, D_FF), lambda b, j: (b, 0, 0))],
        out_shape=[jax.ShapeDtypeStruct((bsz, seq, D_MODEL), F32),
                   jax.ShapeDtypeStruct((bsz, CONV_CARRY, D_FF), F32)],
        scratch_shapes=[pltpu.VMEM((CONV_CARRY, D_FF), F32)],
        compiler_params=pltpu.CompilerParams(dimension_semantics=("arbitrary", "arbitrary"),
                                             vmem_limit_bytes=VMEM_LIMIT),
        name="ffn_prompt",
    )(h, *weights)


def _bmm(spec, a, b):
    return jnp.einsum(spec, a.astype(BF16), b.astype(BF16), preferred_element_type=F32)


def _mixer_sample_kernel(x_ref, mk_ref, mv_ref, pool_ref, hg_ref, lb_ref, win_ref, bd_ref, pscale_ref, hgg_ref,
                         wbp_ref, wbh_ref, wbm_ref, wout_ref, g_ref, b_ref,
                         h_ref, newpool_ref, newhg_ref, seq_scr, out_scr, *, layer, pos0, alpha, seq):
    Bb, S = SAMPLE_BB, SUBLANES
    R = Bb * S
    M = seq * Bb
    slab = lambda z, t: z[t * Bb:(t + 1) * Bb]
    x = jnp.concatenate([x_ref[:, t, :] for t in range(seq)], axis=0)
    xb = x.astype(BF16)
    proj = lambda c: jnp.dot(xb, win_ref[:, c[0]:c[1]], preferred_element_type=F32)

    u_a = proj(C_UA)
    rows = [pool_ref[i] for i in range(POOL_BUF)] + [slab(u_a, t) for t in range(seq)]
    n_rows = len(rows)
    sums = {1: dict(enumerate(rows))}
    for w in POOL_WINDOWS:
        half = sums[w // 2]
        sums[w] = {i: half[i] + half[i - w // 2] for i in range(n_rows) if i - w // 2 in half and i in half}
    grp = lax.broadcasted_iota(jnp.int32, (Bb, POOL_WIDTH), 1) // POOL_GDIM
    diffs = []
    for t in range(seq):
        i = POOL_BUF + t
        pooled = [sums[w][i] / float(min(pos0 + t + 1, w)) for w in POOL_WINDOWS]
        mean = jnp.where(grp == 0, pooled[0], jnp.where(grp == 1, pooled[1], jnp.where(grp == 2, pooled[2], pooled[3])))
        diffs.append(mean - rows[i])
    y_a = _mm(jnp.concatenate(diffs, axis=0), bd_ref[...]) * pscale_ref[...]
    for i in range(POOL_BUF):
        newpool_ref[i] = rows[n_rows - POOL_BUF + i]

    lb = _lower_bound(lb_ref[...], layer)
    logf_t, k_t = _forget_gates(proj(C_FB), lb)
    per_seq_in = jnp.concatenate([logf_t, k_t, _silu_from_half(proj(C_QB)), proj(C_IB), proj(C_QC)], axis=1)
    seq_scr[...] = jnp.zeros_like(seq_scr)
    for c in range(seq_scr.shape[0]):
        for t in range(seq):
            seq_scr[c, pl.ds(t, Bb, stride=S), :] = slab(per_seq_in, t)[:, c * LANES:(c + 1) * LANES]
    cols = lambda lo, hi: jnp.concatenate([seq_scr[c] for c in range(lo // LANES, hi // LANES)], axis=1)
    logf = cols(0, HG_WIDTH)
    k = cols(HG_WIDTH, 2 * HG_WIDTH)
    q = cols(2 * HG_WIDTH, 3 * HG_WIDTH)
    v = cols(3 * HG_WIDTH, 3 * HG_WIDTH + HG_VWIDTH)
    qc = cols(3 * HG_WIDTH + HG_VWIDTH, 3 * HG_WIDTH + HG_VWIDTH + MEM_WIDTH)
    to3 = lambda z: z.reshape(Bb, S, z.shape[-1])
    b3 = to3(_select_mm(_as_bf16(_chunk_tri(R, S)), logf))
    bm = b3[:, S // 2:S // 2 + 1, :]
    bl = b3[:, S - 1:S, :]
    q3, k3, v3 = to3(q), to3(k), to3(v)
    qin = q3 * jnp.exp(b3)
    qd = q3 * jnp.exp(b3 - bm)
    kd = k3 * jnp.exp(bm - b3)
    kl = k3 * jnp.exp(bl - b3)
    rr = lax.broadcasted_iota(jnp.int32, (Bb, S, S), 1)
    cc = lax.broadcasted_iota(jnp.int32, (Bb, S, S), 2)
    causal = cc <= rr
    p1 = bl.astype(BF16).astype(F32)
    p2 = (bl - p1).astype(BF16).astype(F32)
    p3 = (bl - p1) - p2
    r3 = lax.broadcasted_iota(jnp.int32, (Bb, S, HG_WIDTH), 1)
    pieces = jnp.where(r3 == 0, p1, jnp.where(r3 == 1, p2, jnp.where(r3 == 2, p3, 0.0)))
    ones = jnp.ones((Bb, S, HG_VDIM), BF16)
    hgg = hgg_ref[...]
    heads = []
    for h in range(HG_HEADS):
        sl = slice(h * HG_KDIM, (h + 1) * HG_KDIM)
        sc = jnp.where(causal, _bmm('bqd,bkd->bqk', qd[:, :, sl], kd[:, :, sl]), 0.0)
        s0 = hg_ref[:, h]
        o = _bmm('bqk,bke->bqe', sc, v3[:, :, sl]) + _bmm('bqd,bde->bqe', qin[:, :, sl], s0)
        logdec = _bmm('bkd,bke->bde', pieces[:, :, sl], ones)
        newhg_ref[:, h] = jnp.exp(logdec) * s0 + _bmm('bkd,bke->bde', kl[:, :, sl], v3[:, :, sl])
        heads.append(_rms_head(o, hgg[:, sl]).reshape(R, HG_VDIM))

    qc3 = to3(qc)
    head_of_lane = lax.broadcasted_iota(jnp.int32, (Bb, S, MEM_WIDTH), 2) // MEM_HDIM
    q4 = jnp.concatenate([jnp.where(head_of_lane == h, qc3, 0.0) for h in range(MEM_HEADS)], axis=1)
    s = _bmm('bqd,bdm->bqm', q4, mk_ref[...]) * (MEM_HDIM ** -0.5)
    e = jnp.exp(s - jnp.max(s, axis=-1, keepdims=True))
    p = e / jnp.sum(e, axis=-1, keepdims=True)
    o4 = _bmm('bqm,bdm->bqd', p, mv_ref[...])
    y_c = jnp.zeros((Bb, S, MEM_WIDTH), F32)
    for h in range(MEM_HEADS):
        y_c = y_c + jnp.where(head_of_lane == h, o4[:, h * S:(h + 1) * S, :], 0.0)

    staged = jnp.concatenate(heads + [y_c.reshape(R, MEM_WIDTH)], axis=1)
    for c in range(out_scr.shape[0]):
        out_scr[c] = staged[:, c * LANES:(c + 1) * LANES]
    per_seq_out = jnp.concatenate(
        [jnp.concatenate([out_scr[c, pl.ds(t, Bb, stride=S), :] for c in range(out_scr.shape[0])], axis=1)
         for t in range(seq)], axis=0)
    y_b = per_seq_out[:, :HG_VWIDTH] * _silu_from_half(proj(C_GB))
    y_c = per_seq_out[:, HG_VWIDTH:]
    h = _merge_out(x, y_a, y_b, y_c, proj(C_GATE), wbp_ref[...], wbh_ref[...], wbm_ref[...], wout_ref[...],
                   g_ref[...], b_ref[...], alpha)
    h_ref[...] = h.reshape(seq, Bb, D_MODEL)


def _mixer_sample(x, mk, mv, pool_t, hg, lb_logits, w, layer, pos0, alpha):
    Bb, S = SAMPLE_BB, SUBLANES
    bsz, seq, _ = x.shape
    n_mem = mk.shape[2]
    b3 = lambda i: (i, 0, 0)
    tb3 = lambda i: (0, i, 0)
    weights = [lb_logits, w['w_in'], w['bd_pool'], w['pool_scale'], w['hg_norm_g'], w['w_br_pool'], w['w_br_hg'],
               w['w_br_mem'], w['w_out'], w['ln1_g'], w['ln1_b']]
    per_seq_in = 3 * HG_WIDTH + HG_VWIDTH + MEM_WIDTH
    per_seq_out = HG_VWIDTH + MEM_WIDTH
    return pl.pallas_call(
        functools.partial(_mixer_sample_kernel, layer=layer, pos0=pos0, alpha=alpha, seq=seq),
        grid=(bsz // Bb,),
        in_specs=[pl.BlockSpec((Bb, seq, D_MODEL), b3),
                  pl.BlockSpec((Bb, MEM_WIDTH, n_mem), b3),
                  pl.BlockSpec((Bb, MEM_WIDTH, n_mem), b3),
                  pl.BlockSpec((POOL_BUF, Bb, POOL_WIDTH), tb3),
                  pl.BlockSpec((Bb, HG_HEADS, HG_KDIM, HG_VDIM), lambda i: (i, 0, 0, 0))]
                 + [_const_spec(a.shape) for a in weights],
        out_specs=[pl.BlockSpec((seq, Bb, D_MODEL), tb3),
                   pl.BlockSpec((POOL_BUF, Bb, POOL_WIDTH), tb3),
                   pl.BlockSpec((Bb, HG_HEADS, HG_KDIM, HG_VDIM), lambda i: (i, 0, 0, 0))],
        out_shape=[jax.ShapeDtypeStruct((seq, bsz, D_MODEL), F32),
                   jax.ShapeDtypeStruct((POOL_BUF, bsz, POOL_WIDTH), F32),
                   jax.ShapeDtypeStruct((bsz, HG_HEADS, HG_KDIM, HG_VDIM), F32)],
        scratch_shapes=[pltpu.VMEM((per_seq_in // LANES, Bb * S, LANES), F32),
                        pltpu.VMEM((per_seq_out // LANES, Bb * S, LANES), F32)],
        compiler_params=pltpu.CompilerParams(dimension_semantics=("arbitrary",), vmem_limit_bytes=VMEM_LIMIT),
        name="mixer_sample",
    )(x, mk, mv, pool_t, hg, *weights)


def _ffn_sample_kernel(h_ref, conv_ref, wg_ref, wu_ref, cw_ref, cb_ref, wd_ref, g_ref, b_ref,
                       y_ref, newconv_ref, *, alpha, seq):
    bsz = conv_ref.shape[0]
    h = h_ref[...]
    hb = h.astype(BF16)
    a = jnp.dot(hb, wg_ref[...], preferred_element_type=F32)
    u = jnp.dot(hb, wu_ref[...], preferred_element_type=F32)
    rows = [conv_ref[:, i, :] for i in range(CONV_W - 1)] + [a[t * bsz:(t + 1) * bsz] for t in range(seq)]
    cw = cw_ref[...]
    c = jnp.concatenate([cb_ref[...] + sum(rows[t + i] * cw[i:i + 1] for i in range(CONV_W)) for t in range(seq)],
                        axis=0)

    def store(first_row, val):
        for t in range(val.shape[0] // bsz):
            y_ref[:, first_row // bsz + t, :] = val[t * bsz:(t + 1) * bsz]

    _project_norm(store, h, _gelu(c) * u, wd_ref[...], g_ref[...], b_ref[...], alpha,
                  group_rows=bsz * max(1, NORM_ROWS // bsz))
    for i in range(CONV_W - 1):
        newconv_ref[:, i, :] = rows[seq + i]


def _ffn_sample(h_t2d, conv, w, alpha):
    n_rows = h_t2d.shape[0]
    bsz = conv.shape[0]
    weights = [w['w_gate'], w['w_up'], w['conv_w'], w['conv_b'], w['w_down'], w['ln2_g'], w['ln2_b']]
    full = lambda shape: pl.BlockSpec(shape, lambda i: (0,) * len(shape))
    y_shape = (bsz, n_rows // bsz, D_MODEL)
    return pl.pallas_call(
        functools.partial(_ffn_sample_kernel, alpha=alpha, seq=n_rows // bsz),
        grid=(1,),
        in_specs=[full(h_t2d.shape), full(conv.shape)] + [_const_spec(a.shape) for a in weights],
        out_specs=[full(y_shape), full(conv.shape)],
        out_shape=[jax.ShapeDtypeStruct(y_shape, F32), jax.ShapeDtypeStruct(conv.shape, F32)],
        compiler_params=pltpu.CompilerParams(dimension_semantics=("arbitrary",), vmem_limit_bytes=VMEM_LIMIT),
        name="ffn_sample",
    )(h_t2d, conv, *weights)


def _block_diag(w_grp):
    groups, gdim, _ = w_grp.shape
    out = jnp.zeros((groups * gdim, groups * gdim), w_grp.dtype)
    for g in range(groups):
        out = lax.dynamic_update_slice(out, w_grp[g], (g * gdim, g * gdim))
    return out


def kernel(x_prompt, x_sample, state_pool, state_hgrn, state_ffn_conv, cache_mem_k, cache_mem_v, mem_prompt, lb_logits, w_in, w_pool_grp, pool_scale, hg_norm_g, w_mem_k, w_mem_v, w_br_pool, w_br_hg, w_br_mem, w_out, ln1_g, ln1_b, w_gate, w_up, conv_w, conv_b, w_down, ln2_g, ln2_b):
    depth = w_in.shape[0]
    alpha = (2 * depth) ** 0.25
    n_prompt, seq_p, _ = x_prompt.shape
    n_sample, seq_s, _ = x_sample.shape
    n_mem = mem_prompt.shape[1]
    assert seq_p % PROMPT_T == 0 and seq_p % FFN_T == 0 and seq_s <= SUBLANES and n_sample % SAMPLE_BB == 0

    hp = x_prompt
    time_major = lambda a: jnp.transpose(a, (1, 0, 2))
    hs = x_sample
    mem2d = mem_prompt.reshape(n_prompt * n_mem, D_MODEL)
    row = lambda a: a.reshape(1, -1)
    col = jnp.arange(w_in.shape[-1])
    in_range = lambda c: (col >= c[0]) & (col < c[1])
    half_cols = jnp.where(in_range(C_QB) | in_range(C_FB) | in_range(C_GB) | in_range(C_GATE), 0.5, 1.0).astype(F32)
    outs = [[] for _ in range(8)]
    for l in range(depth):
        w = {'bd_pool': _block_diag(w_pool_grp[l]).astype(BF16),
             'pool_scale': row(pool_scale[l]), 'hg_norm_g': row(hg_norm_g[l]),
             'ln1_g': row(ln1_g[l]), 'ln1_b': row(ln1_b[l]),
             'conv_w': conv_w[l], 'conv_b': row(conv_b[l]), 'ln2_g': row(ln2_g[l]), 'ln2_b': row(ln2_b[l])}
        mkt, mvt, (w['w_in'], w['w_out'], w['w_br_pool'], w['w_br_hg'], w['w_br_mem']) = _memkv(
            mem2d, w_mem_k[l], w_mem_v[l], n_mem,
            [(w_in[l], half_cols), (w_out[l], jnp.full((D_MODEL,), 0.5, F32)), (w_br_pool[l], None),
             (w_br_hg[l], None), (w_br_mem[l], None)])
        (h_mid, pool_p, hg_p), (w['w_gate'], w['w_up'], w['w_down']) = _mixer_prompt(
            hp, mkt, mvt, lb_logits, w, l, 0, alpha, (w_gate[l], w_up[l], w_down[l]))
        hp, conv_p = _ffn_prompt(h_mid, w, alpha)
        outs[0].append(pool_p[:, CARRY_ROWS - POOL_BUF:])
        outs[1].append(hg_p)
        outs[2].append(conv_p[:, CONV_CARRY - (CONV_W - 1):])
        outs[3].append(_token_major(mkt))
        outs[4].append(_token_major(mvt))
        hs_mid, pool_s, hg_s = _mixer_sample(
            hs, _feature_major(cache_mem_k[l]), _feature_major(cache_mem_v[l]),
            time_major(state_pool[l]), state_hgrn[l], lb_logits, w, l, PAST_LEN, alpha)
        hs, conv_s = _ffn_sample(hs_mid.reshape(seq_s * n_sample, D_MODEL), state_ffn_conv[l], w, alpha)
        outs[5].append(time_major(pool_s))
        outs[6].append(hg_s)
        outs[7].append(conv_s)
    return (hp, hs) + tuple(jnp.stack(o) for o in outs)
```

```python
import functools

import jax
import jax.numpy as jnp
from jax import lax
from jax.experimental import pallas as pl
from jax.experimental.pallas import tpu as pltpu

F32 = jnp.float32
BF16 = jnp.bfloat16

D_MODEL = 1024
POOL_WIDTH = 256
POOL_WINDOWS = (2, 4, 8, 16)
POOL_GDIM = 64
POOL_BUF = 15
HG_HEADS = 4
HG_KDIM = 128
HG_WIDTH = 512
HG_VDIM = 128
HG_VWIDTH = 512
MEM_HEADS = 4
MEM_HDIM = 64
MEM_WIDTH = 256
N_BRANCH = 3
D_FF = 2816
CONV_W = 3
LN_EPS = 1e-5
RMS_EPS = 1e-6

C_UA = (0, 256)
C_QB = (256, 768)
C_FB = (768, 1280)
C_IB = (1280, 1792)
C_GB = (1792, 2304)
C_QC = (2304, 2560)
C_GATE = (2560, 5632)

LANES = 128
SUBLANES = 8
BF16_TILE_ROWS = 16
CARRY_ROWS = 16
CONV_CARRY = 8
PROMPT_T = 512
FFN_T = 1024
NORM_ROWS = 256
HG_BLOCK = 256
HG_CHUNK = 64
SAMPLE_BB = 16
V7X_VMEM_BYTES = 64 * 1024 * 1024
VMEM_LIMIT = V7X_VMEM_BYTES - 8 * 1024 * 1024
PAST_LEN = 16384


def _mm(a, b):
    return jnp.dot(a.astype(BF16), b.astype(BF16), preferred_element_type=F32)


def _mm_nt(a, b):
    return lax.dot_general(a.astype(BF16), b.astype(BF16), (((1,), (1,)), ((), ())), preferred_element_type=F32)


def _mm_tn(a, b):
    return lax.dot_general(a.astype(BF16), b.astype(BF16), (((0,), (0,)), ((), ())), preferred_element_type=F32)


def _split3(x):
    h1 = x.astype(BF16)
    r1 = x - h1.astype(F32)
    h2 = r1.astype(BF16)
    h3 = (r1 - h2.astype(F32)).astype(BF16)
    return h1, h2, h3


def _select_mm(sel, x):
    h1, h2, h3 = _split3(x)
    d = lambda p: jnp.dot(sel, p, preferred_element_type=F32)
    return d(h1) + d(h2) + d(h3)


def _silu_from_half(p):
    return p + p * jnp.tanh(p)


def _gated(p, m):
    return m + jnp.tanh(p) * m


def _layer_norm(x, g, b):
    mu = jnp.mean(x, axis=-1, keepdims=True)
    xc = x - mu
    var = jnp.mean(xc * xc, axis=-1, keepdims=True)
    return xc * lax.rsqrt(var + LN_EPS) * g + b


def _gelu(x):
    return 0.5 * x * (1.0 + lax.erf(x * (2.0 ** -0.5)))


def _lower_bound(lb_logits, layer):
    m = jnp.max(lb_logits, axis=0, keepdims=True)
    e = jnp.exp(lb_logits - m)
    sm = e / jnp.sum(e, axis=0, keepdims=True)
    return jnp.sum(sm[:layer + 1], axis=0, keepdims=True)


def _pool_diff(full, u_a, pos, tail):
    s2 = full + pltpu.roll(full, 1, 0)
    s4 = s2 + pltpu.roll(s2, 2, 0)
    s8 = s4 + pltpu.roll(s4, 4, 0)
    s16 = s8 + pltpu.roll(s8, 8, 0)
    grp = lax.broadcasted_iota(jnp.int32, u_a.shape, 1) // POOL_GDIM
    wsum = jnp.where(grp == 0, tail(s2), jnp.where(grp == 1, tail(s4), jnp.where(grp == 2, tail(s8), tail(s16))))
    wlen = jnp.where(grp == 0, 2, jnp.where(grp == 1, 4, jnp.where(grp == 2, 8, 16)))
    count = jnp.minimum(pos + 1, wlen).astype(F32)
    return wsum / count - u_a


def _forget_gates(fb_half, lb):
    th = jnp.tanh(fb_half)
    f = lb + (1.0 - lb) * (0.5 + 0.5 * th)
    k = (1.0 - lb) * (0.5 - 0.5 * th)
    return jnp.log(f), k


def _chunk_tri(n, chunk):
    r = lax.broadcasted_iota(jnp.int32, (n, n), 0)
    c = lax.broadcasted_iota(jnp.int32, (n, n), 1)
    return (r // chunk == c // chunk) & (c <= r)


def _as_bf16(mask):
    return jnp.where(mask, 1.0, 0.0).astype(BF16)


def _rms_head(o, g):
    return o * lax.rsqrt(jnp.mean(o * o, axis=-1, keepdims=True) + RMS_EPS) * g


def _project_norm(store, resid, lhs, w, g, b, alpha, group_rows=NORM_ROWS):
    n = resid.shape[0]
    step = min(n, group_rows)
    for i in range(0, n, step):
        store(i, _layer_norm(alpha * resid[i:i + step] + _mm(lhs[i:i + step], w), g, b))


def _rows_of(ref2d):
    def store(first_row, val):
        ref2d[first_row:first_row + val.shape[0], :] = val
    return store


def _merge_out(x, y_a, y_b, y_c, gate_pre, wbp, wbh, wbm, wout, g, b, alpha):
    merged2 = (_gated(gate_pre[:, 0:D_MODEL], _mm(y_a, wbp))
               + _gated(gate_pre[:, D_MODEL:2 * D_MODEL], _mm(y_b, wbh))
               + _gated(gate_pre[:, 2 * D_MODEL:3 * D_MODEL], _mm(y_c, wbm)))
    return _layer_norm(alpha * x + _mm(merged2, wout), g, b)


def _cast_block_count(rows, n_steps):
    return max(n for n in range(1, n_steps + 1) if rows % n == 0 and (rows // n) % BF16_TILE_ROWS == 0)


def _ride_along_casts(step, srcs, scales, dsts, block_counts):
    for src, scale, dst, n_blocks in zip(srcs, scales, dsts, block_counts):
        @pl.when(step < n_blocks)
        def _(src=src, scale=scale, dst=dst):
            val = src[...] if scale is None else src[...] * scale[...]
            dst[...] = val.astype(BF16)


def _memkv_kernel(mem_ref, wk_ref, wv_ref, *rest, cast_blocks, scaled):
    n_cast = len(cast_blocks)
    cast_in, scale_refs = rest[:n_cast], list(rest[n_cast:n_cast + sum(scaled)])
    kt_ref, vt_ref = rest[n_cast + sum(scaled):n_cast + sum(scaled) + 2]
    cast_out = rest[n_cast + sum(scaled) + 2:]
    mt = mem_ref[...].T
    kt_ref[0] = _mm_tn(wk_ref[...], mt)
    vt_ref[0] = _mm_tn(wv_ref[...], mt)
    scales = [scale_refs.pop(0) if s else None for s in scaled]
    _ride_along_casts(pl.program_id(0), cast_in, scales, cast_out, cast_blocks)


def _memkv(mem2d, wk, wv, n_mem, to_cast):
    n_seq = mem2d.shape[0] // n_mem
    const = lambda i: (0, 0)
    per_seq = lambda i: (i, 0, 0)
    arrays = [a for a, _ in to_cast]
    scale_rows = [s.reshape(1, -1) for _, s in to_cast if s is not None]
    cast_blocks = tuple(_cast_block_count(a.shape[0], n_seq) for a in arrays)
    cast_specs = [pl.BlockSpec((a.shape[0] // n, a.shape[1]), lambda i, n=n: (jnp.minimum(i, n - 1), 0))
                  for a, n in zip(arrays, cast_blocks)]
    outs = pl.pallas_call(
        functools.partial(_memkv_kernel, cast_blocks=cast_blocks, scaled=tuple(s is not None for _, s in to_cast)),
        grid=(n_seq,),
        in_specs=[pl.BlockSpec((n_mem, D_MODEL), lambda i: (i, 0)),
                  pl.BlockSpec((D_MODEL, MEM_WIDTH), const),
                  pl.BlockSpec((D_MODEL, MEM_WIDTH), const)]
                 + cast_specs + [pl.BlockSpec(s.shape, const) for s in scale_rows],
        out_specs=[pl.BlockSpec((1, MEM_WIDTH, n_mem), per_seq),
                   pl.BlockSpec((1, MEM_WIDTH, n_mem), per_seq)] + cast_specs,
        out_shape=[jax.ShapeDtypeStruct((n_seq, MEM_WIDTH, n_mem), F32)] * 2
                  + [jax.ShapeDtypeStruct(a.shape, BF16) for a in arrays],
        name="memkv",
    )(mem2d, wk, wv, *arrays, *scale_rows)
    return outs[0], outs[1], outs[2:]


def _feature_major(mem):
    return jnp.transpose(mem, (0, 2, 3, 1)).reshape(mem.shape[0], MEM_WIDTH, mem.shape[1])


def _token_major(mem_t):
    bsz, _, n_mem = mem_t.shape
    return jnp.transpose(mem_t.reshape(bsz, MEM_HEADS, MEM_HDIM, n_mem), (0, 3, 1, 2))


def _mixer_prompt_kernel(x_ref, mk_ref, mv_ref, lb_ref, win_ref, bd_ref, pscale_ref, hgg_ref,
                         wbp_ref, wbh_ref, wbm_ref, wout_ref, g_ref, b_ref, *rest, layer, pos0, alpha, cast_blocks):
    n_cast = len(cast_blocks)
    cast_in, (h_ref, newpool_ref, newhg_ref) = rest[:n_cast], rest[n_cast:n_cast + 3]
    cast_out, (st_scr, pool_scr) = rest[n_cast + 3:2 * n_cast + 3], rest[2 * n_cast + 3:]
    T, C = PROMPT_T, HG_CHUNK
    j = pl.program_id(1)

    _ride_along_casts(pl.program_id(0) * pl.num_programs(1) + j, cast_in, [None] * n_cast, cast_out, cast_blocks)

    @pl.when(j == 0)
    def _():
        pool_scr[...] = jnp.zeros_like(pool_scr)
        st_scr[...] = jnp.zeros_like(st_scr)

    x = x_ref[0]
    xb = x.astype(BF16)
    proj = lambda c: jnp.dot(xb, win_ref[:, c[0]:c[1]], preferred_element_type=F32)

    lb = _lower_bound(lb_ref[...], layer)
    logf, k = _forget_gates(proj(C_FB), lb)
    qb = proj(C_QB)
    v = proj(C_IB)
    same_chunk_causal = _chunk_tri(HG_BLOCK, C)
    tri = _as_bf16(same_chunk_causal)
    blocks = [slice(i * HG_BLOCK, (i + 1) * HG_BLOCK) for i in range(T // HG_BLOCK)]
    bcum = jnp.concatenate([_select_mm(tri, logf[bs]) for bs in blocks], axis=0)
    u_a = proj(C_UA)
    qc = proj(C_QC)
    gb = proj(C_GB)
    gate_a = proj((C_GATE[0], C_GATE[0] + D_MODEL))

    full = jnp.concatenate([pool_scr[...], u_a], axis=0)
    pos = pos0 + j * T + lax.broadcasted_iota(jnp.int32, (T, POOL_WIDTH), 0)
    diff_a = _pool_diff(full, u_a, pos, lambda z: z[CARRY_ROWS:])
    pool_scr[...] = full[T:]

    mk = mk_ref[0].astype(BF16)
    mv = mv_ref[0].astype(BF16)
    head_of_lane = lax.broadcasted_iota(jnp.int32, (T, MEM_WIDTH), 1) // MEM_HDIM
    att = [_mm(jnp.where(head_of_lane == h, qc, 0.0), mk) * (MEM_HDIM ** -0.5) for h in range(MEM_HEADS)]
    gate_b = proj((C_GATE[0] + D_MODEL, C_GATE[0] + 2 * D_MODEL))

    q = _silu_from_half(qb)
    n_chunks = T // C
    chunk_rows = [slice(c * C, (c + 1) * C) for c in range(n_chunks)]
    chunk_row = lambda c, row: bcum[c * C + row:c * C + row + 1]
    per_chunk = lambda fn: jnp.concatenate([jnp.broadcast_to(fn(c), (C, HG_WIDTH)) for c in range(n_chunks)], axis=0)
    bm = per_chunk(lambda c: chunk_row(c, C // 2))
    qin = q * jnp.exp(bcum)
    qd = qin * per_chunk(lambda c: jnp.exp(-chunk_row(c, C // 2)))
    kd = k * jnp.exp(bm - bcum)
    kl = kd * per_chunk(lambda c: jnp.exp(chunk_row(c, C - 1) - chunk_row(c, C // 2)))
    dec = [jnp.exp(chunk_row(c, C - 1)) for c in range(n_chunks)]
    hgg = hgg_ref[...]
    head_lanes = [slice(h * HG_KDIM, (h + 1) * HG_KDIM) for h in range(HG_HEADS)]
    sc = [[jnp.where(same_chunk_causal, _mm_nt(qd[bs, sl], kd[bs, sl]), 0.0) for bs in blocks] for sl in head_lanes]
    grow = [[_mm_tn(v[rs, sl], kl[rs, sl]) for rs in chunk_rows] for sl in head_lanes]
    y_a = _mm(diff_a, bd_ref[...]) * pscale_ref[...]
    gate_c = proj((C_GATE[0] + 2 * D_MODEL, C_GATE[1]))

    y_c = jnp.zeros((T, MEM_WIDTH), F32)
    inv_sum = jnp.zeros((T, MEM_WIDTH), F32)
    for h in range(MEM_HEADS):
        e = jnp.exp(att[h] - jnp.max(att[h], axis=-1, keepdims=True))
        y_c = y_c + jnp.where(head_of_lane == h, _mm_nt(e, mv), 0.0)
        inv_sum = inv_sum + jnp.where(head_of_lane == h, 1.0 / jnp.sum(e, axis=-1, keepdims=True), 0.0)
    y_c = y_c * inv_sum
    m_a = _mm(y_a, wbp_ref[...])

    heads = []
    for h, sl in enumerate(head_lanes):
        o_intra = jnp.concatenate([_mm(sc[h][i], v[bs, sl]) for i, bs in enumerate(blocks)], axis=0)
        st = st_scr[h]
        o_inter = []
        for c, rs in enumerate(chunk_rows):
            o_inter.append(_mm_nt(qin[rs, sl], st))
            st = st * dec[c][:, sl] + grow[h][c]
        st_scr[h] = st
        heads.append(_rms_head(o_intra + jnp.concatenate(o_inter, axis=0), hgg[:, sl]))
    m_c = _mm(y_c, wbm_ref[...])
    y_b = jnp.concatenate(heads, axis=1) * _silu_from_half(gb)
    merged2 = _gated(gate_a, m_a) + _gated(gate_c, m_c) + _gated(gate_b, _mm(y_b, wbh_ref[...]))
    _project_norm(_rows_of(h_ref.at[0]), x, merged2, wout_ref[...], g_ref[...], b_ref[...], alpha)

    @pl.when(j == pl.num_programs(1) - 1)
    def _():
        newpool_ref[0] = pool_scr[...]
        for h in range(HG_HEADS):
            newhg_ref[0, h] = st_scr[h].T


def _const_spec(shape):
    n = len(shape)
    return pl.BlockSpec(shape, lambda *_: (0,) * n, pipeline_mode=pl.Buffered(1))


def _mixer_prompt(x, mk, mv, lb_logits, w, layer, pos0, alpha, to_cast):
    bsz, seq, _ = x.shape
    T = PROMPT_T
    n_j = seq // T
    per_b3 = lambda b, j: (b, 0, 0)
    cast_blocks = tuple(_cast_block_count(a.shape[0], bsz * n_j) for a in to_cast)
    cast_specs = [pl.BlockSpec((a.shape[0] // n, a.shape[1]), lambda b, j, n=n: (jnp.minimum(b * n_j + j, n - 1), 0))
                  for a, n in zip(to_cast, cast_blocks)]
    kern = functools.partial(_mixer_prompt_kernel, layer=layer, pos0=pos0, alpha=alpha, cast_blocks=cast_blocks)
    weights = [lb_logits, w['w_in'], w['bd_pool'], w['pool_scale'], w['hg_norm_g'], w['w_br_pool'], w['w_br_hg'],
               w['w_br_mem'], w['w_out'], w['ln1_g'], w['ln1_b']]
    outs = pl.pallas_call(
        kern,
        grid=(bsz, n_j),
        in_specs=[pl.BlockSpec((1, T, D_MODEL), lambda b, j: (b, j, 0)),
                  pl.BlockSpec((1,) + mk.shape[1:], per_b3),
                  pl.BlockSpec((1,) + mv.shape[1:], per_b3)]
                 + [_const_spec(a.shape) for a in weights] + cast_specs,
        out_specs=[pl.BlockSpec((1, T, D_MODEL), lambda b, j: (b, j, 0)),
                   pl.BlockSpec((1, CARRY_ROWS, POOL_WIDTH), per_b3),
                   pl.BlockSpec((1, HG_HEADS, HG_KDIM, HG_VDIM), lambda b, j: (b, 0, 0, 0))] + cast_specs,
        out_shape=[jax.ShapeDtypeStruct((bsz, seq, D_MODEL), F32),
                   jax.ShapeDtypeStruct((bsz, CARRY_ROWS, POOL_WIDTH), F32),
                   jax.ShapeDtypeStruct((bsz, HG_HEADS, HG_KDIM, HG_VDIM), F32)]
                  + [jax.ShapeDtypeStruct(a.shape, BF16) for a in to_cast],
        scratch_shapes=[pltpu.VMEM((HG_HEADS, HG_VDIM, HG_KDIM), F32),
                        pltpu.VMEM((CARRY_ROWS, POOL_WIDTH), F32)],
        compiler_params=pltpu.CompilerParams(dimension_semantics=("arbitrary", "arbitrary"),
                                             vmem_limit_bytes=VMEM_LIMIT),
        name="mixer_prompt",
    )(x, mk, mv, *weights, *to_cast)
    return outs[:3], outs[3:]


def _conv_gate(full, tail, u, cw, cb):
    c = cb + pltpu.roll(full, 2, 0) * cw[0:1] + pltpu.roll(full, 1, 0) * cw[1:2] + full * cw[2:3]
    return _gelu(tail(c)) * u


def _ffn_prompt_step(j, n_j, h_ref, wg_ref, wu_ref, cw_ref, cb_ref, wd_ref, g_ref, b_ref, y_ref, newconv_ref,
                     carry_scr, alpha):
    T = FFN_T

    @pl.when(j == 0)
    def _():
        carry_scr[...] = jnp.zeros_like(carry_scr)

    h = h_ref[0]
    hb = h.astype(BF16)
    a = jnp.dot(hb, wg_ref[...], preferred_element_type=F32)
    u = jnp.dot(hb, wu_ref[...], preferred_element_type=F32)
    full = jnp.concatenate([carry_scr[...], a], axis=0)
    gated = _conv_gate(full, lambda z: z[CONV_CARRY:], u, cw_ref[...], cb_ref[...])
    _project_norm(_rows_of(y_ref.at[0]), h, gated, wd_ref[...], g_ref[...], b_ref[...], alpha)
    carry_scr[...] = a[T - CONV_CARRY:]

    @pl.when(j == n_j - 1)
    def _():
        newconv_ref[0] = carry_scr[...]


def _ffn_sample_step(seq, h_ref, conv_ref, wg_ref, wu_ref, cw_ref, cb_ref, wd_ref, g_ref, b_ref, y_ref, newconv_ref,
                     alpha):
    bsz = conv_ref.shape[0]
    h = h_ref[...]
    hb = h.astype(BF16)
    a = jnp.dot(hb, wg_ref[...], preferred_element_type=F32)
    u = jnp.dot(hb, wu_ref[...], preferred_element_type=F32)
    rows = [conv_ref[:, i, :] for i in range(CONV_W - 1)] + [a[t * bsz:(t + 1) * bsz] for t in range(seq)]
    cw = cw_ref[...]
    c = jnp.concatenate([cb_ref[...] + sum(rows[t + i] * cw[i:i + 1] for i in range(CONV_W)) for t in range(seq)],
                        axis=0)

    def store(first_row, val):
        for t in range(val.shape[0] // bsz):
            y_ref[:, first_row // bsz + t, :] = val[t * bsz:(t + 1) * bsz]

    _project_norm(store, h, _gelu(c) * u, wd_ref[...], g_ref[...], b_ref[...], alpha,
                  group_rows=bsz * max(1, NORM_ROWS // bsz))
    for i in range(CONV_W - 1):
        newconv_ref[:, i, :] = rows[seq + i]


def _ffn_kernel(hp_ref, hs_ref, convs_ref, wg_ref, wu_ref, cw_ref, cb_ref, wd_ref, g_ref, b_ref,
                yp_ref, newconvp_ref, ys_ref, newconvs_ref, carry_scr, *, alpha, n_j, prompt_steps, seq_s):
    i = pl.program_id(0)
    weights = (wg_ref, wu_ref, cw_ref, cb_ref, wd_ref, g_ref, b_ref)

    @pl.when(i < prompt_steps)
    def _():
        _ffn_prompt_step(i % n_j, n_j, hp_ref, *weights, yp_ref, newconvp_ref, carry_scr, alpha)

    @pl.when(i == prompt_steps)
    def _():
        _ffn_sample_step(seq_s, hs_ref, convs_ref, *weights, ys_ref, newconvs_ref, alpha)


def _ffn(h_prompt, h_sample_t2d, conv_sample, w, alpha):
    bsz, seq, _ = h_prompt.shape
    n_sample = conv_sample.shape[0]
    seq_s = h_sample_t2d.shape[0] // n_sample
    T = FFN_T
    n_j = seq // T
    prompt_steps = bsz * n_j
    weights = [w['w_gate'], w['w_up'], w['conv_w'], w['conv_b'], w['w_down'], w['ln2_g'], w['ln2_b']]
    block = lambda i: jnp.minimum(i, prompt_steps - 1)
    whole = lambda shape: pl.BlockSpec(shape, lambda i: (0,) * len(shape))
    ys_shape = (n_sample, seq_s, D_MODEL)
    return pl.pallas_call(
        functools.partial(_ffn_kernel, alpha=alpha, n_j=n_j, prompt_steps=prompt_steps, seq_s=seq_s),
        grid=(prompt_steps + 1,),
        in_specs=[pl.BlockSpec((1, T, D_MODEL), lambda i: (block(i) // n_j, block(i) % n_j, 0)),
                  _const_spec(h_sample_t2d.shape), _const_spec(conv_sample.shape)]
                 + [_const_spec(a.shape) for a in weights],
        out_specs=[pl.BlockSpec((1, T, D_MODEL), lambda i: (block(i) // n_j, block(i) % n_j, 0)),
                   pl.BlockSpec((1, CONV_CARRY, D_FF), lambda i: (block(i) // n_j, 0, 0)),
                   whole(ys_shape), whole(conv_sample.shape)],
        out_shape=[jax.ShapeDtypeStruct((bsz, seq, D_MODEL), F32),
                   jax.ShapeDtypeStruct((bsz, CONV_CARRY, D_FF), F32),
                   jax.ShapeDtypeStruct(ys_shape, F32),
                   jax.ShapeDtypeStruct(conv_sample.shape, F32)],
        scratch_shapes=[pltpu.VMEM((CONV_CARRY, D_FF), F32)],
        compiler_params=pltpu.CompilerParams(dimension_semantics=("arbitrary",),
                                             vmem_limit_bytes=V7X_VMEM_BYTES - 4 * 1024 * 1024),
        name="ffn",
    )(h_prompt, h_sample_t2d, conv_sample, *weights)


def _bmm(spec, a, b):
    return jnp.einsum(spec, a.astype(BF16), b.astype(BF16), preferred_element_type=F32)


def _mixer_sample_kernel(x_ref, mk_ref, mv_ref, pool_ref, hg_ref, lb_ref, win_ref, bd_ref, pscale_ref, hgg_ref,
                         wbp_ref, wbh_ref, wbm_ref, wout_ref, g_ref, b_ref,
                         h_ref, newpool_ref, newhg_ref, seq_scr, out_scr, *, layer, pos0, alpha, seq):
    Bb, S = SAMPLE_BB, SUBLANES
    R = Bb * S
    M = seq * Bb
    slab = lambda z, t: z[t * Bb:(t + 1) * Bb]
    x = jnp.concatenate([x_ref[:, t, :] for t in range(seq)], axis=0)
    xb = x.astype(BF16)
    proj = lambda c: jnp.dot(xb, win_ref[:, c[0]:c[1]], preferred_element_type=F32)

    u_a = proj(C_UA)
    rows = [pool_ref[i] for i in range(POOL_BUF)] + [slab(u_a, t) for t in range(seq)]
    n_rows = len(rows)
    sums = {1: dict(enumerate(rows))}
    for w in POOL_WINDOWS:
        half = sums[w // 2]
        sums[w] = {i: half[i] + half[i - w // 2] for i in range(n_rows) if i - w // 2 in half and i in half}
    grp = lax.broadcasted_iota(jnp.int32, (Bb, POOL_WIDTH), 1) // POOL_GDIM
    diffs = []
    for t in range(seq):
        i = POOL_BUF + t
        pooled = [sums[w][i] / float(min(pos0 + t + 1, w)) for w in POOL_WINDOWS]
        mean = jnp.where(grp == 0, pooled[0], jnp.where(grp == 1, pooled[1], jnp.where(grp == 2, pooled[2], pooled[3])))
        diffs.append(mean - rows[i])
    y_a = _mm(jnp.concatenate(diffs, axis=0), bd_ref[...]) * pscale_ref[...]
    for i in range(POOL_BUF):
        newpool_ref[i] = rows[n_rows - POOL_BUF + i]

    lb = _lower_bound(lb_ref[...], layer)
    logf_t, k_t = _forget_gates(proj(C_FB), lb)
    per_seq_in = jnp.concatenate([logf_t, k_t, _silu_from_half(proj(C_QB)), proj(C_IB), proj(C_QC)], axis=1)
    seq_scr[...] = jnp.zeros_like(seq_scr)
    for c in range(seq_scr.shape[0]):
        for t in range(seq):
            seq_scr[c, pl.ds(t, Bb, stride=S), :] = slab(per_seq_in, t)[:, c * LANES:(c + 1) * LANES]
    cols = lambda lo, hi: jnp.concatenate([seq_scr[c] for c in range(lo // LANES, hi // LANES)], axis=1)
    logf = cols(0, HG_WIDTH)
    k = cols(HG_WIDTH, 2 * HG_WIDTH)
    q = cols(2 * HG_WIDTH, 3 * HG_WIDTH)
    v = cols(3 * HG_WIDTH, 3 * HG_WIDTH + HG_VWIDTH)
    qc = cols(3 * HG_WIDTH + HG_VWIDTH, 3 * HG_WIDTH + HG_VWIDTH + MEM_WIDTH)
    to3 = lambda z: z.reshape(Bb, S, z.shape[-1])
    b3 = to3(_select_mm(_as_bf16(_chunk_tri(R, S)), logf))
    bm = b3[:, S // 2:S // 2 + 1, :]
    bl = b3[:, S - 1:S, :]
    q3, k3, v3 = to3(q), to3(k), to3(v)
    qin = q3 * jnp.exp(b3)
    qd = q3 * jnp.exp(b3 - bm)
    kd = k3 * jnp.exp(bm - b3)
    kl = k3 * jnp.exp(bl - b3)
    rr = lax.broadcasted_iota(jnp.int32, (Bb, S, S), 1)
    cc = lax.broadcasted_iota(jnp.int32, (Bb, S, S), 2)
    causal = cc <= rr
    p1 = bl.astype(BF16).astype(F32)
    p2 = (bl - p1).astype(BF16).astype(F32)
    p3 = (bl - p1) - p2
    r3 = lax.broadcasted_iota(jnp.int32, (Bb, S, HG_WIDTH), 1)
    pieces = jnp.where(r3 == 0, p1, jnp.where(r3 == 1, p2, jnp.where(r3 == 2, p3, 0.0)))
    ones = jnp.ones((Bb, S, HG_VDIM), BF16)
    hgg = hgg_ref[...]
    heads = []
    for h in range(HG_HEADS):
        sl = slice(h * HG_KDIM, (h + 1) * HG_KDIM)
        sc = jnp.where(causal, _bmm('bqd,bkd->bqk', qd[:, :, sl], kd[:, :, sl]), 0.0)
        s0 = hg_ref[:, h]
        o = _bmm('bqk,bke->bqe', sc, v3[:, :, sl]) + _bmm('bqd,bde->bqe', qin[:, :, sl], s0)
        logdec = _bmm('bkd,bke->bde', pieces[:, :, sl], ones)
        newhg_ref[:, h] = jnp.exp(logdec) * s0 + _bmm('bkd,bke->bde', kl[:, :, sl], v3[:, :, sl])
        heads.append(_rms_head(o, hgg[:, sl]).reshape(R, HG_VDIM))

    qc3 = to3(qc)
    head_of_lane = lax.broadcasted_iota(jnp.int32, (Bb, S, MEM_WIDTH), 2) // MEM_HDIM
    q4 = jnp.concatenate([jnp.where(head_of_lane == h, qc3, 0.0) for h in range(MEM_HEADS)], axis=1)
    s = _bmm('bqd,bdm->bqm', q4, mk_ref[...]) * (MEM_HDIM ** -0.5)
    e = jnp.exp(s - jnp.max(s, axis=-1, keepdims=True))
    p = e / jnp.sum(e, axis=-1, keepdims=True)
    o4 = _bmm('bqm,bdm->bqd', p, mv_ref[...])
    y_c = jnp.zeros((Bb, S, MEM_WIDTH), F32)
    for h in range(MEM_HEADS):
        y_c = y_c + jnp.where(head_of_lane == h, o4[:, h * S:(h + 1) * S, :], 0.0)

    staged = jnp.concatenate(heads + [y_c.reshape(R, MEM_WIDTH)], axis=1)
    for c in range(out_scr.shape[0]):
        out_scr[c] = staged[:, c * LANES:(c + 1) * LANES]
    per_seq_out = jnp.concatenate(
        [jnp.concatenate([out_scr[c, pl.ds(t, Bb, stride=S), :] for c in range(out_scr.shape[0])], axis=1)
         for t in range(seq)], axis=0)
    y_b = per_seq_out[:, :HG_VWIDTH] * _silu_from_half(proj(C_GB))
    y_c = per_seq_out[:, HG_VWIDTH:]
    h = _merge_out(x, y_a, y_b, y_c, proj(C_GATE), wbp_ref[...], wbh_ref[...], wbm_ref[...], wout_ref[...],
                   g_ref[...], b_ref[...], alpha)
    h_ref[...] = h.reshape(seq, Bb, D_MODEL)


def _mixer_sample(x, mk, mv, pool_t, hg, lb_logits, w, layer, pos0, alpha):
    Bb, S = SAMPLE_BB, SUBLANES
    bsz, seq, _ = x.shape
    n_mem = mk.shape[2]
    b3 = lambda i: (i, 0, 0)
    tb3 = lambda i: (0, i, 0)
    weights = [lb_logits, w['w_in'], w['bd_pool'], w['pool_scale'], w['hg_norm_g'], w['w_br_pool'], w['w_br_hg'],
               w['w_br_mem'], w['w_out'], w['ln1_g'], w['ln1_b']]
    per_seq_in = 3 * HG_WIDTH + HG_VWIDTH + MEM_WIDTH
    per_seq_out = HG_VWIDTH + MEM_WIDTH
    return pl.pallas_call(
        functools.partial(_mixer_sample_kernel, layer=layer, pos0=pos0, alpha=alpha, seq=seq),
        grid=(bsz // Bb,),
        in_specs=[pl.BlockSpec((Bb, seq, D_MODEL), b3),
                  pl.BlockSpec((Bb, MEM_WIDTH, n_mem), b3),
                  pl.BlockSpec((Bb, MEM_WIDTH, n_mem), b3),
                  pl.BlockSpec((POOL_BUF, Bb, POOL_WIDTH), tb3),
                  pl.BlockSpec((Bb, HG_HEADS, HG_KDIM, HG_VDIM), lambda i: (i, 0, 0, 0))]
                 + [_const_spec(a.shape) for a in weights],
        out_specs=[pl.BlockSpec((seq, Bb, D_MODEL), tb3),
                   pl.BlockSpec((POOL_BUF, Bb, POOL_WIDTH), tb3),
                   pl.BlockSpec((Bb, HG_HEADS, HG_KDIM, HG_VDIM), lambda i: (i, 0, 0, 0))],
        out_shape=[jax.ShapeDtypeStruct((seq, bsz, D_MODEL), F32),
                   jax.ShapeDtypeStruct((POOL_BUF, bsz, POOL_WIDTH), F32),
                   jax.ShapeDtypeStruct((bsz, HG_HEADS, HG_KDIM, HG_VDIM), F32)],
        scratch_shapes=[pltpu.VMEM((per_seq_in // LANES, Bb * S, LANES), F32),
                        pltpu.VMEM((per_seq_out // LANES, Bb * S, LANES), F32)],
        compiler_params=pltpu.CompilerParams(dimension_semantics=("arbitrary",), vmem_limit_bytes=VMEM_LIMIT),
        name="mixer_sample",
    )(x, mk, mv, pool_t, hg, *weights)


def _block_diag(w_grp):
    groups, gdim, _ = w_grp.shape
    out = jnp.zeros((groups * gdim, groups * gdim), w_grp.dtype)
    for g in range(groups):
        out = lax.dynamic_update_slice(out, w_grp[g], (g * gdim, g * gdim))
    return out


def kernel(x_prompt, x_sample, state_pool, state_hgrn, state_ffn_conv, cache_mem_k, cache_mem_v, mem_prompt, lb_logits, w_in, w_pool_grp, pool_scale, hg_norm_g, w_mem_k, w_mem_v, w_br_pool, w_br_hg, w_br_mem, w_out, ln1_g, ln1_b, w_gate, w_up, conv_w, conv_b, w_down, ln2_g, ln2_b):
    depth = w_in.shape[0]
    alpha = (2 * depth) ** 0.25
    n_prompt, seq_p, _ = x_prompt.shape
    n_sample, seq_s, _ = x_sample.shape
    n_mem = mem_prompt.shape[1]
    assert seq_p % PROMPT_T == 0 and seq_p % FFN_T == 0 and seq_s <= SUBLANES and n_sample % SAMPLE_BB == 0

    hp = x_prompt
    time_major = lambda a: jnp.transpose(a, (1, 0, 2))
    hs = x_sample
    mem2d = mem_prompt.reshape(n_prompt * n_mem, D_MODEL)
    row = lambda a: a.reshape(1, -1)
    col = jnp.arange(w_in.shape[-1])
    in_range = lambda c: (col >= c[0]) & (col < c[1])
    half_cols = jnp.where(in_range(C_QB) | in_range(C_FB) | in_range(C_GB) | in_range(C_GATE), 0.5, 1.0).astype(F32)
    outs = [[] for _ in range(8)]
    for l in range(depth):
        w = {'bd_pool': _block_diag(w_pool_grp[l]).astype(BF16),
             'pool_scale': row(pool_scale[l]), 'hg_norm_g': row(hg_norm_g[l]),
             'ln1_g': row(ln1_g[l]), 'ln1_b': row(ln1_b[l]),
             'conv_w': conv_w[l], 'conv_b': row(conv_b[l]), 'ln2_g': row(ln2_g[l]), 'ln2_b': row(ln2_b[l])}
        mkt, mvt, (w['w_in'], w['w_out'], w['w_br_pool'], w['w_br_hg'], w['w_br_mem']) = _memkv(
            mem2d, w_mem_k[l], w_mem_v[l], n_mem,
            [(w_in[l], half_cols), (w_out[l], jnp.full((D_MODEL,), 0.5, F32)), (w_br_pool[l], None),
             (w_br_hg[l], None), (w_br_mem[l], None)])
        (h_mid, pool_p, hg_p), (w['w_gate'], w['w_up'], w['w_down']) = _mixer_prompt(
            hp, mkt, mvt, lb_logits, w, l, 0, alpha, (w_gate[l], w_up[l], w_down[l]))
        hs_mid, pool_s, hg_s = _mixer_sample(
            hs, _feature_major(cache_mem_k[l]), _feature_major(cache_mem_v[l]),
            time_major(state_pool[l]), state_hgrn[l], lb_logits, w, l, PAST_LEN, alpha)
        hp, conv_p, hs, conv_s = _ffn(h_mid, hs_mid.reshape(seq_s * n_sample, D_MODEL), state_ffn_conv[l], w, alpha)
        outs[0].append(pool_p[:, CARRY_ROWS - POOL_BUF:])
        outs[1].append(hg_p)
        outs[2].append(conv_p[:, CONV_CARRY - (CONV_W - 1):])
        outs[3].append(_token_major(mkt))
        outs[4].append(_token_major(mvt))
        outs[5].append(time_major(pool_s))
        outs[6].append(hg_s)
        outs[7].append(conv_s)
    return (hp, hs) + tuple(jnp.stack(o) for o in outs)
```

```python
import functools

import jax
import jax.numpy as jnp
from jax import lax
from jax.experimental import pallas as pl
from jax.experimental.pallas import tpu as pltpu

F32 = jnp.float32
BF16 = jnp.bfloat16

D_MODEL = 1024
POOL_WIDTH = 256
POOL_WINDOWS = (2, 4, 8, 16)
POOL_GDIM = 64
POOL_BUF = 15
HG_HEADS = 4
HG_KDIM = 128
HG_WIDTH = 512
HG_VDIM = 128
HG_VWIDTH = 512
MEM_HEADS = 4
MEM_HDIM = 64
MEM_WIDTH = 256
N_BRANCH = 3
D_FF = 2816
CONV_W = 3
LN_EPS = 1e-5
RMS_EPS = 1e-6

C_UA = (0, 256)
C_QB = (256, 768)
C_FB = (768, 1280)
C_IB = (1280, 1792)
C_GB = (1792, 2304)
C_QC = (2304, 2560)
C_GATE = (2560, 5632)

LANES = 128
SUBLANES = 8
BF16_TILE_ROWS = 16
CARRY_ROWS = 16
CONV_CARRY = 8
PROMPT_T = 512
FFN_T = 1024
NORM_ROWS = 256
HG_BLOCK = 256
HG_CHUNK = 64
MEMKV_SEQS = 2
SAMPLE_BB = 16
V7X_VMEM_BYTES = 64 * 1024 * 1024
VMEM_LIMIT = V7X_VMEM_BYTES - 8 * 1024 * 1024
PAST_LEN = 16384


def _mm(a, b):
    return jnp.dot(a.astype(BF16), b.astype(BF16), preferred_element_type=F32)


def _mm_nt(a, b):
    return lax.dot_general(a.astype(BF16), b.astype(BF16), (((1,), (1,)), ((), ())), preferred_element_type=F32)


def _mm_tn(a, b):
    return lax.dot_general(a.astype(BF16), b.astype(BF16), (((0,), (0,)), ((), ())), preferred_element_type=F32)


def _split3(x):
    h1 = x.astype(BF16)
    r1 = x - h1.astype(F32)
    h2 = r1.astype(BF16)
    h3 = (r1 - h2.astype(F32)).astype(BF16)
    return h1, h2, h3


def _select_mm(sel, x):
    h1, h2, h3 = _split3(x)
    d = lambda p: jnp.dot(sel, p, preferred_element_type=F32)
    return d(h1) + d(h2) + d(h3)


def _silu_from_half(p):
    return p + p * jnp.tanh(p)


def _gated(p, m):
    return m + jnp.tanh(p) * m


def _layer_norm(x, g, b):
    mu = jnp.mean(x, axis=-1, keepdims=True)
    xc = x - mu
    var = jnp.mean(xc * xc, axis=-1, keepdims=True)
    return xc * lax.rsqrt(var + LN_EPS) * g + b


def _gelu(x):
    return 0.5 * x * (1.0 + lax.erf(x * (2.0 ** -0.5)))


def _lower_bound(lb_logits, layer):
    m = jnp.max(lb_logits, axis=0, keepdims=True)
    e = jnp.exp(lb_logits - m)
    sm = e / jnp.sum(e, axis=0, keepdims=True)
    return jnp.sum(sm[:layer + 1], axis=0, keepdims=True)


def _pool_diff(full, u_a, pos, tail):
    s2 = full + pltpu.roll(full, 1, 0)
    s4 = s2 + pltpu.roll(s2, 2, 0)
    s8 = s4 + pltpu.roll(s4, 4, 0)
    s16 = s8 + pltpu.roll(s8, 8, 0)
    grp = lax.broadcasted_iota(jnp.int32, u_a.shape, 1) // POOL_GDIM
    wsum = jnp.where(grp == 0, tail(s2), jnp.where(grp == 1, tail(s4), jnp.where(grp == 2, tail(s8), tail(s16))))
    wlen = jnp.where(grp == 0, 2, jnp.where(grp == 1, 4, jnp.where(grp == 2, 8, 16)))
    count = jnp.minimum(pos + 1, wlen).astype(F32)
    return wsum / count - u_a


def _forget_gates(fb_half, lb):
    th = jnp.tanh(fb_half)
    f = lb + (1.0 - lb) * (0.5 + 0.5 * th)
    k = (1.0 - lb) * (0.5 - 0.5 * th)
    return jnp.log(f), k


def _chunk_tri(n, chunk):
    r = lax.broadcasted_iota(jnp.int32, (n, n), 0)
    c = lax.broadcasted_iota(jnp.int32, (n, n), 1)
    return (r // chunk == c // chunk) & (c <= r)


def _as_bf16(mask):
    return jnp.where(mask, 1.0, 0.0).astype(BF16)


def _rms_head(o, g):
    return o * lax.rsqrt(jnp.mean(o * o, axis=-1, keepdims=True) + RMS_EPS) * g


def _project_norm(store, resid, lhs, w, g, b, alpha, group_rows=NORM_ROWS):
    n = resid.shape[0]
    step = min(n, group_rows)
    for i in range(0, n, step):
        store(i, _layer_norm(alpha * resid[i:i + step] + _mm(lhs[i:i + step], w), g, b))


def _rows_of(ref2d):
    def store(first_row, val):
        ref2d[first_row:first_row + val.shape[0], :] = val
    return store


def _merge_out(x, y_a, y_b, y_c, gate_pre, wbp, wbh, wbm, wout, g, b, alpha):
    merged2 = (_gated(gate_pre[:, 0:D_MODEL], _mm(y_a, wbp))
               + _gated(gate_pre[:, D_MODEL:2 * D_MODEL], _mm(y_b, wbh))
               + _gated(gate_pre[:, 2 * D_MODEL:3 * D_MODEL], _mm(y_c, wbm)))
    return _layer_norm(alpha * x + _mm(merged2, wout), g, b)


def _cast_block_count(rows, n_steps):
    return max(n for n in range(1, n_steps + 1) if rows % n == 0 and (rows // n) % BF16_TILE_ROWS == 0)


def _ride_along_casts(step, srcs, scales, dsts, block_counts):
    for src, scale, dst, n_blocks in zip(srcs, scales, dsts, block_counts):
        @pl.when(step < n_blocks)
        def _(src=src, scale=scale, dst=dst):
            val = src[...] if scale is None else src[...] * scale[...]
            dst[...] = val.astype(BF16)


def _memkv_kernel(mem_ref, wk_ref, wv_ref, *rest, cast_blocks, scaled):
    n_cast = len(cast_blocks)
    cast_in, scale_refs = rest[:n_cast], list(rest[n_cast:n_cast + sum(scaled)])
    kt_ref, vt_ref = rest[n_cast + sum(scaled):n_cast + sum(scaled) + 2]
    cast_out = rest[n_cast + sum(scaled) + 2:]
    n_mem = kt_ref.shape[2]
    for s in range(kt_ref.shape[0]):
        mt = mem_ref[s * n_mem:(s + 1) * n_mem, :].T
        kt_ref[s] = _mm_tn(wk_ref[...], mt)
        vt_ref[s] = _mm_tn(wv_ref[...], mt)
    scales = [scale_refs.pop(0) if s else None for s in scaled]
    _ride_along_casts(pl.program_id(0), cast_in, scales, cast_out, cast_blocks)


def _memkv(mem2d, wk, wv, n_mem, to_cast):
    n_seq = mem2d.shape[0] // n_mem
    per_step = MEMKV_SEQS if n_seq % MEMKV_SEQS == 0 else 1
    n_steps = n_seq // per_step
    const = lambda i: (0, 0)
    per_seq = lambda i: (i, 0, 0)
    arrays = [a for a, _ in to_cast]
    scale_rows = [s.reshape(1, -1) for _, s in to_cast if s is not None]
    cast_blocks = tuple(_cast_block_count(a.shape[0], n_steps) for a in arrays)
    cast_specs = [pl.BlockSpec((a.shape[0] // n, a.shape[1]), lambda i, n=n: (jnp.minimum(i, n - 1), 0))
                  for a, n in zip(arrays, cast_blocks)]
    outs = pl.pallas_call(
        functools.partial(_memkv_kernel, cast_blocks=cast_blocks, scaled=tuple(s is not None for _, s in to_cast)),
        grid=(n_steps,),
        in_specs=[pl.BlockSpec((per_step * n_mem, D_MODEL), lambda i: (i, 0)),
                  pl.BlockSpec((D_MODEL, MEM_WIDTH), const),
                  pl.BlockSpec((D_MODEL, MEM_WIDTH), const)]
                 + cast_specs + [pl.BlockSpec(s.shape, const) for s in scale_rows],
        out_specs=[pl.BlockSpec((per_step, MEM_WIDTH, n_mem), per_seq),
                   pl.BlockSpec((per_step, MEM_WIDTH, n_mem), per_seq)] + cast_specs,
        out_shape=[jax.ShapeDtypeStruct((n_seq, MEM_WIDTH, n_mem), F32)] * 2
                  + [jax.ShapeDtypeStruct(a.shape, BF16) for a in arrays],
        name="memkv",
    )(mem2d, wk, wv, *arrays, *scale_rows)
    return outs[0], outs[1], outs[2:]


def _feature_major(mem):
    return jnp.transpose(mem, (0, 2, 3, 1)).reshape(mem.shape[0], MEM_WIDTH, mem.shape[1])


def _token_major(mem_t):
    bsz, _, n_mem = mem_t.shape
    return jnp.transpose(mem_t.reshape(bsz, MEM_HEADS, MEM_HDIM, n_mem), (0, 3, 1, 2))


def _mixer_prompt_kernel(x_ref, mk_ref, mv_ref, lb_ref, win_ref, bd_ref, pscale_ref, hgg_ref,
                         wbp_ref, wbh_ref, wbm_ref, wout_ref, g_ref, b_ref, *rest, layer, pos0, alpha, cast_blocks):
    n_cast = len(cast_blocks)
    cast_in, (h_ref, newpool_ref, newhg_ref) = rest[:n_cast], rest[n_cast:n_cast + 3]
    cast_out, (st_scr, pool_scr) = rest[n_cast + 3:2 * n_cast + 3], rest[2 * n_cast + 3:]
    T, C = PROMPT_T, HG_CHUNK
    j = pl.program_id(1)

    _ride_along_casts(pl.program_id(0) * pl.num_programs(1) + j, cast_in, [None] * n_cast, cast_out, cast_blocks)

    @pl.when(j == 0)
    def _():
        pool_scr[...] = jnp.zeros_like(pool_scr)
        st_scr[...] = jnp.zeros_like(st_scr)

    x = x_ref[0]
    xb = x.astype(BF16)
    proj = lambda c: jnp.dot(xb, win_ref[:, c[0]:c[1]], preferred_element_type=F32)

    lb = _lower_bound(lb_ref[...], layer)
    logf, k = _forget_gates(proj(C_FB), lb)
    qb = proj(C_QB)
    v = proj(C_IB)
    same_chunk_causal = _chunk_tri(HG_BLOCK, C)
    tri = _as_bf16(same_chunk_causal)
    blocks = [slice(i * HG_BLOCK, (i + 1) * HG_BLOCK) for i in range(T // HG_BLOCK)]
    bcum = jnp.concatenate([_select_mm(tri, logf[bs]) for bs in blocks], axis=0)
    u_a = proj(C_UA)
    qc = proj(C_QC)
    gb = proj(C_GB)
    gate_a = proj((C_GATE[0], C_GATE[0] + D_MODEL))

    full = jnp.concatenate([pool_scr[...], u_a], axis=0)
    pos = pos0 + j * T + lax.broadcasted_iota(jnp.int32, (T, POOL_WIDTH), 0)
    diff_a = _pool_diff(full, u_a, pos, lambda z: z[CARRY_ROWS:])
    pool_scr[...] = full[T:]

    mk = mk_ref[0].astype(BF16)
    mv = mv_ref[0].astype(BF16)
    head_of_lane = lax.broadcasted_iota(jnp.int32, (T, MEM_WIDTH), 1) // MEM_HDIM
    att = [_mm(jnp.where(head_of_lane == h, qc, 0.0), mk) * (MEM_HDIM ** -0.5) for h in range(MEM_HEADS)]
    gate_b = proj((C_GATE[0] + D_MODEL, C_GATE[0] + 2 * D_MODEL))

    q = _silu_from_half(qb)
    n_chunks = T // C
    chunk_rows = [slice(c * C, (c + 1) * C) for c in range(n_chunks)]
    chunk_row = lambda c, row: bcum[c * C + row:c * C + row + 1]
    per_chunk = lambda fn: jnp.concatenate([jnp.broadcast_to(fn(c), (C, HG_WIDTH)) for c in range(n_chunks)], axis=0)
    bm = per_chunk(lambda c: chunk_row(c, C // 2))
    qin = q * jnp.exp(bcum)
    qd = qin * per_chunk(lambda c: jnp.exp(-chunk_row(c, C // 2)))
    kd = k * jnp.exp(bm - bcum)
    kl = kd * per_chunk(lambda c: jnp.exp(chunk_row(c, C - 1) - chunk_row(c, C // 2)))
    dec = [jnp.exp(chunk_row(c, C - 1)) for c in range(n_chunks)]
    hgg = hgg_ref[...]
    head_lanes = [slice(h * HG_KDIM, (h + 1) * HG_KDIM) for h in range(HG_HEADS)]
    sc = [[jnp.where(same_chunk_causal, _mm_nt(qd[bs, sl], kd[bs, sl]), 0.0) for bs in blocks] for sl in head_lanes]
    grow = [[_mm_tn(v[rs, sl], kl[rs, sl]) for rs in chunk_rows] for sl in head_lanes]
    y_a = _mm(diff_a, bd_ref[...]) * pscale_ref[...]
    gate_c = proj((C_GATE[0] + 2 * D_MODEL, C_GATE[1]))

    y_c = jnp.zeros((T, MEM_WIDTH), F32)
    inv_sum = jnp.zeros((T, MEM_WIDTH), F32)
    for h in range(MEM_HEADS):
        e = jnp.exp(att[h] - jnp.max(att[h], axis=-1, keepdims=True))
        y_c = y_c + jnp.where(head_of_lane == h, _mm_nt(e, mv), 0.0)
        inv_sum = inv_sum + jnp.where(head_of_lane == h, 1.0 / jnp.sum(e, axis=-1, keepdims=True), 0.0)
    y_c = y_c * inv_sum
    m_a = _mm(y_a, wbp_ref[...])

    heads = []
    for h, sl in enumerate(head_lanes):
        o_intra = jnp.concatenate([_mm(sc[h][i], v[bs, sl]) for i, bs in enumerate(blocks)], axis=0)
        st = st_scr[h]
        o_inter = []
        for c, rs in enumerate(chunk_rows):
            o_inter.append(_mm_nt(qin[rs, sl], st))
            st = st * dec[c][:, sl] + grow[h][c]
        st_scr[h] = st
        heads.append(_rms_head(o_intra + jnp.concatenate(o_inter, axis=0), hgg[:, sl]))
    m_c = _mm(y_c, wbm_ref[...])
    y_b = jnp.concatenate(heads, axis=1) * _silu_from_half(gb)
    merged2 = _gated(gate_a, m_a) + _gated(gate_c, m_c) + _gated(gate_b, _mm(y_b, wbh_ref[...]))
    _project_norm(_rows_of(h_ref.at[0]), x, merged2, wout_ref[...], g_ref[...], b_ref[...], alpha)

    @pl.when(j == pl.num_programs(1) - 1)
    def _():
        newpool_ref[0] = pool_scr[...]
        for h in range(HG_HEADS):
            newhg_ref[0, h] = st_scr[h].T


def _const_spec(shape):
    n = len(shape)
    return pl.BlockSpec(shape, lambda *_: (0,) * n, pipeline_mode=pl.Buffered(1))


def _mixer_prompt(x, mk, mv, lb_logits, w, layer, pos0, alpha, to_cast):
    bsz, seq, _ = x.shape
    T = PROMPT_T
    n_j = seq // T
    per_b3 = lambda b, j: (b, 0, 0)
    cast_blocks = tuple(_cast_block_count(a.shape[0], bsz * n_j) for a in to_cast)
    cast_specs = [pl.BlockSpec((a.shape[0] // n, a.shape[1]), lambda b, j, n=n: (jnp.minimum(b * n_j + j, n - 1), 0))
                  for a, n in zip(to_cast, cast_blocks)]
    kern = functools.partial(_mixer_prompt_kernel, layer=layer, pos0=pos0, alpha=alpha, cast_blocks=cast_blocks)
    weights = [lb_logits, w['w_in'], w['bd_pool'], w['pool_scale'], w['hg_norm_g'], w['w_br_pool'], w['w_br_hg'],
               w['w_br_mem'], w['w_out'], w['ln1_g'], w['ln1_b']]
    outs = pl.pallas_call(
        kern,
        grid=(bsz, n_j),
        in_specs=[pl.BlockSpec((1, T, D_MODEL), lambda b, j: (b, j, 0)),
                  pl.BlockSpec((1,) + mk.shape[1:], per_b3),
                  pl.BlockSpec((1,) + mv.shape[1:], per_b3)]
                 + [_const_spec(a.shape) for a in weights] + cast_specs,
        out_specs=[pl.BlockSpec((1, T, D_MODEL), lambda b, j: (b, j, 0)),
                   pl.BlockSpec((1, CARRY_ROWS, POOL_WIDTH), per_b3),
                   pl.BlockSpec((1, HG_HEADS, HG_KDIM, HG_VDIM), lambda b, j: (b, 0, 0, 0))] + cast_specs,
        out_shape=[jax.ShapeDtypeStruct((bsz, seq, D_MODEL), F32),
                   jax.ShapeDtypeStruct((bsz, CARRY_ROWS, POOL_WIDTH), F32),
                   jax.ShapeDtypeStruct((bsz, HG_HEADS, HG_KDIM, HG_VDIM), F32)]
                  + [jax.ShapeDtypeStruct(a.shape, BF16) for a in to_cast],
        scratch_shapes=[pltpu.VMEM((HG_HEADS, HG_VDIM, HG_KDIM), F32),
                        pltpu.VMEM((CARRY_ROWS, POOL_WIDTH), F32)],
        compiler_params=pltpu.CompilerParams(dimension_semantics=("arbitrary", "arbitrary"),
                                             vmem_limit_bytes=VMEM_LIMIT),
        name="mixer_prompt",
    )(x, mk, mv, *weights, *to_cast)
    return outs[:3], outs[3:]


def _conv_gate(full, tail, u, cw, cb):
    c = cb + pltpu.roll(full, 2, 0) * cw[0:1] + pltpu.roll(full, 1, 0) * cw[1:2] + full * cw[2:3]
    return _gelu(tail(c)) * u


def _ffn_prompt_kernel(h_ref, wg_ref, wu_ref, cw_ref, cb_ref, wd_ref, g_ref, b_ref,
                       y_ref, newconv_ref, carry_scr, *, alpha):
    T = FFN_T
    j = pl.program_id(1)

    @pl.when(j == 0)
    def _():
        carry_scr[...] = jnp.zeros_like(carry_scr)

    h = h_ref[0]
    hb = h.astype(BF16)
    a = jnp.dot(hb, wg_ref[...], preferred_element_type=F32)
    u = jnp.dot(hb, wu_ref[...], preferred_element_type=F32)
    full = jnp.concatenate([carry_scr[...], a], axis=0)
    gated = _conv_gate(full, lambda z: z[CONV_CARRY:], u, cw_ref[...], cb_ref[...])
    _project_norm(_rows_of(y_ref.at[0]), h, gated, wd_ref[...], g_ref[...], b_ref[...], alpha)
    carry_scr[...] = a[T - CONV_CARRY:]

    @pl.when(j == pl.num_programs(1) - 1)
    def _():
        newconv_ref[0] = carry_scr[...]


def _ffn_prompt(h, w, alpha):
    bsz, seq, _ = h.shape
    T = FFN_T
    weights = [w['w_gate'], w['w_up'], w['conv_w'], w['conv_b'], w['w_down'], w['ln2_g'], w['ln2_b']]
    return pl.pallas_call(
        functools.partial(_ffn_prompt_kernel, alpha=alpha),
        grid=(bsz, seq // T),
        in_specs=[pl.BlockSpec((1, T, D_MODEL), lambda b, j: (b, j, 0))]
                 + [_const_spec(a.shape) for a in weights],
        out_specs=[pl.BlockSpec((1, T, D_MODEL), lambda b, j: (b, j, 0)),
                   pl.BlockSpec((1, CONV_CARRY, D_FF), lambda b, j: (b, 0, 0))],
        out_shape=[jax.ShapeDtypeStruct((bsz, seq, D_MODEL), F32),
                   jax.ShapeDtypeStruct((bsz, CONV_CARRY, D_FF), F32)],
        scratch_shapes=[pltpu.VMEM((CONV_CARRY, D_FF), F32)],
        compiler_params=pltpu.CompilerParams(dimension_semantics=("arbitrary", "arbitrary"),
                                             vmem_limit_bytes=VMEM_LIMIT),
        name="ffn_prompt",
    )(h, *weights)


def _bmm(spec, a, b):
    return jnp.einsum(spec, a.astype(BF16), b.astype(BF16), preferred_element_type=F32)


def _mixer_sample_kernel(x_ref, mk_ref, mv_ref, pool_ref, hg_ref, lb_ref, win_ref, bd_ref, pscale_ref, hgg_ref,
                         wbp_ref, wbh_ref, wbm_ref, wout_ref, g_ref, b_ref,
                         h_ref, newpool_ref, newhg_ref, seq_scr, out_scr, *, layer, pos0, alpha, seq):
    Bb, S = SAMPLE_BB, SUBLANES
    R = Bb * S
    M = seq * Bb
    slab = lambda z, t: z[t * Bb:(t + 1) * Bb]
    x = jnp.concatenate([x_ref[:, t, :] for t in range(seq)], axis=0)
    xb = x.astype(BF16)
    proj = lambda c: jnp.dot(xb, win_ref[:, c[0]:c[1]], preferred_element_type=F32)

    u_a = proj(C_UA)
    rows = [pool_ref[i] for i in range(POOL_BUF)] + [slab(u_a, t) for t in range(seq)]
    n_rows = len(rows)
    sums = {1: dict(enumerate(rows))}
    for w in POOL_WINDOWS:
        half = sums[w // 2]
        sums[w] = {i: half[i] + half[i - w // 2] for i in range(n_rows) if i - w // 2 in half and i in half}
    grp = lax.broadcasted_iota(jnp.int32, (Bb, POOL_WIDTH), 1) // POOL_GDIM
    diffs = []
    for t in range(seq):
        i = POOL_BUF + t
        pooled = [sums[w][i] / float(min(pos0 + t + 1, w)) for w in POOL_WINDOWS]
        mean = jnp.where(grp == 0, pooled[0], jnp.where(grp == 1, pooled[1], jnp.where(grp == 2, pooled[2], pooled[3])))
        diffs.append(mean - rows[i])
    y_a = _mm(jnp.concatenate(diffs, axis=0), bd_ref[...]) * pscale_ref[...]
    for i in range(POOL_BUF):
        newpool_ref[i] = rows[n_rows - POOL_BUF + i]

    lb = _lower_bound(lb_ref[...], layer)
    logf_t, k_t = _forget_gates(proj(C_FB), lb)
    per_seq_in = jnp.concatenate([logf_t, k_t, _silu_from_half(proj(C_QB)), proj(C_IB), proj(C_QC)], axis=1)
    seq_scr[...] = jnp.zeros_like(seq_scr)
    for c in range(seq_scr.shape[0]):
        for t in range(seq):
            seq_scr[c, pl.ds(t, Bb, stride=S), :] = slab(per_seq_in, t)[:, c * LANES:(c + 1) * LANES]
    cols = lambda lo, hi: jnp.concatenate([seq_scr[c] for c in range(lo // LANES, hi // LANES)], axis=1)
    logf = cols(0, HG_WIDTH)
    k = cols(HG_WIDTH, 2 * HG_WIDTH)
    q = cols(2 * HG_WIDTH, 3 * HG_WIDTH)
    v = cols(3 * HG_WIDTH, 3 * HG_WIDTH + HG_VWIDTH)
    qc = cols(3 * HG_WIDTH + HG_VWIDTH, 3 * HG_WIDTH + HG_VWIDTH + MEM_WIDTH)
    to3 = lambda z: z.reshape(Bb, S, z.shape[-1])
    b3 = to3(_select_mm(_as_bf16(_chunk_tri(R, S)), logf))
    bm = b3[:, S // 2:S // 2 + 1, :]
    bl = b3[:, S - 1:S, :]
    q3, k3, v3 = to3(q), to3(k), to3(v)
    qin = q3 * jnp.exp(b3)
    qd = q3 * jnp.exp(b3 - bm)
    kd = k3 * jnp.exp(bm - b3)
    kl = k3 * jnp.exp(bl - b3)
    rr = lax.broadcasted_iota(jnp.int32, (Bb, S, S), 1)
    cc = lax.broadcasted_iota(jnp.int32, (Bb, S, S), 2)
    causal = cc <= rr
    p1 = bl.astype(BF16).astype(F32)
    p2 = (bl - p1).astype(BF16).astype(F32)
    p3 = (bl - p1) - p2
    r3 = lax.broadcasted_iota(jnp.int32, (Bb, S, HG_WIDTH), 1)
    pieces = jnp.where(r3 == 0, p1, jnp.where(r3 == 1, p2, jnp.where(r3 == 2, p3, 0.0)))
    ones = jnp.ones((Bb, S, HG_VDIM), BF16)
    hgg = hgg_ref[...]
    heads = []
    for h in range(HG_HEADS):
        sl = slice(h * HG_KDIM, (h + 1) * HG_KDIM)
        sc = jnp.where(causal, _bmm('bqd,bkd->bqk', qd[:, :, sl], kd[:, :, sl]), 0.0)
        s0 = hg_ref[:, h]
        o = _bmm('bqk,bke->bqe', sc, v3[:, :, sl]) + _bmm('bqd,bde->bqe', qin[:, :, sl], s0)
        logdec = _bmm('bkd,bke->bde', pieces[:, :, sl], ones)
        newhg_ref[:, h] = jnp.exp(logdec) * s0 + _bmm('bkd,bke->bde', kl[:, :, sl], v3[:, :, sl])
        heads.append(_rms_head(o, hgg[:, sl]).reshape(R, HG_VDIM))

    qc3 = to3(qc)
    head_of_lane = lax.broadcasted_iota(jnp.int32, (Bb, S, MEM_WIDTH), 2) // MEM_HDIM
    q4 = jnp.concatenate([jnp.where(head_of_lane == h, qc3, 0.0) for h in range(MEM_HEADS)], axis=1)
    s = _bmm('bqd,bdm->bqm', q4, mk_ref[...]) * (MEM_HDIM ** -0.5)
    e = jnp.exp(s - jnp.max(s, axis=-1, keepdims=True))
    p = e / jnp.sum(e, axis=-1, keepdims=True)
    o4 = _bmm('bqm,bdm->bqd', p, mv_ref[...])
    y_c = jnp.zeros((Bb, S, MEM_WIDTH), F32)
    for h in range(MEM_HEADS):
        y_c = y_c + jnp.where(head_of_lane == h, o4[:, h * S:(h + 1) * S, :], 0.0)

    staged = jnp.concatenate(heads + [y_c.reshape(R, MEM_WIDTH)], axis=1)
    for c in range(out_scr.shape[0]):
        out_scr[c] = staged[:, c * LANES:(c + 1) * LANES]
    per_seq_out = jnp.concatenate(
        [jnp.concatenate([out_scr[c, pl.ds(t, Bb, stride=S), :] for c in range(out_scr.shape[0])], axis=1)
         for t in range(seq)], axis=0)
    y_b = per_seq_out[:, :HG_VWIDTH] * _silu_from_half(proj(C_GB))
    y_c = per_seq_out[:, HG_VWIDTH:]
    h = _merge_out(x, y_a, y_b, y_c, proj(C_GATE), wbp_ref[...], wbh_ref[...], wbm_ref[...], wout_ref[...],
                   g_ref[...], b_ref[...], alpha)
    h_ref[...] = h.reshape(seq, Bb, D_MODEL)


def _mixer_sample(x, mk, mv, pool_t, hg, lb_logits, w, layer, pos0, alpha):
    Bb, S = SAMPLE_BB, SUBLANES
    bsz, seq, _ = x.shape
    n_mem = mk.shape[2]
    b3 = lambda i: (i, 0, 0)
    tb3 = lambda i: (0, i, 0)
    weights = [lb_logits, w['w_in'], w['bd_pool'], w['pool_scale'], w['hg_norm_g'], w['w_br_pool'], w['w_br_hg'],
               w['w_br_mem'], w['w_out'], w['ln1_g'], w['ln1_b']]
    per_seq_in = 3 * HG_WIDTH + HG_VWIDTH + MEM_WIDTH
    per_seq_out = HG_VWIDTH + MEM_WIDTH
    return pl.pallas_call(
        functools.partial(_mixer_sample_kernel, layer=layer, pos0=pos0, alpha=alpha, seq=seq),
        grid=(bsz // Bb,),
        in_specs=[pl.BlockSpec((Bb, seq, D_MODEL), b3),
                  pl.BlockSpec((Bb, MEM_WIDTH, n_mem), b3),
                  pl.BlockSpec((Bb, MEM_WIDTH, n_mem), b3),
                  pl.BlockSpec((POOL_BUF, Bb, POOL_WIDTH), tb3),
                  pl.BlockSpec((Bb, HG_HEADS, HG_KDIM, HG_VDIM), lambda i: (i, 0, 0, 0))]
                 + [_const_spec(a.shape) for a in weights],
        out_specs=[pl.BlockSpec((seq, Bb, D_MODEL), tb3),
                   pl.BlockSpec((POOL_BUF, Bb, POOL_WIDTH), tb3),
                   pl.BlockSpec((Bb, HG_HEADS, HG_KDIM, HG_VDIM), lambda i: (i, 0, 0, 0))],
        out_shape=[jax.ShapeDtypeStruct((seq, bsz, D_MODEL), F32),
                   jax.ShapeDtypeStruct((POOL_BUF, bsz, POOL_WIDTH), F32),
                   jax.ShapeDtypeStruct((bsz, HG_HEADS, HG_KDIM, HG_VDIM), F32)],
        scratch_shapes=[pltpu.VMEM((per_seq_in // LANES, Bb * S, LANES), F32),
                        pltpu.VMEM((per_seq_out // LANES, Bb * S, LANES), F32)],
        compiler_params=pltpu.CompilerParams(dimension_semantics=("arbitrary",), vmem_limit_bytes=VMEM_LIMIT),
        name="mixer_sample",
    )(x, mk, mv, pool_t, hg, *weights)


def _ffn_sample_kernel(h_ref, conv_ref, wg_ref, wu_ref, cw_ref, cb_ref, wd_ref, g_ref, b_ref,
                       y_ref, newconv_ref, *, alpha, seq):
    bsz = conv_ref.shape[0]
    h = h_ref[...]
    hb = h.astype(BF16)
    a = jnp.dot(hb, wg_ref[...], preferred_element_type=F32)
    u = jnp.dot(hb, wu_ref[...], preferred_element_type=F32)
    rows = [conv_ref[:, i, :] for i in range(CONV_W - 1)] + [a[t * bsz:(t + 1) * bsz] for t in range(seq)]
    cw = cw_ref[...]
    c = jnp.concatenate([cb_ref[...] + sum(rows[t + i] * cw[i:i + 1] for i in range(CONV_W)) for t in range(seq)],
                        axis=0)

    def store(first_row, val):
        for t in range(val.shape[0] // bsz):
            y_ref[:, first_row // bsz + t, :] = val[t * bsz:(t + 1) * bsz]

    _project_norm(store, h, _gelu(c) * u, wd_ref[...], g_ref[...], b_ref[...], alpha,
                  group_rows=bsz * max(1, NORM_ROWS // bsz))
    for i in range(CONV_W - 1):
        newconv_ref[:, i, :] = rows[seq + i]


def _ffn_sample(h_t2d, conv, w, alpha):
    n_rows = h_t2d.shape[0]
    bsz = conv.shape[0]
    weights = [w['w_gate'], w['w_up'], w['conv_w'], w['conv_b'], w['w_down'], w['ln2_g'], w['ln2_b']]
    full = lambda shape: pl.BlockSpec(shape, lambda i: (0,) * len(shape))
    y_shape = (bsz, n_rows // bsz, D_MODEL)
    return pl.pallas_call(
        functools.partial(_ffn_sample_kernel, alpha=alpha, seq=n_rows // bsz),
        grid=(1,),
        in_specs=[full(h_t2d.shape), full(conv.shape)] + [_const_spec(a.shape) for a in weights],
        out_specs=[full(y_shape), full(conv.shape)],
        out_shape=[jax.ShapeDtypeStruct(y_shape, F32), jax.ShapeDtypeStruct(conv.shape, F32)],
        compiler_params=pltpu.CompilerParams(dimension_semantics=("arbitrary",), vmem_limit_bytes=VMEM_LIMIT),
        name="ffn_sample",
    )(h_t2d, conv, *weights)


def _block_diag(w_grp):
    groups, gdim, _ = w_grp.shape
    out = jnp.zeros((groups * gdim, groups * gdim), w_grp.dtype)
    for g in range(groups):
        out = lax.dynamic_update_slice(out, w_grp[g], (g * gdim, g * gdim))
    return out


def kernel(x_prompt, x_sample, state_pool, state_hgrn, state_ffn_conv, cache_mem_k, cache_mem_v, mem_prompt, lb_logits, w_in, w_pool_grp, pool_scale, hg_norm_g, w_mem_k, w_mem_v, w_br_pool, w_br_hg, w_br_mem, w_out, ln1_g, ln1_b, w_gate, w_up, conv_w, conv_b, w_down, ln2_g, ln2_b):
    depth = w_in.shape[0]
    alpha = (2 * depth) ** 0.25
    n_prompt, seq_p, _ = x_prompt.shape
    n_sample, seq_s, _ = x_sample.shape
    n_mem = mem_prompt.shape[1]
    assert seq_p % PROMPT_T == 0 and seq_p % FFN_T == 0 and seq_s <= SUBLANES and n_sample % SAMPLE_BB == 0

    hp = x_prompt
    time_major = lambda a: jnp.transpose(a, (1, 0, 2))
    hs = x_sample
    mem2d = mem_prompt.reshape(n_prompt * n_mem, D_MODEL)
    row = lambda a: a.reshape(1, -1)
    col = jnp.arange(w_in.shape[-1])
    in_range = lambda c: (col >= c[0]) & (col < c[1])
    half_cols = jnp.where(in_range(C_QB) | in_range(C_FB) | in_range(C_GB) | in_range(C_GATE), 0.5, 1.0).astype(F32)
    outs = [[] for _ in range(8)]
    for l in range(depth):
        w = {'bd_pool': _block_diag(w_pool_grp[l]).astype(BF16),
             'pool_scale': row(pool_scale[l]), 'hg_norm_g': row(hg_norm_g[l]),
             'ln1_g': row(ln1_g[l]), 'ln1_b': row(ln1_b[l]),
             'conv_w': conv_w[l], 'conv_b': row(conv_b[l]), 'ln2_g': row(ln2_g[l]), 'ln2_b': row(ln2_b[l])}
        mkt, mvt, (w['w_in'], w['w_out'], w['w_br_pool'], w['w_br_hg'], w['w_br_mem']) = _memkv(
            mem2d, w_mem_k[l], w_mem_v[l], n_mem,
            [(w_in[l], half_cols), (w_out[l], jnp.full((D_MODEL,), 0.5, F32)), (w_br_pool[l], None),
             (w_br_hg[l], None), (w_br_mem[l], None)])
        (h_mid, pool_p, hg_p), (w['w_gate'], w['w_up'], w['w_down']) = _mixer_prompt(
            hp, mkt, mvt, lb_logits, w, l, 0, alpha, (w_gate[l], w_up[l], w_down[l]))
        hp, conv_p = _ffn_prompt(h_mid, w, alpha)
        outs[0].append(pool_p[:, CARRY_ROWS - POOL_BUF:])
        outs[1].append(hg_p)
        outs[2].append(conv_p[:, CONV_CARRY - (CONV_W - 1):])
        outs[3].append(_token_major(mkt))
        outs[4].append(_token_major(mvt))
        hs_mid, pool_s, hg_s = _mixer_sample(
            hs, _feature_major(cache_mem_k[l]), _feature_major(cache_mem_v[l]),
            time_major(state_pool[l]), state_hgrn[l], lb_logits, w, l, PAST_LEN, alpha)
        hs, conv_s = _ffn_sample(hs_mid.reshape(seq_s * n_sample, D_MODEL), state_ffn_conv[l], w, alpha)
        outs[5].append(time_major(pool_s))
        outs[6].append(hg_s)
        outs[7].append(conv_s)
    return (hp, hs) + tuple(jnp.stack(o) for o in outs)
```

```python
import functools

import jax
import jax.numpy as jnp
from jax import lax
from jax.experimental import pallas as pl
from jax.experimental.pallas import tpu as pltpu

F32 = jnp.float32
BF16 = jnp.bfloat16

D_MODEL = 1024
POOL_WIDTH = 256
POOL_WINDOWS = (2, 4, 8, 16)
POOL_GDIM = 64
POOL_BUF = 15
HG_HEADS = 4
HG_KDIM = 128
HG_WIDTH = 512
HG_VDIM = 128
HG_VWIDTH = 512
MEM_HEADS = 4
MEM_HDIM = 64
MEM_WIDTH = 256
N_BRANCH = 3
D_FF = 2816
CONV_W = 3
LN_EPS = 1e-5
RMS_EPS = 1e-6

C_UA = (0, 256)
C_QB = (256, 768)
C_FB = (768, 1280)
C_IB = (1280, 1792)
C_GB = (1792, 2304)
C_QC = (2304, 2560)
C_GATE = (2560, 5632)

LANES = 128
SUBLANES = 8
BF16_TILE_ROWS = 16
CARRY_ROWS = 16
CONV_CARRY = 8
PROMPT_T = 512
FFN_T = 1024
NORM_ROWS = 256
HG_BLOCK = 256
HG_CHUNK = 64
HG_SAFE = 16
MEMKV_SEQS = 2
SAMPLE_BB = 16
V7X_VMEM_BYTES = 64 * 1024 * 1024
VMEM_LIMIT = V7X_VMEM_BYTES - 8 * 1024 * 1024
PAST_LEN = 16384


def _mm(a, b):
    return jnp.dot(a.astype(BF16), b.astype(BF16), preferred_element_type=F32)


def _mm_nt(a, b):
    return lax.dot_general(a.astype(BF16), b.astype(BF16), (((1,), (1,)), ((), ())), preferred_element_type=F32)


def _mm_tn(a, b):
    return lax.dot_general(a.astype(BF16), b.astype(BF16), (((0,), (0,)), ((), ())), preferred_element_type=F32)


def _split3(x):
    h1 = x.astype(BF16)
    r1 = x - h1.astype(F32)
    h2 = r1.astype(BF16)
    h3 = (r1 - h2.astype(F32)).astype(BF16)
    return h1, h2, h3


def _select_mm(sel, x):
    h1, h2, h3 = _split3(x)
    d = lambda p: jnp.dot(sel, p, preferred_element_type=F32)
    return d(h1) + d(h2) + d(h3)


def _silu_from_half(p):
    return p + p * jnp.tanh(p)


def _gated(p, m):
    return m + jnp.tanh(p) * m


def _layer_norm(x, g, b):
    mu = jnp.mean(x, axis=-1, keepdims=True)
    xc = x - mu
    var = jnp.mean(xc * xc, axis=-1, keepdims=True)
    return xc * lax.rsqrt(var + LN_EPS) * g + b


def _gelu(x):
    return 0.5 * x * (1.0 + lax.erf(x * (2.0 ** -0.5)))


def _lower_bound(lb_logits, layer):
    m = jnp.max(lb_logits, axis=0, keepdims=True)
    e = jnp.exp(lb_logits - m)
    sm = e / jnp.sum(e, axis=0, keepdims=True)
    return jnp.sum(sm[:layer + 1], axis=0, keepdims=True)


def _pool_diff(full, u_a, pos, tail):
    s2 = full + pltpu.roll(full, 1, 0)
    s4 = s2 + pltpu.roll(s2, 2, 0)
    s8 = s4 + pltpu.roll(s4, 4, 0)
    s16 = s8 + pltpu.roll(s8, 8, 0)
    grp = lax.broadcasted_iota(jnp.int32, u_a.shape, 1) // POOL_GDIM
    wsum = jnp.where(grp == 0, tail(s2), jnp.where(grp == 1, tail(s4), jnp.where(grp == 2, tail(s8), tail(s16))))
    wlen = jnp.where(grp == 0, 2, jnp.where(grp == 1, 4, jnp.where(grp == 2, 8, 16)))
    count = jnp.minimum(pos + 1, wlen).astype(F32)
    return wsum / count - u_a


def _forget_gates(fb_half, lb):
    th = jnp.tanh(fb_half)
    f = lb + (1.0 - lb) * (0.5 + 0.5 * th)
    k = (1.0 - lb) * (0.5 - 0.5 * th)
    return jnp.log(f), k


def _chunk_tri(n, chunk):
    r = lax.broadcasted_iota(jnp.int32, (n, n), 0)
    c = lax.broadcasted_iota(jnp.int32, (n, n), 1)
    return (r // chunk == c // chunk) & (c <= r)


def _as_bf16(mask):
    return jnp.where(mask, 1.0, 0.0).astype(BF16)


def _rms_head(o, g):
    return o * lax.rsqrt(jnp.mean(o * o, axis=-1, keepdims=True) + RMS_EPS) * g


def _project_norm(store, resid, lhs, w, g, b, alpha, group_rows=NORM_ROWS):
    n = resid.shape[0]
    step = min(n, group_rows)
    for i in range(0, n, step):
        store(i, _layer_norm(alpha * resid[i:i + step] + _mm(lhs[i:i + step], w), g, b))


def _rows_of(ref2d):
    def store(first_row, val):
        ref2d[first_row:first_row + val.shape[0], :] = val
    return store


def _merge_out(x, y_a, y_b, y_c, gate_pre, wbp, wbh, wbm, wout, g, b, alpha):
    merged2 = (_gated(gate_pre[:, 0:D_MODEL], _mm(y_a, wbp))
               + _gated(gate_pre[:, D_MODEL:2 * D_MODEL], _mm(y_b, wbh))
               + _gated(gate_pre[:, 2 * D_MODEL:3 * D_MODEL], _mm(y_c, wbm)))
    return _layer_norm(alpha * x + _mm(merged2, wout), g, b)


def _cast_block_count(rows, n_steps):
    return max(n for n in range(1, n_steps + 1) if rows % n == 0 and (rows // n) % BF16_TILE_ROWS == 0)


def _ride_along_casts(step, srcs, scales, dsts, block_counts):
    for src, scale, dst, n_blocks in zip(srcs, scales, dsts, block_counts):
        @pl.when(step < n_blocks)
        def _(src=src, scale=scale, dst=dst):
            val = src[...] if scale is None else src[...] * scale[...]
            dst[...] = val.astype(BF16)


def _memkv_kernel(mem_ref, wk_ref, wv_ref, *rest, cast_blocks, scaled):
    n_cast = len(cast_blocks)
    cast_in, scale_refs = rest[:n_cast], list(rest[n_cast:n_cast + sum(scaled)])
    kt_ref, vt_ref = rest[n_cast + sum(scaled):n_cast + sum(scaled) + 2]
    cast_out = rest[n_cast + sum(scaled) + 2:]
    n_mem = kt_ref.shape[2]
    for s in range(kt_ref.shape[0]):
        mt = mem_ref[s * n_mem:(s + 1) * n_mem, :].T
        kt_ref[s] = _mm_tn(wk_ref[...], mt)
        vt_ref[s] = _mm_tn(wv_ref[...], mt)
    scales = [scale_refs.pop(0) if s else None for s in scaled]
    _ride_along_casts(pl.program_id(0), cast_in, scales, cast_out, cast_blocks)


def _memkv(mem2d, wk, wv, n_mem, to_cast):
    n_seq = mem2d.shape[0] // n_mem
    per_step = MEMKV_SEQS if n_seq % MEMKV_SEQS == 0 else 1
    n_steps = n_seq // per_step
    const = lambda i: (0, 0)
    per_seq = lambda i: (i, 0, 0)
    arrays = [a for a, _ in to_cast]
    scale_rows = [s.reshape(1, -1) for _, s in to_cast if s is not None]
    cast_blocks = tuple(_cast_block_count(a.shape[0], n_steps) for a in arrays)
    cast_specs = [pl.BlockSpec((a.shape[0] // n, a.shape[1]), lambda i, n=n: (jnp.minimum(i, n - 1), 0))
                  for a, n in zip(arrays, cast_blocks)]
    outs = pl.pallas_call(
        functools.partial(_memkv_kernel, cast_blocks=cast_blocks, scaled=tuple(s is not None for _, s in to_cast)),
        grid=(n_steps,),
        in_specs=[pl.BlockSpec((per_step * n_mem, D_MODEL), lambda i: (i, 0)),
                  pl.BlockSpec((D_MODEL, MEM_WIDTH), const),
                  pl.BlockSpec((D_MODEL, MEM_WIDTH), const)]
                 + cast_specs + [pl.BlockSpec(s.shape, const) for s in scale_rows],
        out_specs=[pl.BlockSpec((per_step, MEM_WIDTH, n_mem), per_seq),
                   pl.BlockSpec((per_step, MEM_WIDTH, n_mem), per_seq)] + cast_specs,
        out_shape=[jax.ShapeDtypeStruct((n_seq, MEM_WIDTH, n_mem), F32)] * 2
                  + [jax.ShapeDtypeStruct(a.shape, BF16) for a in arrays],
        name="memkv",
    )(mem2d, wk, wv, *arrays, *scale_rows)
    return outs[0], outs[1], outs[2:]


def _feature_major(mem):
    return jnp.transpose(mem, (0, 2, 3, 1)).reshape(mem.shape[0], MEM_WIDTH, mem.shape[1])


def _token_major(mem_t):
    bsz, _, n_mem = mem_t.shape
    return jnp.transpose(mem_t.reshape(bsz, MEM_HEADS, MEM_HDIM, n_mem), (0, 3, 1, 2))


def _mixer_prompt_kernel(x_ref, mk_ref, mv_ref, lb_ref, win_ref, bd_ref, pscale_ref, hgg_ref,
                         wbp_ref, wbh_ref, wbm_ref, wout_ref, g_ref, b_ref, *rest, layer, pos0, alpha, cast_blocks):
    n_cast = len(cast_blocks)
    cast_in, (h_ref, newpool_ref, newhg_ref) = rest[:n_cast], rest[n_cast:n_cast + 3]
    cast_out, (st_scr, pool_scr) = rest[n_cast + 3:2 * n_cast + 3], rest[2 * n_cast + 3:]
    T, C = PROMPT_T, HG_CHUNK
    j = pl.program_id(1)

    _ride_along_casts(pl.program_id(0) * pl.num_programs(1) + j, cast_in, [None] * n_cast, cast_out, cast_blocks)

    @pl.when(j == 0)
    def _():
        pool_scr[...] = jnp.zeros_like(pool_scr)
        st_scr[...] = jnp.zeros_like(st_scr)

    x = x_ref[0]
    xb = x.astype(BF16)
    proj = lambda c: jnp.dot(xb, win_ref[:, c[0]:c[1]], preferred_element_type=F32)

    lb = _lower_bound(lb_ref[...], layer)
    logf, k = _forget_gates(proj(C_FB), lb)
    qb = proj(C_QB)
    v = proj(C_IB)
    tri = _as_bf16(_chunk_tri(HG_BLOCK, C))
    blocks = [slice(i * HG_BLOCK, (i + 1) * HG_BLOCK) for i in range(T // HG_BLOCK)]
    bcum = jnp.concatenate([_select_mm(tri, logf[bs]) for bs in blocks], axis=0)
    u_a = proj(C_UA)
    qc = proj(C_QC)
    gb = proj(C_GB)
    gate_a = proj((C_GATE[0], C_GATE[0] + D_MODEL))

    full = jnp.concatenate([pool_scr[...], u_a], axis=0)
    pos = pos0 + j * T + lax.broadcasted_iota(jnp.int32, (T, POOL_WIDTH), 0)
    diff_a = _pool_diff(full, u_a, pos, lambda z: z[CARRY_ROWS:])
    pool_scr[...] = full[T:]

    mk = mk_ref[0].astype(BF16)
    mv = mv_ref[0].astype(BF16)
    head_of_lane = lax.broadcasted_iota(jnp.int32, (T, MEM_WIDTH), 1) // MEM_HDIM
    att = [_mm(jnp.where(head_of_lane == h, qc, 0.0), mk) * (MEM_HDIM ** -0.5) for h in range(MEM_HEADS)]
    gate_b = proj((C_GATE[0] + D_MODEL, C_GATE[0] + 2 * D_MODEL))

    q = _silu_from_half(qb)
    n_chunks = T // C
    chunk_rows = [slice(c * C, (c + 1) * C) for c in range(n_chunks)]

    def group_ref(group, pick):
        return jnp.concatenate([jnp.broadcast_to(bcum[pick(g):pick(g) + 1], (group, HG_WIDTH))
                                for g in range(T // group)], axis=0)

    rr = lax.broadcasted_iota(jnp.int32, (HG_BLOCK, HG_BLOCK), 0)
    cc = lax.broadcasted_iota(jnp.int32, (HG_BLOCK, HG_BLOCK), 1)
    mid = group_ref(HG_SAFE, lambda g: g * HG_SAFE + HG_SAFE // 2)
    levels = [(q * jnp.exp(bcum - mid), k * jnp.exp(mid - bcum), (rr // HG_SAFE == cc // HG_SAFE) & (cc <= rr))]
    size = HG_SAFE
    while size < C:
        end_prev = group_ref(size, lambda g, size=size: max(g * size - 1, 0))
        end_own = group_ref(size, lambda g, size=size: (g + 1) * size - 1)
        later_sibling = (rr // (2 * size) == cc // (2 * size)) & ((rr // size) % 2 == 1) & ((cc // size) % 2 == 0)
        levels.append((q * jnp.exp(jnp.minimum(bcum - end_prev, 0.0)), k * jnp.exp(jnp.minimum(end_own - bcum, 0.0)),
                       later_sibling))
        size *= 2
    qin = q * jnp.exp(bcum)
    kl = k * jnp.exp(group_ref(C, lambda g: (g + 1) * C - 1) - bcum)
    dec = [jnp.exp(bcum[(c + 1) * C - 1:(c + 1) * C]) for c in range(n_chunks)]
    hgg = hgg_ref[...]
    head_lanes = [slice(h * HG_KDIM, (h + 1) * HG_KDIM) for h in range(HG_HEADS)]

    def chunk_scores(bs, sl):
        out = jnp.zeros((HG_BLOCK, HG_BLOCK), F32)
        for qf, kf, mask in levels:
            out = jnp.where(mask, _mm_nt(qf[bs, sl], kf[bs, sl]), out)
        return out

    sc = [[chunk_scores(bs, sl) for bs in blocks] for sl in head_lanes]
    grow = [[_mm_tn(v[rs, sl], kl[rs, sl]) for rs in chunk_rows] for sl in head_lanes]
    y_a = _mm(diff_a, bd_ref[...]) * pscale_ref[...]
    gate_c = proj((C_GATE[0] + 2 * D_MODEL, C_GATE[1]))

    y_c = jnp.zeros((T, MEM_WIDTH), F32)
    inv_sum = jnp.zeros((T, MEM_WIDTH), F32)
    for h in range(MEM_HEADS):
        e = jnp.exp(att[h] - jnp.max(att[h], axis=-1, keepdims=True))
        y_c = y_c + jnp.where(head_of_lane == h, _mm_nt(e, mv), 0.0)
        inv_sum = inv_sum + jnp.where(head_of_lane == h, 1.0 / jnp.sum(e, axis=-1, keepdims=True), 0.0)
    y_c = y_c * inv_sum
    m_a = _mm(y_a, wbp_ref[...])

    heads = []
    for h, sl in enumerate(head_lanes):
        o_intra = jnp.concatenate([_mm(sc[h][i], v[bs, sl]) for i, bs in enumerate(blocks)], axis=0)
        st = st_scr[h]
        o_inter = []
        for c, rs in enumerate(chunk_rows):
            o_inter.append(_mm_nt(qin[rs, sl], st))
            st = st * dec[c][:, sl] + grow[h][c]
        st_scr[h] = st
        heads.append(_rms_head(o_intra + jnp.concatenate(o_inter, axis=0), hgg[:, sl]))
    m_c = _mm(y_c, wbm_ref[...])
    y_b = jnp.concatenate(heads, axis=1) * _silu_from_half(gb)
    merged2 = _gated(gate_a, m_a) + _gated(gate_c, m_c) + _gated(gate_b, _mm(y_b, wbh_ref[...]))
    _project_norm(_rows_of(h_ref.at[0]), x, merged2, wout_ref[...], g_ref[...], b_ref[...], alpha)

    @pl.when(j == pl.num_programs(1) - 1)
    def _():
        newpool_ref[0] = pool_scr[...]
        for h in range(HG_HEADS):
            newhg_ref[0, h] = st_scr[h].T


def _const_spec(shape):
    n = len(shape)
    return pl.BlockSpec(shape, lambda *_: (0,) * n, pipeline_mode=pl.Buffered(1))


def _mixer_prompt(x, mk, mv, lb_logits, w, layer, pos0, alpha, to_cast):
    bsz, seq, _ = x.shape
    T = PROMPT_T
    n_j = seq // T
    per_b3 = lambda b, j: (b, 0, 0)
    cast_blocks = tuple(_cast_block_count(a.shape[0], bsz * n_j) for a in to_cast)
    cast_specs = [pl.BlockSpec((a.shape[0] // n, a.shape[1]), lambda b, j, n=n: (jnp.minimum(b * n_j + j, n - 1), 0))
                  for a, n in zip(to_cast, cast_blocks)]
    kern = functools.partial(_mixer_prompt_kernel, layer=layer, pos0=pos0, alpha=alpha, cast_blocks=cast_blocks)
    weights = [lb_logits, w['w_in'], w['bd_pool'], w['pool_scale'], w['hg_norm_g'], w['w_br_pool'], w['w_br_hg'],
               w['w_br_mem'], w['w_out'], w['ln1_g'], w['ln1_b']]
    outs = pl.pallas_call(
        kern,
        grid=(bsz, n_j),
        in_specs=[pl.BlockSpec((1, T, D_MODEL), lambda b, j: (b, j, 0)),
                  pl.BlockSpec((1,) + mk.shape[1:], per_b3),
                  pl.BlockSpec((1,) + mv.shape[1:], per_b3)]
                 + [_const_spec(a.shape) for a in weights] + cast_specs,
        out_specs=[pl.BlockSpec((1, T, D_MODEL), lambda b, j: (b, j, 0)),
                   pl.BlockSpec((1, CARRY_ROWS, POOL_WIDTH), per_b3),
                   pl.BlockSpec((1, HG_HEADS, HG_KDIM, HG_VDIM), lambda b, j: (b, 0, 0, 0))] + cast_specs,
        out_shape=[jax.ShapeDtypeStruct((bsz, seq, D_MODEL), F32),
                   jax.ShapeDtypeStruct((bsz, CARRY_ROWS, POOL_WIDTH), F32),
                   jax.ShapeDtypeStruct((bsz, HG_HEADS, HG_KDIM, HG_VDIM), F32)]
                  + [jax.ShapeDtypeStruct(a.shape, BF16) for a in to_cast],
        scratch_shapes=[pltpu.VMEM((HG_HEADS, HG_VDIM, HG_KDIM), F32),
                        pltpu.VMEM((CARRY_ROWS, POOL_WIDTH), F32)],
        compiler_params=pltpu.CompilerParams(dimension_semantics=("arbitrary", "arbitrary"),
                                             vmem_limit_bytes=VMEM_LIMIT),
        name="mixer_prompt",
    )(x, mk, mv, *weights, *to_cast)
    return outs[:3], outs[3:]


def _conv_gate(full, tail, u, cw, cb):
    c = cb + pltpu.roll(full, 2, 0) * cw[0:1] + pltpu.roll(full, 1, 0) * cw[1:2] + full * cw[2:3]
    return _gelu(tail(c)) * u


def _ffn_prompt_kernel(h_ref, wg_ref, wu_ref, cw_ref, cb_ref, wd_ref, g_ref, b_ref,
                       y_ref, newconv_ref, carry_scr, *, alpha):
    T = FFN_T
    j = pl.program_id(1)

    @pl.when(j == 0)
    def _():
        carry_scr[...] = jnp.zeros_like(carry_scr)

    h = h_ref[0]
    hb = h.astype(BF16)
    a = jnp.dot(hb, wg_ref[...], preferred_element_type=F32)
    u = jnp.dot(hb, wu_ref[...], preferred_element_type=F32)
    full = jnp.concatenate([carry_scr[...], a], axis=0)
    gated = _conv_gate(full, lambda z: z[CONV_CARRY:], u, cw_ref[...], cb_ref[...])
    _project_norm(_rows_of(y_ref.at[0]), h, gated, wd_ref[...], g_ref[...], b_ref[...], alpha)
    carry_scr[...] = a[T - CONV_CARRY:]

    @pl.when(j == pl.num_programs(1) - 1)
    def _():
        newconv_ref[0] = carry_scr[...]


def _ffn_prompt(h, w, alpha):
    bsz, seq, _ = h.shape
    T = FFN_T
    weights = [w['w_gate'], w['w_up'], w['conv_w'], w['conv_b'], w['w_down'], w['ln2_g'], w['ln2_b']]
    return pl.pallas_call(
        functools.partial(_ffn_prompt_kernel, alpha=alpha),
        grid=(bsz, seq // T),
        in_specs=[pl.BlockSpec((1, T, D_MODEL), lambda b, j: (b, j, 0))]
                 + [_const_spec(a.shape) for a in weights],
        out_specs=[pl.BlockSpec((1, T, D_MODEL), lambda b, j: (b, j, 0)),
                   pl.BlockSpec((1, CONV_CARRY, D_FF), lambda b, j: (b, 0, 0))],
        out_shape=[jax.ShapeDtypeStruct((bsz, seq, D_MODEL), F32),
                   jax.ShapeDtypeStruct((bsz, CONV_CARRY, D_FF), F32)],
        scratch_shapes=[pltpu.VMEM((CONV_CARRY, D_FF), F32)],
        compiler_params=pltpu.CompilerParams(dimension_semantics=("arbitrary", "arbitrary"),
                                             vmem_limit_bytes=VMEM_LIMIT),
        name="ffn_prompt",
    )(h, *weights)


def _bmm(spec, a, b):
    return jnp.einsum(spec, a.astype(BF16), b.astype(BF16), preferred_element_type=F32)


def _mixer_sample_kernel(x_ref, mk_ref, mv_ref, pool_ref, hg_ref, lb_ref, win_ref, bd_ref, pscale_ref, hgg_ref,
                         wbp_ref, wbh_ref, wbm_ref, wout_ref, g_ref, b_ref,
                         h_ref, newpool_ref, newhg_ref, seq_scr, out_scr, *, layer, pos0, alpha, seq):
    Bb, S = SAMPLE_BB, SUBLANES
    R = Bb * S
    M = seq * Bb
    slab = lambda z, t: z[t * Bb:(t + 1) * Bb]
    x = jnp.concatenate([x_ref[:, t, :] for t in range(seq)], axis=0)
    xb = x.astype(BF16)
    proj = lambda c: jnp.dot(xb, win_ref[:, c[0]:c[1]], preferred_element_type=F32)

    u_a = proj(C_UA)
    rows = [pool_ref[i] for i in range(POOL_BUF)] + [slab(u_a, t) for t in range(seq)]
    n_rows = len(rows)
    sums = {1: dict(enumerate(rows))}
    for w in POOL_WINDOWS:
        half = sums[w // 2]
        sums[w] = {i: half[i] + half[i - w // 2] for i in range(n_rows) if i - w // 2 in half and i in half}
    grp = lax.broadcasted_iota(jnp.int32, (Bb, POOL_WIDTH), 1) // POOL_GDIM
    diffs = []
    for t in range(seq):
        i = POOL_BUF + t
        pooled = [sums[w][i] / float(min(pos0 + t + 1, w)) for w in POOL_WINDOWS]
        mean = jnp.where(grp == 0, pooled[0], jnp.where(grp == 1, pooled[1], jnp.where(grp == 2, pooled[2], pooled[3])))
        diffs.append(mean - rows[i])
    y_a = _mm(jnp.concatenate(diffs, axis=0), bd_ref[...]) * pscale_ref[...]
    for i in range(POOL_BUF):
        newpool_ref[i] = rows[n_rows - POOL_BUF + i]

    lb = _lower_bound(lb_ref[...], layer)
    logf_t, k_t = _forget_gates(proj(C_FB), lb)
    per_seq_in = jnp.concatenate([logf_t, k_t, _silu_from_half(proj(C_QB)), proj(C_IB), proj(C_QC)], axis=1)
    seq_scr[...] = jnp.zeros_like(seq_scr)
    for c in range(seq_scr.shape[0]):
        for t in range(seq):
            seq_scr[c, pl.ds(t, Bb, stride=S), :] = slab(per_seq_in, t)[:, c * LANES:(c + 1) * LANES]
    cols = lambda lo, hi: jnp.concatenate([seq_scr[c] for c in range(lo // LANES, hi // LANES)], axis=1)
    logf = cols(0, HG_WIDTH)
    k = cols(HG_WIDTH, 2 * HG_WIDTH)
    q = cols(2 * HG_WIDTH, 3 * HG_WIDTH)
    v = cols(3 * HG_WIDTH, 3 * HG_WIDTH + HG_VWIDTH)
    qc = cols(3 * HG_WIDTH + HG_VWIDTH, 3 * HG_WIDTH + HG_VWIDTH + MEM_WIDTH)
    to3 = lambda z: z.reshape(Bb, S, z.shape[-1])
    b3 = to3(_select_mm(_as_bf16(_chunk_tri(R, S)), logf))
    bm = b3[:, S // 2:S // 2 + 1, :]
    bl = b3[:, S - 1:S, :]
    q3, k3, v3 = to3(q), to3(k), to3(v)
    qin = q3 * jnp.exp(b3)
    qd = q3 * jnp.exp(b3 - bm)
    kd = k3 * jnp.exp(bm - b3)
    kl = k3 * jnp.exp(bl - b3)
    rr = lax.broadcasted_iota(jnp.int32, (Bb, S, S), 1)
    cc = lax.broadcasted_iota(jnp.int32, (Bb, S, S), 2)
    causal = cc <= rr
    p1 = bl.astype(BF16).astype(F32)
    p2 = (bl - p1).astype(BF16).astype(F32)
    p3 = (bl - p1) - p2
    r3 = lax.broadcasted_iota(jnp.int32, (Bb, S, HG_WIDTH), 1)
    pieces = jnp.where(r3 == 0, p1, jnp.where(r3 == 1, p2, jnp.where(r3 == 2, p3, 0.0)))
    ones = jnp.ones((Bb, S, HG_VDIM), BF16)
    hgg = hgg_ref[...]
    heads = []
    for h in range(HG_HEADS):
        sl = slice(h * HG_KDIM, (h + 1) * HG_KDIM)
        sc = jnp.where(causal, _bmm('bqd,bkd->bqk', qd[:, :, sl], kd[:, :, sl]), 0.0)
        s0 = hg_ref[:, h]
        o = _bmm('bqk,bke->bqe', sc, v3[:, :, sl]) + _bmm('bqd,bde->bqe', qin[:, :, sl], s0)
        logdec = _bmm('bkd,bke->bde', pieces[:, :, sl], ones)
        newhg_ref[:, h] = jnp.exp(logdec) * s0 + _bmm('bkd,bke->bde', kl[:, :, sl], v3[:, :, sl])
        heads.append(_rms_head(o, hgg[:, sl]).reshape(R, HG_VDIM))

    qc3 = to3(qc)
    head_of_lane = lax.broadcasted_iota(jnp.int32, (Bb, S, MEM_WIDTH), 2) // MEM_HDIM
    q4 = jnp.concatenate([jnp.where(head_of_lane == h, qc3, 0.0) for h in range(MEM_HEADS)], axis=1)
    s = _bmm('bqd,bdm->bqm', q4, mk_ref[...]) * (MEM_HDIM ** -0.5)
    e = jnp.exp(s - jnp.max(s, axis=-1, keepdims=True))
    p = e / jnp.sum(e, axis=-1, keepdims=True)
    o4 = _bmm('bqm,bdm->bqd', p, mv_ref[...])
    y_c = jnp.zeros((Bb, S, MEM_WIDTH), F32)
    for h in range(MEM_HEADS):
        y_c = y_c + jnp.where(head_of_lane == h, o4[:, h * S:(h + 1) * S, :], 0.0)

    staged = jnp.concatenate(heads + [y_c.reshape(R, MEM_WIDTH)], axis=1)
    for c in range(out_scr.shape[0]):
        out_scr[c] = staged[:, c * LANES:(c + 1) * LANES]
    per_seq_out = jnp.concatenate(
        [jnp.concatenate([out_scr[c, pl.ds(t, Bb, stride=S), :] for c in range(out_scr.shape[0])], axis=1)
         for t in range(seq)], axis=0)
    y_b = per_seq_out[:, :HG_VWIDTH] * _silu_from_half(proj(C_GB))
    y_c = per_seq_out[:, HG_VWIDTH:]
    h = _merge_out(x, y_a, y_b, y_c, proj(C_GATE), wbp_ref[...], wbh_ref[...], wbm_ref[...], wout_ref[...],
                   g_ref[...], b_ref[...], alpha)
    h_ref[...] = h.reshape(seq, Bb, D_MODEL)


def _mixer_sample(x, mk, mv, pool_t, hg, lb_logits, w, layer, pos0, alpha):
    Bb, S = SAMPLE_BB, SUBLANES
    bsz, seq, _ = x.shape
    n_mem = mk.shape[2]
    b3 = lambda i: (i, 0, 0)
    tb3 = lambda i: (0, i, 0)
    weights = [lb_logits, w['w_in'], w['bd_pool'], w['pool_scale'], w['hg_norm_g'], w['w_br_pool'], w['w_br_hg'],
               w['w_br_mem'], w['w_out'], w['ln1_g'], w['ln1_b']]
    per_seq_in = 3 * HG_WIDTH + HG_VWIDTH + MEM_WIDTH
    per_seq_out = HG_VWIDTH + MEM_WIDTH
    return pl.pallas_call(
        functools.partial(_mixer_sample_kernel, layer=layer, pos0=pos0, alpha=alpha, seq=seq),
        grid=(bsz // Bb,),
        in_specs=[pl.BlockSpec((Bb, seq, D_MODEL), b3),
                  pl.BlockSpec((Bb, MEM_WIDTH, n_mem), b3),
                  pl.BlockSpec((Bb, MEM_WIDTH, n_mem), b3),
                  pl.BlockSpec((POOL_BUF, Bb, POOL_WIDTH), tb3),
                  pl.BlockSpec((Bb, HG_HEADS, HG_KDIM, HG_VDIM), lambda i: (i, 0, 0, 0))]
                 + [_const_spec(a.shape) for a in weights],
        out_specs=[pl.BlockSpec((seq, Bb, D_MODEL), tb3),
                   pl.BlockSpec((POOL_BUF, Bb, POOL_WIDTH), tb3),
                   pl.BlockSpec((Bb, HG_HEADS, HG_KDIM, HG_VDIM), lambda i: (i, 0, 0, 0))],
        out_shape=[jax.ShapeDtypeStruct((seq, bsz, D_MODEL), F32),
                   jax.ShapeDtypeStruct((POOL_BUF, bsz, POOL_WIDTH), F32),
                   jax.ShapeDtypeStruct((bsz, HG_HEADS, HG_KDIM, HG_VDIM), F32)],
        scratch_shapes=[pltpu.VMEM((per_seq_in // LANES, Bb * S, LANES), F32),
                        pltpu.VMEM((per_seq_out // LANES, Bb * S, LANES), F32)],
        compiler_params=pltpu.CompilerParams(dimension_semantics=("arbitrary",), vmem_limit_bytes=VMEM_LIMIT),
        name="mixer_sample",
    )(x, mk, mv, pool_t, hg, *weights)


def _ffn_sample_kernel(h_ref, conv_ref, wg_ref, wu_ref, cw_ref, cb_ref, wd_ref, g_ref, b_ref,
                       y_ref, newconv_ref, *, alpha, seq):
    bsz = conv_ref.shape[0]
    h = h_ref[...]
    hb = h.astype(BF16)
    a = jnp.dot(hb, wg_ref[...], preferred_element_type=F32)
    u = jnp.dot(hb, wu_ref[...], preferred_element_type=F32)
    rows = [conv_ref[:, i, :] for i in range(CONV_W - 1)] + [a[t * bsz:(t + 1) * bsz] for t in range(seq)]
    cw = cw_ref[...]
    c = jnp.concatenate([cb_ref[...] + sum(rows[t + i] * cw[i:i + 1] for i in range(CONV_W)) for t in range(seq)],
                        axis=0)

    def store(first_row, val):
        for t in range(val.shape[0] // bsz):
            y_ref[:, first_row // bsz + t, :] = val[t * bsz:(t + 1) * bsz]

    _project_norm(store, h, _gelu(c) * u, wd_ref[...], g_ref[...], b_ref[...], alpha,
                  group_rows=bsz * max(1, NORM_ROWS // bsz))
    for i in range(CONV_W - 1):
        newconv_ref[:, i, :] = rows[seq + i]


def _ffn_sample(h_t2d, conv, w, alpha):
    n_rows = h_t2d.shape[0]
    bsz = conv.shape[0]
    weights = [w['w_gate'], w['w_up'], w['conv_w'], w['conv_b'], w['w_down'], w['ln2_g'], w['ln2_b']]
    full = lambda shape: pl.BlockSpec(shape, lambda i: (0,) * len(shape))
    y_shape = (bsz, n_rows // bsz, D_MODEL)
    return pl.pallas_call(
        functools.partial(_ffn_sample_kernel, alpha=alpha, seq=n_rows // bsz),
        grid=(1,),
        in_specs=[full(h_t2d.shape), full(conv.shape)] + [_const_spec(a.shape) for a in weights],
        out_specs=[full(y_shape), full(conv.shape)],
        out_shape=[jax.ShapeDtypeStruct(y_shape, F32), jax.ShapeDtypeStruct(conv.shape, F32)],
        compiler_params=pltpu.CompilerParams(dimension_semantics=("arbitrary",), vmem_limit_bytes=VMEM_LIMIT),
        name="ffn_sample",
    )(h_t2d, conv, *weights)


def _block_diag(w_grp):
    groups, gdim, _ = w_grp.shape
    out = jnp.zeros((groups * gdim, groups * gdim), w_grp.dtype)
    for g in range(groups):
        out = lax.dynamic_update_slice(out, w_grp[g], (g * gdim, g * gdim))
    return out


def kernel(x_prompt, x_sample, state_pool, state_hgrn, state_ffn_conv, cache_mem_k, cache_mem_v, mem_prompt, lb_logits, w_in, w_pool_grp, pool_scale, hg_norm_g, w_mem_k, w_mem_v, w_br_pool, w_br_hg, w_br_mem, w_out, ln1_g, ln1_b, w_gate, w_up, conv_w, conv_b, w_down, ln2_g, ln2_b):
    depth = w_in.shape[0]
    alpha = (2 * depth) ** 0.25
    n_prompt, seq_p, _ = x_prompt.shape
    n_sample, seq_s, _ = x_sample.shape
    n_mem = mem_prompt.shape[1]
    assert seq_p % PROMPT_T == 0 and seq_p % FFN_T == 0 and seq_s <= SUBLANES and n_sample % SAMPLE_BB == 0

    hp = x_prompt
    time_major = lambda a: jnp.transpose(a, (1, 0, 2))
    hs = x_sample
    mem2d = mem_prompt.reshape(n_prompt * n_mem, D_MODEL)
    row = lambda a: a.reshape(1, -1)
    col = jnp.arange(w_in.shape[-1])
    in_range = lambda c: (col >= c[0]) & (col < c[1])
    half_cols = jnp.where(in_range(C_QB) | in_range(C_FB) | in_range(C_GB) | in_range(C_GATE), 0.5, 1.0).astype(F32)
    outs = [[] for _ in range(8)]
    for l in range(depth):
        w = {'bd_pool': _block_diag(w_pool_grp[l]).astype(BF16),
             'pool_scale': row(pool_scale[l]), 'hg_norm_g': row(hg_norm_g[l]),
             'ln1_g': row(ln1_g[l]), 'ln1_b': row(ln1_b[l]),
             'conv_w': conv_w[l], 'conv_b': row(conv_b[l]), 'ln2_g': row(ln2_g[l]), 'ln2_b': row(ln2_b[l])}
        mkt, mvt, (w['w_in'], w['w_out'], w['w_br_pool'], w['w_br_hg'], w['w_br_mem']) = _memkv(
            mem2d, w_mem_k[l], w_mem_v[l], n_mem,
            [(w_in[l], half_cols), (w_out[l], jnp.full((D_MODEL,), 0.5, F32)), (w_br_pool[l], None),
             (w_br_hg[l], None), (w_br_mem[l], None)])
        (h_mid, pool_p, hg_p), (w['w_gate'], w['w_up'], w['w_down']) = _mixer_prompt(
            hp, mkt, mvt, lb_logits, w, l, 0, alpha, (w_gate[l], w_up[l], w_down[l]))
        hp, conv_p = _ffn_prompt(h_mid, w, alpha)
        outs[0].append(pool_p[:, CARRY_ROWS - POOL_BUF:])
        outs[1].append(hg_p)
        outs[2].append(conv_p[:, CONV_CARRY - (CONV_W - 1):])
        outs[3].append(_token_major(mkt))
        outs[4].append(_token_major(mvt))
        hs_mid, pool_s, hg_s = _mixer_sample(
            hs, _feature_major(cache_mem_k[l]), _feature_major(cache_mem_v[l]),
            time_major(state_pool[l]), state_hgrn[l], lb_logits, w, l, PAST_LEN, alpha)
        hs, conv_s = _ffn_sample(hs_mid.reshape(seq_s * n_sample, D_MODEL), state_ffn_conv[l], w, alpha)
        outs[5].append(time_major(pool_s))
        outs[6].append(hg_s)
        outs[7].append(conv_s)
    return (hp, hs) + tuple(jnp.stack(o) for o in outs)
```

```python
import functools

import jax
import jax.numpy as jnp
from jax import lax
from jax.experimental import pallas as pl
from jax.experimental.pallas import tpu as pltpu

F32 = jnp.float32
BF16 = jnp.bfloat16

D_MODEL = 1024
POOL_WIDTH = 256
POOL_WINDOWS = (2, 4, 8, 16)
POOL_GDIM = 64
POOL_BUF = 15
HG_HEADS = 4
HG_KDIM = 128
HG_WIDTH = 512
HG_VDIM = 128
HG_VWIDTH = 512
MEM_HEADS = 4
MEM_HDIM = 64
MEM_WIDTH = 256
N_BRANCH = 3
D_FF = 2816
CONV_W = 3
LN_EPS = 1e-5
RMS_EPS = 1e-6

C_UA = (0, 256)
C_QB = (256, 768)
C_FB = (768, 1280)
C_IB = (1280, 1792)
C_GB = (1792, 2304)
C_QC = (2304, 2560)
C_GATE = (2560, 5632)

LANES = 128
SUBLANES = 8
BF16_TILE_ROWS = 16
CARRY_ROWS = 16
CONV_CARRY = 8
PROMPT_T = 512
FFN_T = 1024
NORM_ROWS = 256
HG_BLOCK = 256
HG_CHUNK = 128
HG_SAFE = 16
MEMKV_SEQS = 2
SAMPLE_BB = 16
V7X_VMEM_BYTES = 64 * 1024 * 1024
VMEM_LIMIT = V7X_VMEM_BYTES - 8 * 1024 * 1024
PAST_LEN = 16384


def _mm(a, b):
    return jnp.dot(a.astype(BF16), b.astype(BF16), preferred_element_type=F32)


def _mm_nt(a, b):
    return lax.dot_general(a.astype(BF16), b.astype(BF16), (((1,), (1,)), ((), ())), preferred_element_type=F32)


def _mm_tn(a, b):
    return lax.dot_general(a.astype(BF16), b.astype(BF16), (((0,), (0,)), ((), ())), preferred_element_type=F32)


def _split3(x):
    h1 = x.astype(BF16)
    r1 = x - h1.astype(F32)
    h2 = r1.astype(BF16)
    h3 = (r1 - h2.astype(F32)).astype(BF16)
    return h1, h2, h3


def _select_mm(sel, x):
    h1, h2, h3 = _split3(x)
    d = lambda p: jnp.dot(sel, p, preferred_element_type=F32)
    return d(h1) + d(h2) + d(h3)


def _silu_from_half(p):
    return p + p * jnp.tanh(p)


def _gated(p, m):
    return m + jnp.tanh(p) * m


def _layer_norm(x, g, b):
    mu = jnp.mean(x, axis=-1, keepdims=True)
    xc = x - mu
    var = jnp.mean(xc * xc, axis=-1, keepdims=True)
    return xc * lax.rsqrt(var + LN_EPS) * g + b


def _gelu(x):
    return 0.5 * x * (1.0 + lax.erf(x * (2.0 ** -0.5)))


def _lower_bound(lb_logits, layer):
    m = jnp.max(lb_logits, axis=0, keepdims=True)
    e = jnp.exp(lb_logits - m)
    sm = e / jnp.sum(e, axis=0, keepdims=True)
    return jnp.sum(sm[:layer + 1], axis=0, keepdims=True)


def _pool_diff(full, u_a, pos, tail):
    s2 = full + pltpu.roll(full, 1, 0)
    s4 = s2 + pltpu.roll(s2, 2, 0)
    s8 = s4 + pltpu.roll(s4, 4, 0)
    s16 = s8 + pltpu.roll(s8, 8, 0)
    grp = lax.broadcasted_iota(jnp.int32, u_a.shape, 1) // POOL_GDIM
    wsum = jnp.where(grp == 0, tail(s2), jnp.where(grp == 1, tail(s4), jnp.where(grp == 2, tail(s8), tail(s16))))
    wlen = jnp.where(grp == 0, 2, jnp.where(grp == 1, 4, jnp.where(grp == 2, 8, 16)))
    count = jnp.minimum(pos + 1, wlen).astype(F32)
    return wsum / count - u_a


def _forget_gates(fb_half, lb):
    th = jnp.tanh(fb_half)
    f = lb + (1.0 - lb) * (0.5 + 0.5 * th)
    k = (1.0 - lb) * (0.5 - 0.5 * th)
    return jnp.log(f), k


def _chunk_tri(n, chunk):
    r = lax.broadcasted_iota(jnp.int32, (n, n), 0)
    c = lax.broadcasted_iota(jnp.int32, (n, n), 1)
    return (r // chunk == c // chunk) & (c <= r)


def _as_bf16(mask):
    return jnp.where(mask, 1.0, 0.0).astype(BF16)


def _rms_head(o, g):
    return o * lax.rsqrt(jnp.mean(o * o, axis=-1, keepdims=True) + RMS_EPS) * g


def _project_norm(store, resid, lhs, w, g, b, alpha, group_rows=NORM_ROWS):
    n = resid.shape[0]
    step = min(n, group_rows)
    for i in range(0, n, step):
        store(i, _layer_norm(alpha * resid[i:i + step] + _mm(lhs[i:i + step], w), g, b))


def _rows_of(ref2d):
    def store(first_row, val):
        ref2d[first_row:first_row + val.shape[0], :] = val
    return store


def _merge_out(x, y_a, y_b, y_c, gate_pre, wbp, wbh, wbm, wout, g, b, alpha):
    merged2 = (_gated(gate_pre[:, 0:D_MODEL], _mm(y_a, wbp))
               + _gated(gate_pre[:, D_MODEL:2 * D_MODEL], _mm(y_b, wbh))
               + _gated(gate_pre[:, 2 * D_MODEL:3 * D_MODEL], _mm(y_c, wbm)))
    return _layer_norm(alpha * x + _mm(merged2, wout), g, b)


def _cast_block_count(rows, n_steps):
    return max(n for n in range(1, n_steps + 1) if rows % n == 0 and (rows // n) % BF16_TILE_ROWS == 0)


def _ride_along_casts(step, srcs, scales, dsts, block_counts):
    for src, scale, dst, n_blocks in zip(srcs, scales, dsts, block_counts):
        @pl.when(step < n_blocks)
        def _(src=src, scale=scale, dst=dst):
            val = src[...] if scale is None else src[...] * scale[...]
            dst[...] = val.astype(BF16)


def _memkv_kernel(mem_ref, wk_ref, wv_ref, *rest, cast_blocks, scaled):
    n_cast = len(cast_blocks)
    cast_in, scale_refs = rest[:n_cast], list(rest[n_cast:n_cast + sum(scaled)])
    kt_ref, vt_ref = rest[n_cast + sum(scaled):n_cast + sum(scaled) + 2]
    cast_out = rest[n_cast + sum(scaled) + 2:]
    n_mem = kt_ref.shape[2]
    for s in range(kt_ref.shape[0]):
        mt = mem_ref[s * n_mem:(s + 1) * n_mem, :].T
        kt_ref[s] = _mm_tn(wk_ref[...], mt)
        vt_ref[s] = _mm_tn(wv_ref[...], mt)
    scales = [scale_refs.pop(0) if s else None for s in scaled]
    _ride_along_casts(pl.program_id(0), cast_in, scales, cast_out, cast_blocks)


def _memkv(mem2d, wk, wv, n_mem, to_cast):
    n_seq = mem2d.shape[0] // n_mem
    per_step = MEMKV_SEQS if n_seq % MEMKV_SEQS == 0 else 1
    n_steps = n_seq // per_step
    const = lambda i: (0, 0)
    per_seq = lambda i: (i, 0, 0)
    arrays = [a for a, _ in to_cast]
    scale_rows = [s.reshape(1, -1) for _, s in to_cast if s is not None]
    cast_blocks = tuple(_cast_block_count(a.shape[0], n_steps) for a in arrays)
    cast_specs = [pl.BlockSpec((a.shape[0] // n, a.shape[1]), lambda i, n=n: (jnp.minimum(i, n - 1), 0))
                  for a, n in zip(arrays, cast_blocks)]
    outs = pl.pallas_call(
        functools.partial(_memkv_kernel, cast_blocks=cast_blocks, scaled=tuple(s is not None for _, s in to_cast)),
        grid=(n_steps,),
        in_specs=[pl.BlockSpec((per_step * n_mem, D_MODEL), lambda i: (i, 0)),
                  pl.BlockSpec((D_MODEL, MEM_WIDTH), const),
                  pl.BlockSpec((D_MODEL, MEM_WIDTH), const)]
                 + cast_specs + [pl.BlockSpec(s.shape, const) for s in scale_rows],
        out_specs=[pl.BlockSpec((per_step, MEM_WIDTH, n_mem), per_seq),
                   pl.BlockSpec((per_step, MEM_WIDTH, n_mem), per_seq)] + cast_specs,
        out_shape=[jax.ShapeDtypeStruct((n_seq, MEM_WIDTH, n_mem), F32)] * 2
                  + [jax.ShapeDtypeStruct(a.shape, BF16) for a in arrays],
        name="memkv",
    )(mem2d, wk, wv, *arrays, *scale_rows)
    return outs[0], outs[1], outs[2:]


def _feature_major(mem):
    return jnp.transpose(mem, (0, 2, 3, 1)).reshape(mem.shape[0], MEM_WIDTH, mem.shape[1])


def _token_major(mem_t):
    bsz, _, n_mem = mem_t.shape
    return jnp.transpose(mem_t.reshape(bsz, MEM_HEADS, MEM_HDIM, n_mem), (0, 3, 1, 2))


def _mixer_prompt_kernel(x_ref, mk_ref, mv_ref, lb_ref, win_ref, bd_ref, pscale_ref, hgg_ref,
                         wbp_ref, wbh_ref, wbm_ref, wout_ref, g_ref, b_ref, *rest, layer, pos0, alpha, cast_blocks):
    n_cast = len(cast_blocks)
    cast_in, (h_ref, newpool_ref, newhg_ref) = rest[:n_cast], rest[n_cast:n_cast + 3]
    cast_out, (st_scr, pool_scr) = rest[n_cast + 3:2 * n_cast + 3], rest[2 * n_cast + 3:]
    T, C = PROMPT_T, HG_CHUNK
    j = pl.program_id(1)

    _ride_along_casts(pl.program_id(0) * pl.num_programs(1) + j, cast_in, [None] * n_cast, cast_out, cast_blocks)

    @pl.when(j == 0)
    def _():
        pool_scr[...] = jnp.zeros_like(pool_scr)
        st_scr[...] = jnp.zeros_like(st_scr)

    x = x_ref[0]
    xb = x.astype(BF16)
    proj = lambda c: jnp.dot(xb, win_ref[:, c[0]:c[1]], preferred_element_type=F32)

    lb = _lower_bound(lb_ref[...], layer)
    logf, k = _forget_gates(proj(C_FB), lb)
    qb = proj(C_QB)
    v = proj(C_IB)
    tri = _as_bf16(_chunk_tri(HG_BLOCK, C))
    blocks = [slice(i * HG_BLOCK, (i + 1) * HG_BLOCK) for i in range(T // HG_BLOCK)]
    bcum = jnp.concatenate([_select_mm(tri, logf[bs]) for bs in blocks], axis=0)
    u_a = proj(C_UA)
    qc = proj(C_QC)
    gb = proj(C_GB)
    gate_a = proj((C_GATE[0], C_GATE[0] + D_MODEL))

    full = jnp.concatenate([pool_scr[...], u_a], axis=0)
    pos = pos0 + j * T + lax.broadcasted_iota(jnp.int32, (T, POOL_WIDTH), 0)
    diff_a = _pool_diff(full, u_a, pos, lambda z: z[CARRY_ROWS:])
    pool_scr[...] = full[T:]

    mk = mk_ref[0].astype(BF16)
    mv = mv_ref[0].astype(BF16)
    head_of_lane = lax.broadcasted_iota(jnp.int32, (T, MEM_WIDTH), 1) // MEM_HDIM
    att = [_mm(jnp.where(head_of_lane == h, qc, 0.0), mk) * (MEM_HDIM ** -0.5) for h in range(MEM_HEADS)]
    gate_b = proj((C_GATE[0] + D_MODEL, C_GATE[0] + 2 * D_MODEL))

    q = _silu_from_half(qb)
    n_chunks = T // C
    chunk_rows = [slice(c * C, (c + 1) * C) for c in range(n_chunks)]

    def group_ref(group, pick):
        return jnp.concatenate([jnp.broadcast_to(bcum[pick(g):pick(g) + 1], (group, HG_WIDTH))
                                for g in range(T // group)], axis=0)

    rr = lax.broadcasted_iota(jnp.int32, (HG_BLOCK, HG_BLOCK), 0)
    cc = lax.broadcasted_iota(jnp.int32, (HG_BLOCK, HG_BLOCK), 1)
    mid = group_ref(HG_SAFE, lambda g: g * HG_SAFE + HG_SAFE // 2)
    q_mid, k_mid = q * jnp.exp(bcum - mid), k * jnp.exp(mid - bcum)
    same_safe_block_causal = (rr // HG_SAFE == cc // HG_SAFE) & (cc <= rr)
    later_rows = lambda x, size: jnp.concatenate(
        [x[(2 * i + 1) * size:(2 * i + 2) * size] for i in range(x.shape[0] // (2 * size))], axis=0)
    ru = lax.broadcasted_iota(jnp.int32, (HG_BLOCK // 2, HG_BLOCK), 0)
    cu = lax.broadcasted_iota(jnp.int32, (HG_BLOCK // 2, HG_BLOCK), 1)
    sibling_levels = []
    size = HG_SAFE
    while size < C:
        end_prev = jnp.concatenate([jnp.broadcast_to(bcum[(2 * i + 1) * size - 1:(2 * i + 1) * size], (size, HG_WIDTH))
                                    for i in range(T // (2 * size))], axis=0)
        end_own = group_ref(size, lambda g, size=size: (g + 1) * size - 1)
        q_later = later_rows(q, size) * jnp.exp(jnp.minimum(later_rows(bcum, size) - end_prev, 0.0))
        k_earlier = k * jnp.exp(jnp.minimum(end_own - bcum, 0.0))
        earlier_sibling = (cu // (2 * size) == ru // size) & ((cu // size) % 2 == 0)
        sibling_levels.append((size, q_later, k_earlier, earlier_sibling))
        size *= 2
    qin = q * jnp.exp(bcum)
    kl = k * jnp.exp(group_ref(C, lambda g: (g + 1) * C - 1) - bcum)
    dec = [jnp.exp(bcum[(c + 1) * C - 1:(c + 1) * C]) for c in range(n_chunks)]
    hgg = hgg_ref[...]
    head_lanes = [slice(h * HG_KDIM, (h + 1) * HG_KDIM) for h in range(HG_HEADS)]

    def chunk_scores(i, bs, sl):
        out = jnp.where(same_safe_block_causal, _mm_nt(q_mid[bs, sl], k_mid[bs, sl]), 0.0)
        half = slice(i * (HG_BLOCK // 2), (i + 1) * (HG_BLOCK // 2))
        for size, q_later, k_earlier, mask in sibling_levels:
            packed = jnp.where(mask, _mm_nt(q_later[half, sl], k_earlier[bs, sl]), 0.0)
            gap = jnp.zeros((size, HG_BLOCK), F32)
            out = out + jnp.concatenate(
                [piece for u in range(HG_BLOCK // (2 * size)) for piece in (gap, packed[u * size:(u + 1) * size])], axis=0)
        return out

    sc = [[chunk_scores(i, bs, sl) for i, bs in enumerate(blocks)] for sl in head_lanes]
    grow = [[_mm_tn(v[rs, sl], kl[rs, sl]) for rs in chunk_rows] for sl in head_lanes]
    y_a = _mm(diff_a, bd_ref[...]) * pscale_ref[...]
    gate_c = proj((C_GATE[0] + 2 * D_MODEL, C_GATE[1]))

    y_c = jnp.zeros((T, MEM_WIDTH), F32)
    inv_sum = jnp.zeros((T, MEM_WIDTH), F32)
    for h in range(MEM_HEADS):
        e = jnp.exp(att[h] - jnp.max(att[h], axis=-1, keepdims=True))
        y_c = y_c + jnp.where(head_of_lane == h, _mm_nt(e, mv), 0.0)
        inv_sum = inv_sum + jnp.where(head_of_lane == h, 1.0 / jnp.sum(e, axis=-1, keepdims=True), 0.0)
    y_c = y_c * inv_sum
    m_a = _mm(y_a, wbp_ref[...])

    heads = []
    for h, sl in enumerate(head_lanes):
        o_intra = jnp.concatenate([_mm(sc[h][i], v[bs, sl]) for i, bs in enumerate(blocks)], axis=0)
        st = st_scr[h]
        o_inter = []
        for c, rs in enumerate(chunk_rows):
            o_inter.append(_mm_nt(qin[rs, sl], st))
            st = st * dec[c][:, sl] + grow[h][c]
        st_scr[h] = st
        heads.append(_rms_head(o_intra + jnp.concatenate(o_inter, axis=0), hgg[:, sl]))
    m_c = _mm(y_c, wbm_ref[...])
    y_b = jnp.concatenate(heads, axis=1) * _silu_from_half(gb)
    merged2 = _gated(gate_a, m_a) + _gated(gate_c, m_c) + _gated(gate_b, _mm(y_b, wbh_ref[...]))
    _project_norm(_rows_of(h_ref.at[0]), x, merged2, wout_ref[...], g_ref[...], b_ref[...], alpha)

    @pl.when(j == pl.num_programs(1) - 1)
    def _():
        newpool_ref[0] = pool_scr[...]
        for h in range(HG_HEADS):
            newhg_ref[0, h] = st_scr[h].T


def _const_spec(shape):
    n = len(shape)
    return pl.BlockSpec(shape, lambda *_: (0,) * n, pipeline_mode=pl.Buffered(1))


def _mixer_prompt(x, mk, mv, lb_logits, w, layer, pos0, alpha, to_cast):
    bsz, seq, _ = x.shape
    T = PROMPT_T
    n_j = seq // T
    per_b3 = lambda b, j: (b, 0, 0)
    cast_blocks = tuple(_cast_block_count(a.shape[0], bsz * n_j) for a in to_cast)
    cast_specs = [pl.BlockSpec((a.shape[0] // n, a.shape[1]), lambda b, j, n=n: (jnp.minimum(b * n_j + j, n - 1), 0))
                  for a, n in zip(to_cast, cast_blocks)]
    kern = functools.partial(_mixer_prompt_kernel, layer=layer, pos0=pos0, alpha=alpha, cast_blocks=cast_blocks)
    weights = [lb_logits, w['w_in'], w['bd_pool'], w['pool_scale'], w['hg_norm_g'], w['w_br_pool'], w['w_br_hg'],
               w['w_br_mem'], w['w_out'], w['ln1_g'], w['ln1_b']]
    outs = pl.pallas_call(
        kern,
        grid=(bsz, n_j),
        in_specs=[pl.BlockSpec((1, T, D_MODEL), lambda b, j: (b, j, 0)),
                  pl.BlockSpec((1,) + mk.shape[1:], per_b3),
                  pl.BlockSpec((1,) + mv.shape[1:], per_b3)]
                 + [_const_spec(a.shape) for a in weights] + cast_specs,
        out_specs=[pl.BlockSpec((1, T, D_MODEL), lambda b, j: (b, j, 0)),
                   pl.BlockSpec((1, CARRY_ROWS, POOL_WIDTH), per_b3),
                   pl.BlockSpec((1, HG_HEADS, HG_KDIM, HG_VDIM), lambda b, j: (b, 0, 0, 0))] + cast_specs,
        out_shape=[jax.ShapeDtypeStruct((bsz, seq, D_MODEL), F32),
                   jax.ShapeDtypeStruct((bsz, CARRY_ROWS, POOL_WIDTH), F32),
                   jax.ShapeDtypeStruct((bsz, HG_HEADS, HG_KDIM, HG_VDIM), F32)]
                  + [jax.ShapeDtypeStruct(a.shape, BF16) for a in to_cast],
        scratch_shapes=[pltpu.VMEM((HG_HEADS, HG_VDIM, HG_KDIM), F32),
                        pltpu.VMEM((CARRY_ROWS, POOL_WIDTH), F32)],
        compiler_params=pltpu.CompilerParams(dimension_semantics=("arbitrary", "arbitrary"),
                                             vmem_limit_bytes=VMEM_LIMIT),
        name="mixer_prompt",
    )(x, mk, mv, *weights, *to_cast)
    return outs[:3], outs[3:]


def _conv_gate(full, tail, u, cw, cb):
    c = cb + pltpu.roll(full, 2, 0) * cw[0:1] + pltpu.roll(full, 1, 0) * cw[1:2] + full * cw[2:3]
    return _gelu(tail(c)) * u


def _ffn_prompt_kernel(h_ref, wg_ref, wu_ref, cw_ref, cb_ref, wd_ref, g_ref, b_ref,
                       y_ref, newconv_ref, carry_scr, *, alpha):
    T = FFN_T
    j = pl.program_id(1)

    @pl.when(j == 0)
    def _():
        carry_scr[...] = jnp.zeros_like(carry_scr)

    h = h_ref[0]
    hb = h.astype(BF16)
    a = jnp.dot(hb, wg_ref[...], preferred_element_type=F32)
    u = jnp.dot(hb, wu_ref[...], preferred_element_type=F32)
    full = jnp.concatenate([carry_scr[...], a], axis=0)
    gated = _conv_gate(full, lambda z: z[CONV_CARRY:], u, cw_ref[...], cb_ref[...])
    _project_norm(_rows_of(y_ref.at[0]), h, gated, wd_ref[...], g_ref[...], b_ref[...], alpha)
    carry_scr[...] = a[T - CONV_CARRY:]

    @pl.when(j == pl.num_programs(1) - 1)
    def _():
        newconv_ref[0] = carry_scr[...]


def _ffn_prompt(h, w, alpha):
    bsz, seq, _ = h.shape
    T = FFN_T
    weights = [w['w_gate'], w['w_up'], w['conv_w'], w['conv_b'], w['w_down'], w['ln2_g'], w['ln2_b']]
    return pl.pallas_call(
        functools.partial(_ffn_prompt_kernel, alpha=alpha),
        grid=(bsz, seq // T),
        in_specs=[pl.BlockSpec((1, T, D_MODEL), lambda b, j: (b, j, 0))]
                 + [_const_spec(a.shape) for a in weights],
        out_specs=[pl.BlockSpec((1, T, D_MODEL), lambda b, j: (b, j, 0)),
                   pl.BlockSpec((1, CONV_CARRY, D_FF), lambda b, j: (b, 0, 0))],
        out_shape=[jax.ShapeDtypeStruct((bsz, seq, D_MODEL), F32),
                   jax.ShapeDtypeStruct((bsz, CONV_CARRY, D_FF), F32)],
        scratch_shapes=[pltpu.VMEM((CONV_CARRY, D_FF), F32)],
        compiler_params=pltpu.CompilerParams(dimension_semantics=("arbitrary", "arbitrary"),
                                             vmem_limit_bytes=VMEM_LIMIT),
        name="ffn_prompt",
    )(h, *weights)


def _bmm(spec, a, b):
    return jnp.einsum(spec, a.astype(BF16), b.astype(BF16), preferred_element_type=F32)


def _mixer_sample_kernel(x_ref, mk_ref, mv_ref, pool_ref, hg_ref, lb_ref, win_ref, bd_ref, pscale_ref, hgg_ref,
                         wbp_ref, wbh_ref, wbm_ref, wout_ref, g_ref, b_ref,
                         h_ref, newpool_ref, newhg_ref, seq_scr, out_scr, *, layer, pos0, alpha, seq):
    Bb, S = SAMPLE_BB, SUBLANES
    R = Bb * S
    M = seq * Bb
    slab = lambda z, t: z[t * Bb:(t + 1) * Bb]
    x = jnp.concatenate([x_ref[:, t, :] for t in range(seq)], axis=0)
    xb = x.astype(BF16)
    proj = lambda c: jnp.dot(xb, win_ref[:, c[0]:c[1]], preferred_element_type=F32)

    u_a = proj(C_UA)
    rows = [pool_ref[i] for i in range(POOL_BUF)] + [slab(u_a, t) for t in range(seq)]
    n_rows = len(rows)
    sums = {1: dict(enumerate(rows))}
    for w in POOL_WINDOWS:
        half = sums[w // 2]
        sums[w] = {i: half[i] + half[i - w // 2] for i in range(n_rows) if i - w // 2 in half and i in half}
    grp = lax.broadcasted_iota(jnp.int32, (Bb, POOL_WIDTH), 1) // POOL_GDIM
    diffs = []
    for t in range(seq):
        i = POOL_BUF + t
        pooled = [sums[w][i] / float(min(pos0 + t + 1, w)) for w in POOL_WINDOWS]
        mean = jnp.where(grp == 0, pooled[0], jnp.where(grp == 1, pooled[1], jnp.where(grp == 2, pooled[2], pooled[3])))
        diffs.append(mean - rows[i])
    y_a = _mm(jnp.concatenate(diffs, axis=0), bd_ref[...]) * pscale_ref[...]
    for i in range(POOL_BUF):
        newpool_ref[i] = rows[n_rows - POOL_BUF + i]

    lb = _lower_bound(lb_ref[...], layer)
    logf_t, k_t = _forget_gates(proj(C_FB), lb)
    per_seq_in = jnp.concatenate([logf_t, k_t, _silu_from_half(proj(C_QB)), proj(C_IB), proj(C_QC)], axis=1)
    seq_scr[...] = jnp.zeros_like(seq_scr)
    for c in range(seq_scr.shape[0]):
        for t in range(seq):
            seq_scr[c, pl.ds(t, Bb, stride=S), :] = slab(per_seq_in, t)[:, c * LANES:(c + 1) * LANES]
    cols = lambda lo, hi: jnp.concatenate([seq_scr[c] for c in range(lo // LANES, hi // LANES)], axis=1)
    logf = cols(0, HG_WIDTH)
    k = cols(HG_WIDTH, 2 * HG_WIDTH)
    q = cols(2 * HG_WIDTH, 3 * HG_WIDTH)
    v = cols(3 * HG_WIDTH, 3 * HG_WIDTH + HG_VWIDTH)
    qc = cols(3 * HG_WIDTH + HG_VWIDTH, 3 * HG_WIDTH + HG_VWIDTH + MEM_WIDTH)
    to3 = lambda z: z.reshape(Bb, S, z.shape[-1])
    b3 = to3(_select_mm(_as_bf16(_chunk_tri(R, S)), logf))
    bm = b3[:, S // 2:S // 2 + 1, :]
    bl = b3[:, S - 1:S, :]
    q3, k3, v3 = to3(q), to3(k), to3(v)
    qin = q3 * jnp.exp(b3)
    qd = q3 * jnp.exp(b3 - bm)
    kd = k3 * jnp.exp(bm - b3)
    kl = k3 * jnp.exp(bl - b3)
    rr = lax.broadcasted_iota(jnp.int32, (Bb, S, S), 1)
    cc = lax.broadcasted_iota(jnp.int32, (Bb, S, S), 2)
    causal = cc <= rr
    p1 = bl.astype(BF16).astype(F32)
    p2 = (bl - p1).astype(BF16).astype(F32)
    p3 = (bl - p1) - p2
    r3 = lax.broadcasted_iota(jnp.int32, (Bb, S, HG_WIDTH), 1)
    pieces = jnp.where(r3 == 0, p1, jnp.where(r3 == 1, p2, jnp.where(r3 == 2, p3, 0.0)))
    ones = jnp.ones((Bb, S, HG_VDIM), BF16)
    hgg = hgg_ref[...]
    heads = []
    for h in range(HG_HEADS):
        sl = slice(h * HG_KDIM, (h + 1) * HG_KDIM)
        sc = jnp.where(causal, _bmm('bqd,bkd->bqk', qd[:, :, sl], kd[:, :, sl]), 0.0)
        s0 = hg_ref[:, h]
        o = _bmm('bqk,bke->bqe', sc, v3[:, :, sl]) + _bmm('bqd,bde->bqe', qin[:, :, sl], s0)
        logdec = _bmm('bkd,bke->bde', pieces[:, :, sl], ones)
        newhg_ref[:, h] = jnp.exp(logdec) * s0 + _bmm('bkd,bke->bde', kl[:, :, sl], v3[:, :, sl])
        heads.append(_rms_head(o, hgg[:, sl]).reshape(R, HG_VDIM))

    qc3 = to3(qc)
    head_of_lane = lax.broadcasted_iota(jnp.int32, (Bb, S, MEM_WIDTH), 2) // MEM_HDIM
    q4 = jnp.concatenate([jnp.where(head_of_lane == h, qc3, 0.0) for h in range(MEM_HEADS)], axis=1)
    s = _bmm('bqd,bdm->bqm', q4, mk_ref[...]) * (MEM_HDIM ** -0.5)
    e = jnp.exp(s - jnp.max(s, axis=-1, keepdims=True))
    p = e / jnp.sum(e, axis=-1, keepdims=True)
    o4 = _bmm('bqm,bdm->bqd', p, mv_ref[...])
    y_c = jnp.zeros((Bb, S, MEM_WIDTH), F32)
    for h in range(MEM_HEADS):
        y_c = y_c + jnp.where(head_of_lane == h, o4[:, h * S:(h + 1) * S, :], 0.0)

    staged = jnp.concatenate(heads + [y_c.reshape(R, MEM_WIDTH)], axis=1)
    for c in range(out_scr.shape[0]):
        out_scr[c] = staged[:, c * LANES:(c + 1) * LANES]
    per_seq_out = jnp.concatenate(
        [jnp.concatenate([out_scr[c, pl.ds(t, Bb, stride=S), :] for c in range(out_scr.shape[0])], axis=1)
         for t in range(seq)], axis=0)
    y_b = per_seq_out[:, :HG_VWIDTH] * _silu_from_half(proj(C_GB))
    y_c = per_seq_out[:, HG_VWIDTH:]
    h = _merge_out(x, y_a, y_b, y_c, proj(C_GATE), wbp_ref[...], wbh_ref[...], wbm_ref[...], wout_ref[...],
                   g_ref[...], b_ref[...], alpha)
    h_ref[...] = h.reshape(seq, Bb, D_MODEL)


def _mixer_sample(x, mk, mv, pool_t, hg, lb_logits, w, layer, pos0, alpha):
    Bb, S = SAMPLE_BB, SUBLANES
    bsz, seq, _ = x.shape
    n_mem = mk.shape[2]
    b3 = lambda i: (i, 0, 0)
    tb3 = lambda i: (0, i, 0)
    weights = [lb_logits, w['w_in'], w['bd_pool'], w['pool_scale'], w['hg_norm_g'], w['w_br_pool'], w['w_br_hg'],
               w['w_br_mem'], w['w_out'], w['ln1_g'], w['ln1_b']]
    per_seq_in = 3 * HG_WIDTH + HG_VWIDTH + MEM_WIDTH
    per_seq_out = HG_VWIDTH + MEM_WIDTH
    return pl.pallas_call(
        functools.partial(_mixer_sample_kernel, layer=layer, pos0=pos0, alpha=alpha, seq=seq),
        grid=(bsz // Bb,),
        in_specs=[pl.BlockSpec((Bb, seq, D_MODEL), b3),
                  pl.BlockSpec((Bb, MEM_WIDTH, n_mem), b3),
                  pl.BlockSpec((Bb, MEM_WIDTH, n_mem), b3),
                  pl.BlockSpec((POOL_BUF, Bb, POOL_WIDTH), tb3),
                  pl.BlockSpec((Bb, HG_HEADS, HG_KDIM, HG_VDIM), lambda i: (i, 0, 0, 0))]
                 + [_const_spec(a.shape) for a in weights],
        out_specs=[pl.BlockSpec((seq, Bb, D_MODEL), tb3),
                   pl.BlockSpec((POOL_BUF, Bb, POOL_WIDTH), tb3),
                   pl.BlockSpec((Bb, HG_HEADS, HG_KDIM, HG_VDIM), lambda i: (i, 0, 0, 0))],
        out_shape=[jax.ShapeDtypeStruct((seq, bsz, D_MODEL), F32),
                   jax.ShapeDtypeStruct((POOL_BUF, bsz, POOL_WIDTH), F32),
                   jax.ShapeDtypeStruct((bsz, HG_HEADS, HG_KDIM, HG_VDIM), F32)],
        scratch_shapes=[pltpu.VMEM((per_seq_in // LANES, Bb * S, LANES), F32),
                        pltpu.VMEM((per_seq_out // LANES, Bb * S, LANES), F32)],
        compiler_params=pltpu.CompilerParams(dimension_semantics=("arbitrary",), vmem_limit_bytes=VMEM_LIMIT),
        name="mixer_sample",
    )(x, mk, mv, pool_t, hg, *weights)


def _ffn_sample_kernel(h_ref, conv_ref, wg_ref, wu_ref, cw_ref, cb_ref, wd_ref, g_ref, b_ref,
                       y_ref, newconv_ref, *, alpha, seq):
    bsz = conv_ref.shape[0]
    h = h_ref[...]
    hb = h.astype(BF16)
    a = jnp.dot(hb, wg_ref[...], preferred_element_type=F32)
    u = jnp.dot(hb, wu_ref[...], preferred_element_type=F32)
    rows = [conv_ref[:, i, :] for i in range(CONV_W - 1)] + [a[t * bsz:(t + 1) * bsz] for t in range(seq)]
    cw = cw_ref[...]
    c = jnp.concatenate([cb_ref[...] + sum(rows[t + i] * cw[i:i + 1] for i in range(CONV_W)) for t in range(seq)],
                        axis=0)

    def store(first_row, val):
        for t in range(val.shape[0] // bsz):
            y_ref[:, first_row // bsz + t, :] = val[t * bsz:(t + 1) * bsz]

    _project_norm(store, h, _gelu(c) * u, wd_ref[...], g_ref[...], b_ref[...], alpha,
                  group_rows=bsz * max(1, NORM_ROWS // bsz))
    for i in range(CONV_W - 1):
        newconv_ref[:, i, :] = rows[seq + i]


def _ffn_sample(h_t2d, conv, w, alpha):
    n_rows = h_t2d.shape[0]
    bsz = conv.shape[0]
    weights = [w['w_gate'], w['w_up'], w['conv_w'], w['conv_b'], w['w_down'], w['ln2_g'], w['ln2_b']]
    full = lambda shape: pl.BlockSpec(shape, lambda i: (0,) * len(shape))
    y_shape = (bsz, n_rows // bsz, D_MODEL)
    return pl.pallas_call(
        functools.partial(_ffn_sample_kernel, alpha=alpha, seq=n_rows // bsz),
        grid=(1,),
        in_specs=[full(h_t2d.shape), full(conv.shape)] + [_const_spec(a.shape) for a in weights],
        out_specs=[full(y_shape), full(conv.shape)],
        out_shape=[jax.ShapeDtypeStruct(y_shape, F32), jax.ShapeDtypeStruct(conv.shape, F32)],
        compiler_params=pltpu.CompilerParams(dimension_semantics=("arbitrary",), vmem_limit_bytes=VMEM_LIMIT),
        name="ffn_sample",
    )(h_t2d, conv, *weights)


def _block_diag(w_grp):
    groups, gdim, _ = w_grp.shape
    out = jnp.zeros((groups * gdim, groups * gdim), w_grp.dtype)
    for g in range(groups):
        out = lax.dynamic_update_slice(out, w_grp[g], (g * gdim, g * gdim))
    return out


def kernel(x_prompt, x_sample, state_pool, state_hgrn, state_ffn_conv, cache_mem_k, cache_mem_v, mem_prompt, lb_logits, w_in, w_pool_grp, pool_scale, hg_norm_g, w_mem_k, w_mem_v, w_br_pool, w_br_hg, w_br_mem, w_out, ln1_g, ln1_b, w_gate, w_up, conv_w, conv_b, w_down, ln2_g, ln2_b):
    depth = w_in.shape[0]
    alpha = (2 * depth) ** 0.25
    n_prompt, seq_p, _ = x_prompt.shape
    n_sample, seq_s, _ = x_sample.shape
    n_mem = mem_prompt.shape[1]
    assert seq_p % PROMPT_T == 0 and seq_p % FFN_T == 0 and seq_s <= SUBLANES and n_sample % SAMPLE_BB == 0

    hp = x_prompt
    time_major = lambda a: jnp.transpose(a, (1, 0, 2))
    hs = x_sample
    mem2d = mem_prompt.reshape(n_prompt * n_mem, D_MODEL)
    row = lambda a: a.reshape(1, -1)
    col = jnp.arange(w_in.shape[-1])
    in_range = lambda c: (col >= c[0]) & (col < c[1])
    half_cols = jnp.where(in_range(C_QB) | in_range(C_FB) | in_range(C_GB) | in_range(C_GATE), 0.5, 1.0).astype(F32)
    outs = [[] for _ in range(8)]
    for l in range(depth):
        w = {'bd_pool': _block_diag(w_pool_grp[l]).astype(BF16),
             'pool_scale': row(pool_scale[l]), 'hg_norm_g': row(hg_norm_g[l]),
             'ln1_g': row(ln1_g[l]), 'ln1_b': row(ln1_b[l]),
             'conv_w': conv_w[l], 'conv_b': row(conv_b[l]), 'ln2_g': row(ln2_g[l]), 'ln2_b': row(ln2_b[l])}
        mkt, mvt, (w['w_in'], w['w_out'], w['w_br_pool'], w['w_br_hg'], w['w_br_mem']) = _memkv(
            mem2d, w_mem_k[l], w_mem_v[l], n_mem,
            [(w_in[l], half_cols), (w_out[l], jnp.full((D_MODEL,), 0.5, F32)), (w_br_pool[l], None),
             (w_br_hg[l], None), (w_br_mem[l], None)])
        (h_mid, pool_p, hg_p), (w['w_gate'], w['w_up'], w['w_down']) = _mixer_prompt(
            hp, mkt, mvt, lb_logits, w, l, 0, alpha, (w_gate[l], w_up[l], w_down[l]))
        hp, conv_p = _ffn_prompt(h_mid, w, alpha)
        outs[0].append(pool_p[:, CARRY_ROWS - POOL_BUF:])
        outs[1].append(hg_p)
        outs[2].append(conv_p[:, CONV_CARRY - (CONV_W - 1):])
        outs[3].append(_token_major(mkt))
        outs[4].append(_token_major(mvt))
        hs_mid, pool_s, hg_s = _mixer_sample(
            hs, _feature_major(cache_mem_k[l]), _feature_major(cache_mem_v[l]),
            time_major(state_pool[l]), state_hgrn[l], lb_logits, w, l, PAST_LEN, alpha)
        hs, conv_s = _ffn_sample(hs_mid.reshape(seq_s * n_sample, D_MODEL), state_ffn_conv[l], w, alpha)
        outs[5].append(time_major(pool_s))
        outs[6].append(hg_s)
        outs[7].append(conv_s)
    return (hp, hs) + tuple(jnp.stack(o) for o in outs)
```

```python
import functools

import jax
import jax.numpy as jnp
from jax import lax
from jax.experimental import pallas as pl
from jax.experimental.pallas import tpu as pltpu

F32 = jnp.float32
BF16 = jnp.bfloat16

D_MODEL = 1024
POOL_WIDTH = 256
POOL_WINDOWS = (2, 4, 8, 16)
POOL_GDIM = 64
POOL_BUF = 15
HG_HEADS = 4
HG_KDIM = 128
HG_WIDTH = 512
HG_VDIM = 128
HG_VWIDTH = 512
MEM_HEADS = 4
MEM_HDIM = 64
MEM_WIDTH = 256
N_BRANCH = 3
D_FF = 2816
CONV_W = 3
LN_EPS = 1e-5
RMS_EPS = 1e-6

C_UA = (0, 256)
C_QB = (256, 768)
C_FB = (768, 1280)
C_IB = (1280, 1792)
C_GB = (1792, 2304)
C_QC = (2304, 2560)
C_GATE = (2560, 5632)

LANES = 128
SUBLANES = 8
BF16_TILE_ROWS = 16
CARRY_ROWS = 16
CONV_CARRY = 8
PROMPT_T = 512
FFN_T = 1024
NORM_ROWS = 256
HG_BLOCK = 256
HG_CHUNK = 128
HG_SAFE = 16
MEMKV_SEQS = 2
SAMPLE_BB = 16
V7X_VMEM_BYTES = 64 * 1024 * 1024
VMEM_LIMIT = V7X_VMEM_BYTES - 8 * 1024 * 1024
PAST_LEN = 16384


def _mm(a, b):
    return jnp.dot(a.astype(BF16), b.astype(BF16), preferred_element_type=F32)


def _mm_nt(a, b):
    return lax.dot_general(a.astype(BF16), b.astype(BF16), (((1,), (1,)), ((), ())), preferred_element_type=F32)


def _mm_tn(a, b):
    return lax.dot_general(a.astype(BF16), b.astype(BF16), (((0,), (0,)), ((), ())), preferred_element_type=F32)


def _split3(x):
    h1 = x.astype(BF16)
    r1 = x - h1.astype(F32)
    h2 = r1.astype(BF16)
    h3 = (r1 - h2.astype(F32)).astype(BF16)
    return h1, h2, h3


def _select_mm(sel, x):
    h1, h2, h3 = _split3(x)
    d = lambda p: jnp.dot(sel, p, preferred_element_type=F32)
    return d(h1) + d(h2) + d(h3)


def _silu_from_half(p):
    return p + p * jnp.tanh(p)


def _gated(p, m):
    return m + jnp.tanh(p) * m


def _layer_norm(x, g, b):
    mu = jnp.mean(x, axis=-1, keepdims=True)
    xc = x - mu
    var = jnp.mean(xc * xc, axis=-1, keepdims=True)
    return xc * lax.rsqrt(var + LN_EPS) * g + b


def _gelu(x):
    return 0.5 * x * (1.0 + lax.erf(x * (2.0 ** -0.5)))


def _lower_bound(lb_logits, layer):
    m = jnp.max(lb_logits, axis=0, keepdims=True)
    e = jnp.exp(lb_logits - m)
    sm = e / jnp.sum(e, axis=0, keepdims=True)
    return jnp.sum(sm[:layer + 1], axis=0, keepdims=True)


def _pool_diff(full, u_a, pos, tail):
    s2 = full + pltpu.roll(full, 1, 0)
    s4 = s2 + pltpu.roll(s2, 2, 0)
    s8 = s4 + pltpu.roll(s4, 4, 0)
    s16 = s8 + pltpu.roll(s8, 8, 0)
    grp = lax.broadcasted_iota(jnp.int32, u_a.shape, 1) // POOL_GDIM
    wsum = jnp.where(grp == 0, tail(s2), jnp.where(grp == 1, tail(s4), jnp.where(grp == 2, tail(s8), tail(s16))))
    wlen = jnp.where(grp == 0, 2, jnp.where(grp == 1, 4, jnp.where(grp == 2, 8, 16)))
    count = jnp.minimum(pos + 1, wlen).astype(F32)
    return wsum / count - u_a


def _forget_gates(fb_half, lb):
    th = jnp.tanh(fb_half)
    f = lb + (1.0 - lb) * (0.5 + 0.5 * th)
    k = (1.0 - lb) * (0.5 - 0.5 * th)
    return jnp.log(f), k


def _chunk_tri(n, chunk):
    r = lax.broadcasted_iota(jnp.int32, (n, n), 0)
    c = lax.broadcasted_iota(jnp.int32, (n, n), 1)
    return (r // chunk == c // chunk) & (c <= r)


def _as_bf16(mask):
    return jnp.where(mask, 1.0, 0.0).astype(BF16)


def _rms_head(o, g):
    return o * lax.rsqrt(jnp.mean(o * o, axis=-1, keepdims=True) + RMS_EPS) * g


def _project_norm(store, resid, lhs, w, g, b, alpha, group_rows=NORM_ROWS):
    n = resid.shape[0]
    step = min(n, group_rows)
    for i in range(0, n, step):
        store(i, _layer_norm(alpha * resid[i:i + step] + _mm(lhs[i:i + step], w), g, b))


def _rows_of(ref2d):
    def store(first_row, val):
        ref2d[first_row:first_row + val.shape[0], :] = val
    return store


def _merge_out(x, y_a, y_b, y_c, gate_pre, wbp, wbh, wbm, wout, g, b, alpha):
    merged2 = (_gated(gate_pre[:, 0:D_MODEL], _mm(y_a, wbp))
               + _gated(gate_pre[:, D_MODEL:2 * D_MODEL], _mm(y_b, wbh))
               + _gated(gate_pre[:, 2 * D_MODEL:3 * D_MODEL], _mm(y_c, wbm)))
    return _layer_norm(alpha * x + _mm(merged2, wout), g, b)


def _cast_block_count(rows, n_steps):
    return max(n for n in range(1, n_steps + 1) if rows % n == 0 and (rows // n) % BF16_TILE_ROWS == 0)


def _ride_along_casts(step, srcs, scales, dsts, block_counts):
    for src, scale, dst, n_blocks in zip(srcs, scales, dsts, block_counts):
        @pl.when(step < n_blocks)
        def _(src=src, scale=scale, dst=dst):
            val = src[...] if scale is None else src[...] * scale[...]
            dst[...] = val.astype(BF16)


def _memkv_kernel(mem_ref, wk_ref, wv_ref, *rest, cast_blocks, scaled):
    n_cast = len(cast_blocks)
    cast_in, scale_refs = rest[:n_cast], list(rest[n_cast:n_cast + sum(scaled)])
    kt_ref, vt_ref = rest[n_cast + sum(scaled):n_cast + sum(scaled) + 2]
    cast_out = rest[n_cast + sum(scaled) + 2:]
    n_mem = kt_ref.shape[2]
    for s in range(kt_ref.shape[0]):
        mt = mem_ref[s * n_mem:(s + 1) * n_mem, :].T
        kt_ref[s] = _mm_tn(wk_ref[...], mt)
        vt_ref[s] = _mm_tn(wv_ref[...], mt)
    scales = [scale_refs.pop(0) if s else None for s in scaled]
    _ride_along_casts(pl.program_id(0), cast_in, scales, cast_out, cast_blocks)


def _memkv(mem2d, wk, wv, n_mem, to_cast):
    n_seq = mem2d.shape[0] // n_mem
    per_step = MEMKV_SEQS if n_seq % MEMKV_SEQS == 0 else 1
    n_steps = n_seq // per_step
    const = lambda i: (0, 0)
    per_seq = lambda i: (i, 0, 0)
    arrays = [a for a, _ in to_cast]
    scale_rows = [s.reshape(1, -1) for _, s in to_cast if s is not None]
    cast_blocks = tuple(_cast_block_count(a.shape[0], n_steps) for a in arrays)
    cast_specs = [pl.BlockSpec((a.shape[0] // n, a.shape[1]), lambda i, n=n: (jnp.minimum(i, n - 1), 0))
                  for a, n in zip(arrays, cast_blocks)]
    outs = pl.pallas_call(
        functools.partial(_memkv_kernel, cast_blocks=cast_blocks, scaled=tuple(s is not None for _, s in to_cast)),
        grid=(n_steps,),
        in_specs=[pl.BlockSpec((per_step * n_mem, D_MODEL), lambda i: (i, 0)),
                  pl.BlockSpec((D_MODEL, MEM_WIDTH), const),
                  pl.BlockSpec((D_MODEL, MEM_WIDTH), const)]
                 + cast_specs + [pl.BlockSpec(s.shape, const) for s in scale_rows],
        out_specs=[pl.BlockSpec((per_step, MEM_WIDTH, n_mem), per_seq),
                   pl.BlockSpec((per_step, MEM_WIDTH, n_mem), per_seq)] + cast_specs,
        out_shape=[jax.ShapeDtypeStruct((n_seq, MEM_WIDTH, n_mem), F32)] * 2
                  + [jax.ShapeDtypeStruct(a.shape, BF16) for a in arrays],
        name="memkv",
    )(mem2d, wk, wv, *arrays, *scale_rows)
    return outs[0], outs[1], outs[2:]


def _feature_major(mem):
    return jnp.transpose(mem, (0, 2, 3, 1)).reshape(mem.shape[0], MEM_WIDTH, mem.shape[1])


def _token_major(mem_t):
    bsz, _, n_mem = mem_t.shape
    return jnp.transpose(mem_t.reshape(bsz, MEM_HEADS, MEM_HDIM, n_mem), (0, 3, 1, 2))


def _mixer_prompt_kernel(x_ref, mk_ref, mv_ref, lb_ref, win_ref, bd_ref, pscale_ref, hgg_ref,
                         wbp_ref, wbh_ref, wbm_ref, wout_ref, g_ref, b_ref, *rest, layer, pos0, alpha, cast_blocks):
    n_cast = len(cast_blocks)
    cast_in, (h_ref, newpool_ref, newhg_ref) = rest[:n_cast], rest[n_cast:n_cast + 3]
    cast_out, (st_scr, pool_scr) = rest[n_cast + 3:2 * n_cast + 3], rest[2 * n_cast + 3:]
    T, C = PROMPT_T, HG_CHUNK
    j = pl.program_id(1)

    _ride_along_casts(pl.program_id(0) * pl.num_programs(1) + j, cast_in, [None] * n_cast, cast_out, cast_blocks)

    @pl.when(j == 0)
    def _():
        pool_scr[...] = jnp.zeros_like(pool_scr)
        st_scr[...] = jnp.zeros_like(st_scr)

    x = x_ref[0]
    xb = x.astype(BF16)
    proj = lambda c: jnp.dot(xb, win_ref[:, c[0]:c[1]], preferred_element_type=F32)

    lb = _lower_bound(lb_ref[...], layer)
    logf, k = _forget_gates(proj(C_FB), lb)
    qb = proj(C_QB)
    v = proj(C_IB)
    tri = _as_bf16(_chunk_tri(HG_BLOCK, C))
    blocks = [slice(i * HG_BLOCK, (i + 1) * HG_BLOCK) for i in range(T // HG_BLOCK)]
    bcum = jnp.concatenate([_select_mm(tri, logf[bs]) for bs in blocks], axis=0)
    u_a = proj(C_UA)
    qc = proj(C_QC)
    gb = proj(C_GB)
    gate_a = proj((C_GATE[0], C_GATE[0] + D_MODEL))

    full = jnp.concatenate([pool_scr[...], u_a], axis=0)
    pos = pos0 + j * T + lax.broadcasted_iota(jnp.int32, (T, POOL_WIDTH), 0)
    diff_a = _pool_diff(full, u_a, pos, lambda z: z[CARRY_ROWS:])
    pool_scr[...] = full[T:]

    mk = mk_ref[0].astype(BF16)
    mv = mv_ref[0].astype(BF16)
    head_of_lane = lax.broadcasted_iota(jnp.int32, (T, MEM_WIDTH), 1) // MEM_HDIM
    att = [_mm(jnp.where(head_of_lane == h, qc, 0.0), mk) * (MEM_HDIM ** -0.5) for h in range(MEM_HEADS)]
    gate_b = proj((C_GATE[0] + D_MODEL, C_GATE[0] + 2 * D_MODEL))

    q = _silu_from_half(qb)
    n_chunks = T // C
    chunk_rows = [slice(c * C, (c + 1) * C) for c in range(n_chunks)]

    def group_ref(group, pick):
        return jnp.concatenate([jnp.broadcast_to(bcum[pick(g):pick(g) + 1], (group, HG_WIDTH))
                                for g in range(T // group)], axis=0)

    rr = lax.broadcasted_iota(jnp.int32, (HG_BLOCK, HG_BLOCK), 0)
    cc = lax.broadcasted_iota(jnp.int32, (HG_BLOCK, HG_BLOCK), 1)
    mid = group_ref(HG_SAFE, lambda g: g * HG_SAFE + HG_SAFE // 2)
    q_mid, k_mid = q * jnp.exp(bcum - mid), k * jnp.exp(mid - bcum)
    same_safe_block_causal = (rr // HG_SAFE == cc // HG_SAFE) & (cc <= rr)
    later_rows = lambda x, size: jnp.concatenate(
        [x[(2 * i + 1) * size:(2 * i + 2) * size] for i in range(x.shape[0] // (2 * size))], axis=0)
    ru = lax.broadcasted_iota(jnp.int32, (HG_BLOCK // 2, HG_BLOCK), 0)
    cu = lax.broadcasted_iota(jnp.int32, (HG_BLOCK // 2, HG_BLOCK), 1)
    sibling_levels = []
    size = HG_SAFE
    while size < C:
        end_prev = jnp.concatenate([jnp.broadcast_to(bcum[(2 * i + 1) * size - 1:(2 * i + 1) * size], (size, HG_WIDTH))
                                    for i in range(T // (2 * size))], axis=0)
        end_own = group_ref(size, lambda g, size=size: (g + 1) * size - 1)
        q_later = later_rows(q, size) * jnp.exp(jnp.minimum(later_rows(bcum, size) - end_prev, 0.0))
        k_earlier = k * jnp.exp(jnp.minimum(end_own - bcum, 0.0))
        earlier_sibling = (cu // (2 * size) == ru // size) & ((cu // size) % 2 == 0)
        sibling_levels.append((size, q_later, k_earlier, earlier_sibling))
        size *= 2
    qin = q * jnp.exp(bcum)
    kl = k * jnp.exp(group_ref(C, lambda g: (g + 1) * C - 1) - bcum)
    dec = [jnp.exp(bcum[(c + 1) * C - 1:(c + 1) * C]) for c in range(n_chunks)]
    hgg = hgg_ref[...]
    head_lanes = [slice(h * HG_KDIM, (h + 1) * HG_KDIM) for h in range(HG_HEADS)]

    def chunk_scores(i, bs, sl):
        out = jnp.where(same_safe_block_causal, _mm_nt(q_mid[bs, sl], k_mid[bs, sl]), 0.0)
        half = slice(i * (HG_BLOCK // 2), (i + 1) * (HG_BLOCK // 2))
        for size, q_later, k_earlier, mask in sibling_levels:
            packed = jnp.where(mask, _mm_nt(q_later[half, sl], k_earlier[bs, sl]), 0.0)
            gap = jnp.zeros((size, HG_BLOCK), F32)
            out = out + jnp.concatenate(
                [piece for u in range(HG_BLOCK // (2 * size)) for piece in (gap, packed[u * size:(u + 1) * size])], axis=0)
        return out

    sc = [[chunk_scores(i, bs, sl) for i, bs in enumerate(blocks)] for sl in head_lanes]
    grow = [[_mm_tn(v[rs, sl], kl[rs, sl]) for rs in chunk_rows] for sl in head_lanes]
    y_a = _mm(diff_a, bd_ref[...]) * pscale_ref[...]
    gate_c = proj((C_GATE[0] + 2 * D_MODEL, C_GATE[1]))

    y_c = jnp.zeros((T, MEM_WIDTH), F32)
    inv_sum = jnp.zeros((T, MEM_WIDTH), F32)
    for h in range(MEM_HEADS):
        e = jnp.exp(att[h] - jnp.max(att[h], axis=-1, keepdims=True))
        y_c = y_c + jnp.where(head_of_lane == h, _mm_nt(e, mv), 0.0)
        inv_sum = inv_sum + jnp.where(head_of_lane == h, 1.0 / jnp.sum(e, axis=-1, keepdims=True), 0.0)
    y_c = y_c * inv_sum
    m_a = _mm(y_a, wbp_ref[...])

    heads = []
    for h, sl in enumerate(head_lanes):
        o_intra = jnp.concatenate([_mm(sc[h][i], v[bs, sl]) for i, bs in enumerate(blocks)], axis=0)
        st = st_scr[h]
        o_inter = []
        for c, rs in enumerate(chunk_rows):
            o_inter.append(_mm_nt(qin[rs, sl], st))
            st = st * dec[c][:, sl] + grow[h][c]
        st_scr[h] = st
        heads.append(_rms_head(o_intra + jnp.concatenate(o_inter, axis=0), hgg[:, sl]))
    m_c = _mm(y_c, wbm_ref[...])
    y_b = jnp.concatenate(heads, axis=1) * _silu_from_half(gb)
    merged2 = _gated(gate_a, m_a) + _gated(gate_c, m_c) + _gated(gate_b, _mm(y_b, wbh_ref[...]))
    _project_norm(_rows_of(h_ref.at[0]), x, merged2, wout_ref[...], g_ref[...], b_ref[...], alpha)

    @pl.when(j == pl.num_programs(1) - 1)
    def _():
        newpool_ref[0] = pool_scr[...]
        for h in range(HG_HEADS):
            newhg_ref[0, h] = st_scr[h].T


def _const_spec(shape):
    n = len(shape)
    return pl.BlockSpec(shape, lambda *_: (0,) * n, pipeline_mode=pl.Buffered(1))


def _mixer_prompt(x, mk, mv, lb_logits, w, layer, pos0, alpha, to_cast):
    bsz, seq, _ = x.shape
    T = PROMPT_T
    n_j = seq // T
    per_b3 = lambda b, j: (b, 0, 0)
    cast_blocks = tuple(_cast_block_count(a.shape[0], bsz * n_j) for a in to_cast)
    cast_specs = [pl.BlockSpec((a.shape[0] // n, a.shape[1]), lambda b, j, n=n: (jnp.minimum(b * n_j + j, n - 1), 0))
                  for a, n in zip(to_cast, cast_blocks)]
    kern = functools.partial(_mixer_prompt_kernel, layer=layer, pos0=pos0, alpha=alpha, cast_blocks=cast_blocks)
    weights = [lb_logits, w['w_in'], w['bd_pool'], w['pool_scale'], w['hg_norm_g'], w['w_br_pool'], w['w_br_hg'],
               w['w_br_mem'], w['w_out'], w['ln1_g'], w['ln1_b']]
    outs = pl.pallas_call(
        kern,
        grid=(bsz, n_j),
        in_specs=[pl.BlockSpec((1, T, D_MODEL), lambda b, j: (b, j, 0)),
                  pl.BlockSpec((1,) + mk.shape[1:], per_b3),
                  pl.BlockSpec((1,) + mv.shape[1:], per_b3)]
                 + [_const_spec(a.shape) for a in weights] + cast_specs,
        out_specs=[pl.BlockSpec((1, T, D_MODEL), lambda b, j: (b, j, 0)),
                   pl.BlockSpec((1, CARRY_ROWS, POOL_WIDTH), per_b3),
                   pl.BlockSpec((1, HG_HEADS, HG_KDIM, HG_VDIM), lambda b, j: (b, 0, 0, 0))] + cast_specs,
        out_shape=[jax.ShapeDtypeStruct((bsz, seq, D_MODEL), F32),
                   jax.ShapeDtypeStruct((bsz, CARRY_ROWS, POOL_WIDTH), F32),
                   jax.ShapeDtypeStruct((bsz, HG_HEADS, HG_KDIM, HG_VDIM), F32)]
                  + [jax.ShapeDtypeStruct(a.shape, BF16) for a in to_cast],
        scratch_shapes=[pltpu.VMEM((HG_HEADS, HG_VDIM, HG_KDIM), F32),
                        pltpu.VMEM((CARRY_ROWS, POOL_WIDTH), F32)],
        compiler_params=pltpu.CompilerParams(dimension_semantics=("arbitrary", "arbitrary"),
                                             vmem_limit_bytes=VMEM_LIMIT),
        name="mixer_prompt",
    )(x, mk, mv, *weights, *to_cast)
    return outs[:3], outs[3:]


def _conv_gate(full, tail, u, cw, cb):
    c = cb + pltpu.roll(full, 2, 0) * cw[0:1] + pltpu.roll(full, 1, 0) * cw[1:2] + full * cw[2:3]
    return _gelu(tail(c)) * u


def _ffn_prompt_kernel(h_ref, wg_ref, wu_ref, cw_ref, cb_ref, wd_ref, g_ref, b_ref,
                       y_ref, newconv_ref, carry_scr, *, alpha):
    T = FFN_T
    j = pl.program_id(1)

    @pl.when(j == 0)
    def _():
        carry_scr[...] = jnp.zeros_like(carry_scr)

    h = h_ref[0]
    hb = h.astype(BF16)
    a = jnp.dot(hb, wg_ref[...], preferred_element_type=F32)
    u = jnp.dot(hb, wu_ref[...], preferred_element_type=F32)
    full = jnp.concatenate([carry_scr[...], a], axis=0)
    gated = _conv_gate(full, lambda z: z[CONV_CARRY:], u, cw_ref[...], cb_ref[...])
    _project_norm(_rows_of(y_ref.at[0]), h, gated, wd_ref[...], g_ref[...], b_ref[...], alpha)
    carry_scr[...] = a[T - CONV_CARRY:]

    @pl.when(j == pl.num_programs(1) - 1)
    def _():
        newconv_ref[0] = carry_scr[...]


def _ffn_prompt(h, w, alpha):
    bsz, seq, _ = h.shape
    T = FFN_T
    weights = [w['w_gate'], w['w_up'], w['conv_w'], w['conv_b'], w['w_down'], w['ln2_g'], w['ln2_b']]
    return pl.pallas_call(
        functools.partial(_ffn_prompt_kernel, alpha=alpha),
        grid=(bsz, seq // T),
        in_specs=[pl.BlockSpec((1, T, D_MODEL), lambda b, j: (b, j, 0))]
                 + [_const_spec(a.shape) for a in weights],
        out_specs=[pl.BlockSpec((1, T, D_MODEL), lambda b, j: (b, j, 0)),
                   pl.BlockSpec((1, CONV_CARRY, D_FF), lambda b, j: (b, 0, 0))],
        out_shape=[jax.ShapeDtypeStruct((bsz, seq, D_MODEL), F32),
                   jax.ShapeDtypeStruct((bsz, CONV_CARRY, D_FF), F32)],
        scratch_shapes=[pltpu.VMEM((CONV_CARRY, D_FF), F32)],
        compiler_params=pltpu.CompilerParams(dimension_semantics=("arbitrary", "arbitrary"),
                                             vmem_limit_bytes=VMEM_LIMIT),
        name="ffn_prompt",
    )(h, *weights)


def _bmm(spec, a, b):
    return jnp.einsum(spec, a.astype(BF16), b.astype(BF16), preferred_element_type=F32)


def _mixer_sample_kernel(x_ref, mk_ref, mv_ref, pool_ref, hg_ref, lb_ref, win_ref, bd_ref, pscale_ref, hgg_ref,
                         wbp_ref, wbh_ref, wbm_ref, wout_ref, g_ref, b_ref,
                         h_ref, newpool_ref, newhg_ref, seq_scr, out_scr, *, layer, pos0, alpha, seq):
    Bb, S = SAMPLE_BB, SUBLANES
    R = Bb * S
    M = seq * Bb
    slab = lambda z, t: z[t * Bb:(t + 1) * Bb]
    x = jnp.concatenate([x_ref[:, t, :] for t in range(seq)], axis=0)
    xb = x.astype(BF16)
    proj = lambda c: jnp.dot(xb, win_ref[:, c[0]:c[1]], preferred_element_type=F32)

    u_a = proj(C_UA)
    rows = [pool_ref[i] for i in range(POOL_BUF)] + [slab(u_a, t) for t in range(seq)]
    n_rows = len(rows)
    sums = {1: dict(enumerate(rows))}
    for w in POOL_WINDOWS:
        half = sums[w // 2]
        sums[w] = {i: half[i] + half[i - w // 2] for i in range(n_rows) if i - w // 2 in half and i in half}
    grp = lax.broadcasted_iota(jnp.int32, (Bb, POOL_WIDTH), 1) // POOL_GDIM
    diffs = []
    for t in range(seq):
        i = POOL_BUF + t
        pooled = [sums[w][i] / float(min(pos0 + t + 1, w)) for w in POOL_WINDOWS]
        mean = jnp.where(grp == 0, pooled[0], jnp.where(grp == 1, pooled[1], jnp.where(grp == 2, pooled[2], pooled[3])))
        diffs.append(mean - rows[i])
    y_a = _mm(jnp.concatenate(diffs, axis=0), bd_ref[...]) * pscale_ref[...]
    for i in range(POOL_BUF):
        newpool_ref[i] = rows[n_rows - POOL_BUF + i]

    lb = _lower_bound(lb_ref[...], layer)
    logf_t, k_t = _forget_gates(proj(C_FB), lb)
    per_seq_in = jnp.concatenate([logf_t, k_t, _silu_from_half(proj(C_QB)), proj(C_IB), proj(C_QC)], axis=1)
    seq_scr[...] = jnp.zeros_like(seq_scr)
    for c in range(seq_scr.shape[0]):
        for t in range(seq):
            seq_scr[c, pl.ds(t, Bb, stride=S), :] = slab(per_seq_in, t)[:, c * LANES:(c + 1) * LANES]
    cols = lambda lo, hi: jnp.concatenate([seq_scr[c] for c in range(lo // LANES, hi // LANES)], axis=1)
    logf = cols(0, HG_WIDTH)
    k = cols(HG_WIDTH, 2 * HG_WIDTH)
    q = cols(2 * HG_WIDTH, 3 * HG_WIDTH)
    v = cols(3 * HG_WIDTH, 3 * HG_WIDTH + HG_VWIDTH)
    qc = cols(3 * HG_WIDTH + HG_VWIDTH, 3 * HG_WIDTH + HG_VWIDTH + MEM_WIDTH)
    to3 = lambda z: z.reshape(Bb, S, z.shape[-1])
    b3 = to3(_select_mm(_as_bf16(_chunk_tri(R, S)), logf))
    bm = b3[:, S // 2:S // 2 + 1, :]
    bl = b3[:, S - 1:S, :]
    q3, k3, v3 = to3(q), to3(k), to3(v)
    qin = q3 * jnp.exp(b3)
    qd = q3 * jnp.exp(b3 - bm)
    kd = k3 * jnp.exp(bm - b3)
    kl = k3 * jnp.exp(bl - b3)
    rr = lax.broadcasted_iota(jnp.int32, (Bb, S, S), 1)
    cc = lax.broadcasted_iota(jnp.int32, (Bb, S, S), 2)
    causal = cc <= rr
    p1 = bl.astype(BF16).astype(F32)
    p2 = (bl - p1).astype(BF16).astype(F32)
    p3 = (bl - p1) - p2
    r3 = lax.broadcasted_iota(jnp.int32, (Bb, S, HG_WIDTH), 1)
    pieces = jnp.where(r3 == 0, p1, jnp.where(r3 == 1, p2, jnp.where(r3 == 2, p3, 0.0)))
    ones = jnp.ones((Bb, S, HG_VDIM), BF16)
    hgg = hgg_ref[...]
    heads = []
    for h in range(HG_HEADS):
        sl = slice(h * HG_KDIM, (h + 1) * HG_KDIM)
        sc = jnp.where(causal, _bmm('bqd,bkd->bqk', qd[:, :, sl], kd[:, :, sl]), 0.0)
        s0 = hg_ref[:, h]
        o = _bmm('bqk,bke->bqe', sc, v3[:, :, sl]) + _bmm('bqd,bde->bqe', qin[:, :, sl], s0)
        logdec = _bmm('bkd,bke->bde', pieces[:, :, sl], ones)
        newhg_ref[:, h] = jnp.exp(logdec) * s0 + _bmm('bkd,bke->bde', kl[:, :, sl], v3[:, :, sl])
        heads.append(_rms_head(o, hgg[:, sl]).reshape(R, HG_VDIM))

    qc3 = to3(qc)
    head_of_lane = lax.broadcasted_iota(jnp.int32, (Bb, S, MEM_WIDTH), 2) // MEM_HDIM
    q4 = jnp.concatenate([jnp.where(head_of_lane == h, qc3, 0.0) for h in range(MEM_HEADS)], axis=1)
    s = _bmm('bqd,bdm->bqm', q4, mk_ref[...]) * (MEM_HDIM ** -0.5)
    e = jnp.exp(s - jnp.max(s, axis=-1, keepdims=True))
    p = e / jnp.sum(e, axis=-1, keepdims=True)
    o4 = _bmm('bqm,bdm->bqd', p, mv_ref[...])
    y_c = jnp.zeros((Bb, S, MEM_WIDTH), F32)
    for h in range(MEM_HEADS):
        y_c = y_c + jnp.where(head_of_lane == h, o4[:, h * S:(h + 1) * S, :], 0.0)

    staged = jnp.concatenate(heads + [y_c.reshape(R, MEM_WIDTH)], axis=1)
    for c in range(out_scr.shape[0]):
        out_scr[c] = staged[:, c * LANES:(c + 1) * LANES]
    per_seq_out = jnp.concatenate(
        [jnp.concatenate([out_scr[c, pl.ds(t, Bb, stride=S), :] for c in range(out_scr.shape[0])], axis=1)
         for t in range(seq)], axis=0)
    y_b = per_seq_out[:, :HG_VWIDTH] * _silu_from_half(proj(C_GB))
    y_c = per_seq_out[:, HG_VWIDTH:]
    h = _merge_out(x, y_a, y_b, y_c, proj(C_GATE), wbp_ref[...], wbh_ref[...], wbm_ref[...], wout_ref[...],
                   g_ref[...], b_ref[...], alpha)
    h_ref[...] = h.reshape(seq, Bb, D_MODEL)


def _mixer_sample(x, mk, mv, pool_t, hg, lb_logits, w, layer, pos0, alpha):
    Bb, S = SAMPLE_BB, SUBLANES
    bsz, seq, _ = x.shape
    n_mem = mk.shape[2]
    b3 = lambda i: (i, 0, 0)
    tb3 = lambda i: (0, i, 0)
    weights = [lb_logits, w['w_in'], w['bd_pool'], w['pool_scale'], w['hg_norm_g'], w['w_br_pool'], w['w_br_hg'],
               w['w_br_mem'], w['w_out'], w['ln1_g'], w['ln1_b']]
    per_seq_in = 3 * HG_WIDTH + HG_VWIDTH + MEM_WIDTH
    per_seq_out = HG_VWIDTH + MEM_WIDTH
    return pl.pallas_call(
        functools.partial(_mixer_sample_kernel, layer=layer, pos0=pos0, alpha=alpha, seq=seq),
        grid=(bsz // Bb,),
        in_specs=[pl.BlockSpec((Bb, seq, D_MODEL), b3),
                  pl.BlockSpec((Bb, MEM_WIDTH, n_mem), b3),
                  pl.BlockSpec((Bb, MEM_WIDTH, n_mem), b3),
                  pl.BlockSpec((POOL_BUF, Bb, POOL_WIDTH), tb3),
                  pl.BlockSpec((Bb, HG_HEADS, HG_KDIM, HG_VDIM), lambda i: (i, 0, 0, 0))]
                 + [_const_spec(a.shape) for a in weights],
        out_specs=[pl.BlockSpec((seq, Bb, D_MODEL), tb3),
                   pl.BlockSpec((POOL_BUF, Bb, POOL_WIDTH), tb3),
                   pl.BlockSpec((Bb, HG_HEADS, HG_KDIM, HG_VDIM), lambda i: (i, 0, 0, 0))],
        out_shape=[jax.ShapeDtypeStruct((seq, bsz, D_MODEL), F32),
                   jax.ShapeDtypeStruct((POOL_BUF, bsz, POOL_WIDTH), F32),
                   jax.ShapeDtypeStruct((bsz, HG_HEADS, HG_KDIM, HG_VDIM), F32)],
        scratch_shapes=[pltpu.VMEM((per_seq_in // LANES, Bb * S, LANES), F32),
                        pltpu.VMEM((per_seq_out // LANES, Bb * S, LANES), F32)],
        compiler_params=pltpu.CompilerParams(dimension_semantics=("arbitrary",), vmem_limit_bytes=VMEM_LIMIT),
        name="mixer_sample",
    )(x, mk, mv, pool_t, hg, *weights)


def _ffn_sample_kernel(h_ref, conv_ref, wg_ref, wu_ref, cw_ref, cb_ref, wd_ref, g_ref, b_ref,
                       y_ref, newconv_ref, *, alpha, seq):
    bsz = conv_ref.shape[0]
    h = h_ref[...]
    hb = h.astype(BF16)
    a = jnp.dot(hb, wg_ref[...], preferred_element_type=F32)
    u = jnp.dot(hb, wu_ref[...], preferred_element_type=F32)
    rows = [conv_ref[:, i, :] for i in range(CONV_W - 1)] + [a[t * bsz:(t + 1) * bsz] for t in range(seq)]
    cw = cw_ref[...]
    c = jnp.concatenate([cb_ref[...] + sum(rows[t + i] * cw[i:i + 1] for i in range(CONV_W)) for t in range(seq)],
                        axis=0)

    def store(first_row, val):
        for t in range(val.shape[0] // bsz):
            y_ref[:, first_row // bsz + t, :] = val[t * bsz:(t + 1) * bsz]

    _project_norm(store, h, _gelu(c) * u, wd_ref[...], g_ref[...], b_ref[...], alpha,
                  group_rows=bsz * max(1, NORM_ROWS // bsz))
    for i in range(CONV_W - 1):
        newconv_ref[:, i, :] = rows[seq + i]


def _ffn_sample(h_t2d, conv, w, alpha):
    n_rows = h_t2d.shape[0]
    bsz = conv.shape[0]
    weights = [w['w_gate'], w['w_up'], w['conv_w'], w['conv_b'], w['w_down'], w['ln2_g'], w['ln2_b']]
    full = lambda shape: pl.BlockSpec(shape, lambda i: (0,) * len(shape))
    y_shape = (bsz, n_rows // bsz, D_MODEL)
    return pl.pallas_call(
        functools.partial(_ffn_sample_kernel, alpha=alpha, seq=n_rows // bsz),
        grid=(1,),
        in_specs=[full(h_t2d.shape), full(conv.shape)] + [_const_spec(a.shape) for a in weights],
        out_specs=[full(y_shape), full(conv.shape)],
        out_shape=[jax.ShapeDtypeStruct(y_shape, F32), jax.ShapeDtypeStruct(conv.shape, F32)],
        compiler_params=pltpu.CompilerParams(dimension_semantics=("arbitrary",), vmem_limit_bytes=VMEM_LIMIT),
        name="ffn_sample",
    )(h_t2d, conv, *weights)


def _block_diag(w_grp):
    groups, gdim, _ = w_grp.shape
    out = jnp.zeros((groups * gdim, groups * gdim), w_grp.dtype)
    for g in range(groups):
        out = lax.dynamic_update_slice(out, w_grp[g], (g * gdim, g * gdim))
    return out


def kernel(x_prompt, x_sample, state_pool, state_hgrn, state_ffn_conv, cache_mem_k, cache_mem_v, mem_prompt, lb_logits, w_in, w_pool_grp, pool_scale, hg_norm_g, w_mem_k, w_mem_v, w_br_pool, w_br_hg, w_br_mem, w_out, ln1_g, ln1_b, w_gate, w_up, conv_w, conv_b, w_down, ln2_g, ln2_b):
    depth = w_in.shape[0]
    alpha = (2 * depth) ** 0.25
    n_prompt, seq_p, _ = x_prompt.shape
    n_sample, seq_s, _ = x_sample.shape
    n_mem = mem_prompt.shape[1]
    assert PROMPT_T % HG_BLOCK == 0 and HG_BLOCK % HG_CHUNK == 0 and HG_CHUNK % (2 * HG_SAFE) == 0
    assert seq_p % PROMPT_T == 0 and seq_p % FFN_T == 0 and seq_s <= SUBLANES and n_sample % SAMPLE_BB == 0

    hp = x_prompt
    time_major = lambda a: jnp.transpose(a, (1, 0, 2))
    hs = x_sample
    mem2d = mem_prompt.reshape(n_prompt * n_mem, D_MODEL)
    row = lambda a: a.reshape(1, -1)
    col = jnp.arange(w_in.shape[-1])
    in_range = lambda c: (col >= c[0]) & (col < c[1])
    half_cols = jnp.where(in_range(C_QB) | in_range(C_FB) | in_range(C_GB) | in_range(C_GATE), 0.5, 1.0).astype(F32)
    outs = [[] for _ in range(8)]
    for l in range(depth):
        w = {'bd_pool': _block_diag(w_pool_grp[l]).astype(BF16),
             'pool_scale': row(pool_scale[l]), 'hg_norm_g': row(hg_norm_g[l]),
             'ln1_g': row(ln1_g[l]), 'ln1_b': row(ln1_b[l]),
             'conv_w': conv_w[l], 'conv_b': row(conv_b[l]), 'ln2_g': row(ln2_g[l]), 'ln2_b': row(ln2_b[l])}
        mkt, mvt, (w['w_in'], w['w_out'], w['w_br_pool'], w['w_br_hg'], w['w_br_mem']) = _memkv(
            mem2d, w_mem_k[l], w_mem_v[l], n_mem,
            [(w_in[l], half_cols), (w_out[l], jnp.full((D_MODEL,), 0.5, F32)), (w_br_pool[l], None),
             (w_br_hg[l], None), (w_br_mem[l], None)])
        (h_mid, pool_p, hg_p), (w['w_gate'], w['w_up'], w['w_down']) = _mixer_prompt(
            hp, mkt, mvt, lb_logits, w, l, 0, alpha, (w_gate[l], w_up[l], w_down[l]))
        hp, conv_p = _ffn_prompt(h_mid, w, alpha)
        outs[0].append(pool_p[:, CARRY_ROWS - POOL_BUF:])
        outs[1].append(hg_p)
        outs[2].append(conv_p[:, CONV_CARRY - (CONV_W - 1):])
        outs[3].append(_token_major(mkt))
        outs[4].append(_token_major(mvt))
        hs_mid, pool_s, hg_s = _mixer_sample(
            hs, _feature_major(cache_mem_k[l]), _feature_major(cache_mem_v[l]),
            time_major(state_pool[l]), state_hgrn[l], lb_logits, w, l, PAST_LEN, alpha)
        hs, conv_s = _ffn_sample(hs_mid.reshape(seq_s * n_sample, D_MODEL), state_ffn_conv[l], w, alpha)
        outs[5].append(time_major(pool_s))
        outs[6].append(hg_s)
        outs[7].append(conv_s)
    return (hp, hs) + tuple(jnp.stack(o) for o in outs)
```

```python
import functools

import jax
import jax.numpy as jnp
from jax import lax
from jax.experimental import pallas as pl
from jax.experimental.pallas import tpu as pltpu

F32 = jnp.float32
BF16 = jnp.bfloat16

D_MODEL = 1024
POOL_WIDTH = 256
POOL_WINDOWS = (2, 4, 8, 16)
POOL_GDIM = 64
POOL_BUF = 15
HG_HEADS = 4
HG_KDIM = 128
HG_WIDTH = 512
HG_VDIM = 128
HG_VWIDTH = 512
MEM_HEADS = 4
MEM_HDIM = 64
MEM_WIDTH = 256
N_BRANCH = 3
D_FF = 2816
CONV_W = 3
LN_EPS = 1e-5
RMS_EPS = 1e-6

C_UA = (0, 256)
C_QB = (256, 768)
C_FB = (768, 1280)
C_IB = (1280, 1792)
C_GB = (1792, 2304)
C_QC = (2304, 2560)
C_GATE = (2560, 5632)

LANES = 128
SUBLANES = 8
BF16_TILE_ROWS = 16
CARRY_ROWS = 16
CONV_CARRY = 8
PROMPT_T = 512
FFN_T = 1024
NORM_ROWS = 256
HG_BLOCK = 256
HG_CHUNK = 128
HG_SAFE = 8
MEMKV_SEQS = 2
SAMPLE_BB = 16
V7X_VMEM_BYTES = 64 * 1024 * 1024
VMEM_LIMIT = V7X_VMEM_BYTES - 8 * 1024 * 1024
PAST_LEN = 16384


def _mm(a, b):
    return jnp.dot(a.astype(BF16), b.astype(BF16), preferred_element_type=F32)


def _mm_nt(a, b):
    return lax.dot_general(a.astype(BF16), b.astype(BF16), (((1,), (1,)), ((), ())), preferred_element_type=F32)


def _mm_tn(a, b):
    return lax.dot_general(a.astype(BF16), b.astype(BF16), (((0,), (0,)), ((), ())), preferred_element_type=F32)


def _split3(x):
    h1 = x.astype(BF16)
    r1 = x - h1.astype(F32)
    h2 = r1.astype(BF16)
    h3 = (r1 - h2.astype(F32)).astype(BF16)
    return h1, h2, h3


def _select_mm(sel, x):
    h1, h2, h3 = _split3(x)
    d = lambda p: jnp.dot(sel, p, preferred_element_type=F32)
    return d(h1) + d(h2) + d(h3)


def _silu_from_half(p):
    return p + p * jnp.tanh(p)


def _gated(p, m):
    return m + jnp.tanh(p) * m


def _layer_norm(x, g, b):
    mu = jnp.mean(x, axis=-1, keepdims=True)
    xc = x - mu
    var = jnp.mean(xc * xc, axis=-1, keepdims=True)
    return xc * lax.rsqrt(var + LN_EPS) * g + b


def _gelu(x):
    return 0.5 * x * (1.0 + lax.erf(x * (2.0 ** -0.5)))


def _lower_bound(lb_logits, layer):
    m = jnp.max(lb_logits, axis=0, keepdims=True)
    e = jnp.exp(lb_logits - m)
    sm = e / jnp.sum(e, axis=0, keepdims=True)
    return jnp.sum(sm[:layer + 1], axis=0, keepdims=True)


def _pool_diff(full, u_a, pos, tail):
    s2 = full + pltpu.roll(full, 1, 0)
    s4 = s2 + pltpu.roll(s2, 2, 0)
    s8 = s4 + pltpu.roll(s4, 4, 0)
    s16 = s8 + pltpu.roll(s8, 8, 0)
    grp = lax.broadcasted_iota(jnp.int32, u_a.shape, 1) // POOL_GDIM
    wsum = jnp.where(grp == 0, tail(s2), jnp.where(grp == 1, tail(s4), jnp.where(grp == 2, tail(s8), tail(s16))))
    wlen = jnp.where(grp == 0, 2, jnp.where(grp == 1, 4, jnp.where(grp == 2, 8, 16)))
    count = jnp.minimum(pos + 1, wlen).astype(F32)
    return wsum / count - u_a


def _forget_gates(fb_half, lb):
    th = jnp.tanh(fb_half)
    f = lb + (1.0 - lb) * (0.5 + 0.5 * th)
    k = (1.0 - lb) * (0.5 - 0.5 * th)
    return jnp.log(f), k


def _chunk_tri(n, chunk):
    r = lax.broadcasted_iota(jnp.int32, (n, n), 0)
    c = lax.broadcasted_iota(jnp.int32, (n, n), 1)
    return (r // chunk == c // chunk) & (c <= r)


def _as_bf16(mask):
    return jnp.where(mask, 1.0, 0.0).astype(BF16)


def _rms_head(o, g):
    return o * lax.rsqrt(jnp.mean(o * o, axis=-1, keepdims=True) + RMS_EPS) * g


def _project_norm(store, resid, lhs, w, g, b, alpha, group_rows=NORM_ROWS):
    n = resid.shape[0]
    step = min(n, group_rows)
    for i in range(0, n, step):
        store(i, _layer_norm(alpha * resid[i:i + step] + _mm(lhs[i:i + step], w), g, b))


def _rows_of(ref2d):
    def store(first_row, val):
        ref2d[first_row:first_row + val.shape[0], :] = val
    return store


def _merge_out(x, y_a, y_b, y_c, gate_pre, wbp, wbh, wbm, wout, g, b, alpha):
    merged2 = (_gated(gate_pre[:, 0:D_MODEL], _mm(y_a, wbp))
               + _gated(gate_pre[:, D_MODEL:2 * D_MODEL], _mm(y_b, wbh))
               + _gated(gate_pre[:, 2 * D_MODEL:3 * D_MODEL], _mm(y_c, wbm)))
    return _layer_norm(alpha * x + _mm(merged2, wout), g, b)


def _cast_block_count(rows, n_steps):
    return max(n for n in range(1, n_steps + 1) if rows % n == 0 and (rows // n) % BF16_TILE_ROWS == 0)


def _ride_along_casts(step, srcs, scales, dsts, block_counts):
    for src, scale, dst, n_blocks in zip(srcs, scales, dsts, block_counts):
        @pl.when(step < n_blocks)
        def _(src=src, scale=scale, dst=dst):
            val = src[...] if scale is None else src[...] * scale[...]
            dst[...] = val.astype(BF16)


def _memkv_kernel(mem_ref, wk_ref, wv_ref, *rest, cast_blocks, scaled):
    n_cast = len(cast_blocks)
    cast_in, scale_refs = rest[:n_cast], list(rest[n_cast:n_cast + sum(scaled)])
    kt_ref, vt_ref = rest[n_cast + sum(scaled):n_cast + sum(scaled) + 2]
    cast_out = rest[n_cast + sum(scaled) + 2:]
    n_mem = kt_ref.shape[2]
    for s in range(kt_ref.shape[0]):
        mt = mem_ref[s * n_mem:(s + 1) * n_mem, :].T
        kt_ref[s] = _mm_tn(wk_ref[...], mt)
        vt_ref[s] = _mm_tn(wv_ref[...], mt)
    scales = [scale_refs.pop(0) if s else None for s in scaled]
    _ride_along_casts(pl.program_id(0), cast_in, scales, cast_out, cast_blocks)


def _memkv(mem2d, wk, wv, n_mem, to_cast):
    n_seq = mem2d.shape[0] // n_mem
    per_step = MEMKV_SEQS if n_seq % MEMKV_SEQS == 0 else 1
    n_steps = n_seq // per_step
    const = lambda i: (0, 0)
    per_seq = lambda i: (i, 0, 0)
    arrays = [a for a, _ in to_cast]
    scale_rows = [s.reshape(1, -1) for _, s in to_cast if s is not None]
    cast_blocks = tuple(_cast_block_count(a.shape[0], n_steps) for a in arrays)
    cast_specs = [pl.BlockSpec((a.shape[0] // n, a.shape[1]), lambda i, n=n: (jnp.minimum(i, n - 1), 0))
                  for a, n in zip(arrays, cast_blocks)]
    outs = pl.pallas_call(
        functools.partial(_memkv_kernel, cast_blocks=cast_blocks, scaled=tuple(s is not None for _, s in to_cast)),
        grid=(n_steps,),
        in_specs=[pl.BlockSpec((per_step * n_mem, D_MODEL), lambda i: (i, 0)),
                  pl.BlockSpec((D_MODEL, MEM_WIDTH), const),
                  pl.BlockSpec((D_MODEL, MEM_WIDTH), const)]
                 + cast_specs + [pl.BlockSpec(s.shape, const) for s in scale_rows],
        out_specs=[pl.BlockSpec((per_step, MEM_WIDTH, n_mem), per_seq),
                   pl.BlockSpec((per_step, MEM_WIDTH, n_mem), per_seq)] + cast_specs,
        out_shape=[jax.ShapeDtypeStruct((n_seq, MEM_WIDTH, n_mem), F32)] * 2
                  + [jax.ShapeDtypeStruct(a.shape, BF16) for a in arrays],
        name="memkv",
    )(mem2d, wk, wv, *arrays, *scale_rows)
    return outs[0], outs[1], outs[2:]


def _feature_major(mem):
    return jnp.transpose(mem, (0, 2, 3, 1)).reshape(mem.shape[0], MEM_WIDTH, mem.shape[1])


def _token_major(mem_t):
    bsz, _, n_mem = mem_t.shape
    return jnp.transpose(mem_t.reshape(bsz, MEM_HEADS, MEM_HDIM, n_mem), (0, 3, 1, 2))


def _mixer_prompt_kernel(x_ref, mk_ref, mv_ref, lb_ref, win_ref, bd_ref, pscale_ref, hgg_ref,
                         wbp_ref, wbh_ref, wbm_ref, wout_ref, g_ref, b_ref, *rest, layer, pos0, alpha, cast_blocks):
    n_cast = len(cast_blocks)
    cast_in, (h_ref, newpool_ref, newhg_ref) = rest[:n_cast], rest[n_cast:n_cast + 3]
    cast_out, (st_scr, pool_scr) = rest[n_cast + 3:2 * n_cast + 3], rest[2 * n_cast + 3:]
    T, C = PROMPT_T, HG_CHUNK
    j = pl.program_id(1)

    _ride_along_casts(pl.program_id(0) * pl.num_programs(1) + j, cast_in, [None] * n_cast, cast_out, cast_blocks)

    @pl.when(j == 0)
    def _():
        pool_scr[...] = jnp.zeros_like(pool_scr)
        st_scr[...] = jnp.zeros_like(st_scr)

    x = x_ref[0]
    xb = x.astype(BF16)
    proj = lambda c: jnp.dot(xb, win_ref[:, c[0]:c[1]], preferred_element_type=F32)

    lb = _lower_bound(lb_ref[...], layer)
    logf, k = _forget_gates(proj(C_FB), lb)
    qb = proj(C_QB)
    v = proj(C_IB)
    tri = _as_bf16(_chunk_tri(HG_BLOCK, C))
    blocks = [slice(i * HG_BLOCK, (i + 1) * HG_BLOCK) for i in range(T // HG_BLOCK)]
    bcum = jnp.concatenate([_select_mm(tri, logf[bs]) for bs in blocks], axis=0)
    u_a = proj(C_UA)
    qc = proj(C_QC)
    gb = proj(C_GB)
    gate_a = proj((C_GATE[0], C_GATE[0] + D_MODEL))

    full = jnp.concatenate([pool_scr[...], u_a], axis=0)
    pos = pos0 + j * T + lax.broadcasted_iota(jnp.int32, (T, POOL_WIDTH), 0)
    diff_a = _pool_diff(full, u_a, pos, lambda z: z[CARRY_ROWS:])
    pool_scr[...] = full[T:]

    mk = mk_ref[0].astype(BF16)
    mv = mv_ref[0].astype(BF16)
    head_of_lane = lax.broadcasted_iota(jnp.int32, (T, MEM_WIDTH), 1) // MEM_HDIM
    att = [_mm(jnp.where(head_of_lane == h, qc, 0.0), mk) * (MEM_HDIM ** -0.5) for h in range(MEM_HEADS)]
    gate_b = proj((C_GATE[0] + D_MODEL, C_GATE[0] + 2 * D_MODEL))

    q = _silu_from_half(qb)
    n_chunks = T // C
    chunk_rows = [slice(c * C, (c + 1) * C) for c in range(n_chunks)]

    def group_ref(group, pick):
        return jnp.concatenate([jnp.broadcast_to(bcum[pick(g):pick(g) + 1], (group, HG_WIDTH))
                                for g in range(T // group)], axis=0)

    rr = lax.broadcasted_iota(jnp.int32, (HG_BLOCK, HG_BLOCK), 0)
    cc = lax.broadcasted_iota(jnp.int32, (HG_BLOCK, HG_BLOCK), 1)
    mid = group_ref(HG_SAFE, lambda g: g * HG_SAFE + HG_SAFE // 2)
    q_mid, k_mid = q * jnp.exp(bcum - mid), k * jnp.exp(mid - bcum)
    same_safe_block_causal = (rr // HG_SAFE == cc // HG_SAFE) & (cc <= rr)
    later_rows = lambda x, size: jnp.concatenate(
        [x[(2 * i + 1) * size:(2 * i + 2) * size] for i in range(x.shape[0] // (2 * size))], axis=0)
    ru = lax.broadcasted_iota(jnp.int32, (HG_BLOCK // 2, HG_BLOCK), 0)
    cu = lax.broadcasted_iota(jnp.int32, (HG_BLOCK // 2, HG_BLOCK), 1)
    sibling_levels = []
    size = HG_SAFE
    while size < C:
        end_prev = jnp.concatenate([jnp.broadcast_to(bcum[(2 * i + 1) * size - 1:(2 * i + 1) * size], (size, HG_WIDTH))
                                    for i in range(T // (2 * size))], axis=0)
        end_own = group_ref(size, lambda g, size=size: (g + 1) * size - 1)
        q_later = later_rows(q, size) * jnp.exp(jnp.minimum(later_rows(bcum, size) - end_prev, 0.0))
        k_earlier = k * jnp.exp(jnp.minimum(end_own - bcum, 0.0))
        earlier_sibling = (cu // (2 * size) == ru // size) & ((cu // size) % 2 == 0)
        sibling_levels.append((size, q_later, k_earlier, earlier_sibling))
        size *= 2
    qin = q * jnp.exp(bcum)
    kl = k * jnp.exp(group_ref(C, lambda g: (g + 1) * C - 1) - bcum)
    dec = [jnp.exp(bcum[(c + 1) * C - 1:(c + 1) * C]) for c in range(n_chunks)]
    hgg = hgg_ref[...]
    head_lanes = [slice(h * HG_KDIM, (h + 1) * HG_KDIM) for h in range(HG_HEADS)]

    def chunk_scores(i, bs, sl):
        out = jnp.where(same_safe_block_causal, _mm_nt(q_mid[bs, sl], k_mid[bs, sl]), 0.0)
        half = slice(i * (HG_BLOCK // 2), (i + 1) * (HG_BLOCK // 2))
        for size, q_later, k_earlier, mask in sibling_levels:
            packed = jnp.where(mask, _mm_nt(q_later[half, sl], k_earlier[bs, sl]), 0.0)
            gap = jnp.zeros((size, HG_BLOCK), F32)
            out = out + jnp.concatenate(
                [piece for u in range(HG_BLOCK // (2 * size)) for piece in (gap, packed[u * size:(u + 1) * size])], axis=0)
        return out

    sc = [[chunk_scores(i, bs, sl) for i, bs in enumerate(blocks)] for sl in head_lanes]
    grow = [[_mm_tn(v[rs, sl], kl[rs, sl]) for rs in chunk_rows] for sl in head_lanes]
    y_a = _mm(diff_a, bd_ref[...]) * pscale_ref[...]
    gate_c = proj((C_GATE[0] + 2 * D_MODEL, C_GATE[1]))

    y_c = jnp.zeros((T, MEM_WIDTH), F32)
    inv_sum = jnp.zeros((T, MEM_WIDTH), F32)
    for h in range(MEM_HEADS):
        e = jnp.exp(att[h] - jnp.max(att[h], axis=-1, keepdims=True))
        y_c = y_c + jnp.where(head_of_lane == h, _mm_nt(e, mv), 0.0)
        inv_sum = inv_sum + jnp.where(head_of_lane == h, 1.0 / jnp.sum(e, axis=-1, keepdims=True), 0.0)
    y_c = y_c * inv_sum
    m_a = _mm(y_a, wbp_ref[...])

    heads = []
    for h, sl in enumerate(head_lanes):
        o_intra = jnp.concatenate([_mm(sc[h][i], v[bs, sl]) for i, bs in enumerate(blocks)], axis=0)
        st = st_scr[h]
        o_inter = []
        for c, rs in enumerate(chunk_rows):
            o_inter.append(_mm_nt(qin[rs, sl], st))
            st = st * dec[c][:, sl] + grow[h][c]
        st_scr[h] = st
        heads.append(_rms_head(o_intra + jnp.concatenate(o_inter, axis=0), hgg[:, sl]))
    m_c = _mm(y_c, wbm_ref[...])
    y_b = jnp.concatenate(heads, axis=1) * _silu_from_half(gb)
    merged2 = _gated(gate_a, m_a) + _gated(gate_c, m_c) + _gated(gate_b, _mm(y_b, wbh_ref[...]))
    _project_norm(_rows_of(h_ref.at[0]), x, merged2, wout_ref[...], g_ref[...], b_ref[...], alpha)

    @pl.when(j == pl.num_programs(1) - 1)
    def _():
        newpool_ref[0] = pool_scr[...]
        for h in range(HG_HEADS):
            newhg_ref[0, h] = st_scr[h].T


def _const_spec(shape):
    n = len(shape)
    return pl.BlockSpec(shape, lambda *_: (0,) * n, pipeline_mode=pl.Buffered(1))


def _mixer_prompt(x, mk, mv, lb_logits, w, layer, pos0, alpha, to_cast):
    bsz, seq, _ = x.shape
    T = PROMPT_T
    n_j = seq // T
    per_b3 = lambda b, j: (b, 0, 0)
    cast_blocks = tuple(_cast_block_count(a.shape[0], bsz * n_j) for a in to_cast)
    cast_specs = [pl.BlockSpec((a.shape[0] // n, a.shape[1]), lambda b, j, n=n: (jnp.minimum(b * n_j + j, n - 1), 0))
                  for a, n in zip(to_cast, cast_blocks)]
    kern = functools.partial(_mixer_prompt_kernel, layer=layer, pos0=pos0, alpha=alpha, cast_blocks=cast_blocks)
    weights = [lb_logits, w['w_in'], w['bd_pool'], w['pool_scale'], w['hg_norm_g'], w['w_br_pool'], w['w_br_hg'],
               w['w_br_mem'], w['w_out'], w['ln1_g'], w['ln1_b']]
    outs = pl.pallas_call(
        kern,
        grid=(bsz, n_j),
        in_specs=[pl.BlockSpec((1, T, D_MODEL), lambda b, j: (b, j, 0)),
                  pl.BlockSpec((1,) + mk.shape[1:], per_b3),
                  pl.BlockSpec((1,) + mv.shape[1:], per_b3)]
                 + [_const_spec(a.shape) for a in weights] + cast_specs,
        out_specs=[pl.BlockSpec((1, T, D_MODEL), lambda b, j: (b, j, 0)),
                   pl.BlockSpec((1, CARRY_ROWS, POOL_WIDTH), per_b3),
                   pl.BlockSpec((1, HG_HEADS, HG_KDIM, HG_VDIM), lambda b, j: (b, 0, 0, 0))] + cast_specs,
        out_shape=[jax.ShapeDtypeStruct((bsz, seq, D_MODEL), F32),
                   jax.ShapeDtypeStruct((bsz, CARRY_ROWS, POOL_WIDTH), F32),
                   jax.ShapeDtypeStruct((bsz, HG_HEADS, HG_KDIM, HG_VDIM), F32)]
                  + [jax.ShapeDtypeStruct(a.shape, BF16) for a in to_cast],
        scratch_shapes=[pltpu.VMEM((HG_HEADS, HG_VDIM, HG_KDIM), F32),
                        pltpu.VMEM((CARRY_ROWS, POOL_WIDTH), F32)],
        compiler_params=pltpu.CompilerParams(dimension_semantics=("arbitrary", "arbitrary"),
                                             vmem_limit_bytes=VMEM_LIMIT),
        name="mixer_prompt",
    )(x, mk, mv, *weights, *to_cast)
    return outs[:3], outs[3:]


def _conv_gate(full, tail, u, cw, cb):
    c = cb + pltpu.roll(full, 2, 0) * cw[0:1] + pltpu.roll(full, 1, 0) * cw[1:2] + full * cw[2:3]
    return _gelu(tail(c)) * u


def _ffn_prompt_kernel(h_ref, wg_ref, wu_ref, cw_ref, cb_ref, wd_ref, g_ref, b_ref,
                       y_ref, newconv_ref, carry_scr, *, alpha):
    T = FFN_T
    j = pl.program_id(1)

    @pl.when(j == 0)
    def _():
        carry_scr[...] = jnp.zeros_like(carry_scr)

    h = h_ref[0]
    hb = h.astype(BF16)
    a = jnp.dot(hb, wg_ref[...], preferred_element_type=F32)
    u = jnp.dot(hb, wu_ref[...], preferred_element_type=F32)
    full = jnp.concatenate([carry_scr[...], a], axis=0)
    gated = _conv_gate(full, lambda z: z[CONV_CARRY:], u, cw_ref[...], cb_ref[...])
    _project_norm(_rows_of(y_ref.at[0]), h, gated, wd_ref[...], g_ref[...], b_ref[...], alpha)
    carry_scr[...] = a[T - CONV_CARRY:]

    @pl.when(j == pl.num_programs(1) - 1)
    def _():
        newconv_ref[0] = carry_scr[...]


def _ffn_prompt(h, w, alpha):
    bsz, seq, _ = h.shape
    T = FFN_T
    weights = [w['w_gate'], w['w_up'], w['conv_w'], w['conv_b'], w['w_down'], w['ln2_g'], w['ln2_b']]
    return pl.pallas_call(
        functools.partial(_ffn_prompt_kernel, alpha=alpha),
        grid=(bsz, seq // T),
        in_specs=[pl.BlockSpec((1, T, D_MODEL), lambda b, j: (b, j, 0))]
                 + [_const_spec(a.shape) for a in weights],
        out_specs=[pl.BlockSpec((1, T, D_MODEL), lambda b, j: (b, j, 0)),
                   pl.BlockSpec((1, CONV_CARRY, D_FF), lambda b, j: (b, 0, 0))],
        out_shape=[jax.ShapeDtypeStruct((bsz, seq, D_MODEL), F32),
                   jax.ShapeDtypeStruct((bsz, CONV_CARRY, D_FF), F32)],
        scratch_shapes=[pltpu.VMEM((CONV_CARRY, D_FF), F32)],
        compiler_params=pltpu.CompilerParams(dimension_semantics=("arbitrary", "arbitrary"),
                                             vmem_limit_bytes=VMEM_LIMIT),
        name="ffn_prompt",
    )(h, *weights)


def _bmm(spec, a, b):
    return jnp.einsum(spec, a.astype(BF16), b.astype(BF16), preferred_element_type=F32)


def _mixer_sample_kernel(x_ref, mk_ref, mv_ref, pool_ref, hg_ref, lb_ref, win_ref, bd_ref, pscale_ref, hgg_ref,
                         wbp_ref, wbh_ref, wbm_ref, wout_ref, g_ref, b_ref,
                         h_ref, newpool_ref, newhg_ref, seq_scr, out_scr, *, layer, pos0, alpha, seq):
    Bb, S = SAMPLE_BB, SUBLANES
    R = Bb * S
    M = seq * Bb
    slab = lambda z, t: z[t * Bb:(t + 1) * Bb]
    x = jnp.concatenate([x_ref[:, t, :] for t in range(seq)], axis=0)
    xb = x.astype(BF16)
    proj = lambda c: jnp.dot(xb, win_ref[:, c[0]:c[1]], preferred_element_type=F32)

    u_a = proj(C_UA)
    rows = [pool_ref[i] for i in range(POOL_BUF)] + [slab(u_a, t) for t in range(seq)]
    n_rows = len(rows)
    sums = {1: dict(enumerate(rows))}
    for w in POOL_WINDOWS:
        half = sums[w // 2]
        sums[w] = {i: half[i] + half[i - w // 2] for i in range(n_rows) if i - w // 2 in half and i in half}
    grp = lax.broadcasted_iota(jnp.int32, (Bb, POOL_WIDTH), 1) // POOL_GDIM
    diffs = []
    for t in range(seq):
        i = POOL_BUF + t
        pooled = [sums[w][i] / float(min(pos0 + t + 1, w)) for w in POOL_WINDOWS]
        mean = jnp.where(grp == 0, pooled[0], jnp.where(grp == 1, pooled[1], jnp.where(grp == 2, pooled[2], pooled[3])))
        diffs.append(mean - rows[i])
    y_a = _mm(jnp.concatenate(diffs, axis=0), bd_ref[...]) * pscale_ref[...]
    for i in range(POOL_BUF):
        newpool_ref[i] = rows[n_rows - POOL_BUF + i]

    lb = _lower_bound(lb_ref[...], layer)
    logf_t, k_t = _forget_gates(proj(C_FB), lb)
    per_seq_in = jnp.concatenate([logf_t, k_t, _silu_from_half(proj(C_QB)), proj(C_IB), proj(C_QC)], axis=1)
    seq_scr[...] = jnp.zeros_like(seq_scr)
    for c in range(seq_scr.shape[0]):
        for t in range(seq):
            seq_scr[c, pl.ds(t, Bb, stride=S), :] = slab(per_seq_in, t)[:, c * LANES:(c + 1) * LANES]
    cols = lambda lo, hi: jnp.concatenate([seq_scr[c] for c in range(lo // LANES, hi // LANES)], axis=1)
    logf = cols(0, HG_WIDTH)
    k = cols(HG_WIDTH, 2 * HG_WIDTH)
    q = cols(2 * HG_WIDTH, 3 * HG_WIDTH)
    v = cols(3 * HG_WIDTH, 3 * HG_WIDTH + HG_VWIDTH)
    qc = cols(3 * HG_WIDTH + HG_VWIDTH, 3 * HG_WIDTH + HG_VWIDTH + MEM_WIDTH)
    to3 = lambda z: z.reshape(Bb, S, z.shape[-1])
    b3 = to3(_select_mm(_as_bf16(_chunk_tri(R, S)), logf))
    bm = b3[:, S // 2:S // 2 + 1, :]
    bl = b3[:, S - 1:S, :]
    q3, k3, v3 = to3(q), to3(k), to3(v)
    qin = q3 * jnp.exp(b3)
    qd = q3 * jnp.exp(b3 - bm)
    kd = k3 * jnp.exp(bm - b3)
    kl = k3 * jnp.exp(bl - b3)
    rr = lax.broadcasted_iota(jnp.int32, (Bb, S, S), 1)
    cc = lax.broadcasted_iota(jnp.int32, (Bb, S, S), 2)
    causal = cc <= rr
    p1 = bl.astype(BF16).astype(F32)
    p2 = (bl - p1).astype(BF16).astype(F32)
    p3 = (bl - p1) - p2
    r3 = lax.broadcasted_iota(jnp.int32, (Bb, S, HG_WIDTH), 1)
    pieces = jnp.where(r3 == 0, p1, jnp.where(r3 == 1, p2, jnp.where(r3 == 2, p3, 0.0)))
    ones = jnp.ones((Bb, S, HG_VDIM), BF16)
    hgg = hgg_ref[...]
    heads = []
    for h in range(HG_HEADS):
        sl = slice(h * HG_KDIM, (h + 1) * HG_KDIM)
        sc = jnp.where(causal, _bmm('bqd,bkd->bqk', qd[:, :, sl], kd[:, :, sl]), 0.0)
        s0 = hg_ref[:, h]
        o = _bmm('bqk,bke->bqe', sc, v3[:, :, sl]) + _bmm('bqd,bde->bqe', qin[:, :, sl], s0)
        logdec = _bmm('bkd,bke->bde', pieces[:, :, sl], ones)
        newhg_ref[:, h] = jnp.exp(logdec) * s0 + _bmm('bkd,bke->bde', kl[:, :, sl], v3[:, :, sl])
        heads.append(_rms_head(o, hgg[:, sl]).reshape(R, HG_VDIM))

    qc3 = to3(qc)
    head_of_lane = lax.broadcasted_iota(jnp.int32, (Bb, S, MEM_WIDTH), 2) // MEM_HDIM
    q4 = jnp.concatenate([jnp.where(head_of_lane == h, qc3, 0.0) for h in range(MEM_HEADS)], axis=1)
    s = _bmm('bqd,bdm->bqm', q4, mk_ref[...]) * (MEM_HDIM ** -0.5)
    e = jnp.exp(s - jnp.max(s, axis=-1, keepdims=True))
    p = e / jnp.sum(e, axis=-1, keepdims=True)
    o4 = _bmm('bqm,bdm->bqd', p, mv_ref[...])
    y_c = jnp.zeros((Bb, S, MEM_WIDTH), F32)
    for h in range(MEM_HEADS):
        y_c = y_c + jnp.where(head_of_lane == h, o4[:, h * S:(h + 1) * S, :], 0.0)

    staged = jnp.concatenate(heads + [y_c.reshape(R, MEM_WIDTH)], axis=1)
    for c in range(out_scr.shape[0]):
        out_scr[c] = staged[:, c * LANES:(c + 1) * LANES]
    per_seq_out = jnp.concatenate(
        [jnp.concatenate([out_scr[c, pl.ds(t, Bb, stride=S), :] for c in range(out_scr.shape[0])], axis=1)
         for t in range(seq)], axis=0)
    y_b = per_seq_out[:, :HG_VWIDTH] * _silu_from_half(proj(C_GB))
    y_c = per_seq_out[:, HG_VWIDTH:]
    h = _merge_out(x, y_a, y_b, y_c, proj(C_GATE), wbp_ref[...], wbh_ref[...], wbm_ref[...], wout_ref[...],
                   g_ref[...], b_ref[...], alpha)
    h_ref[...] = h.reshape(seq, Bb, D_MODEL)


def _mixer_sample(x, mk, mv, pool_t, hg, lb_logits, w, layer, pos0, alpha):
    Bb, S = SAMPLE_BB, SUBLANES
    bsz, seq, _ = x.shape
    n_mem = mk.shape[2]
    b3 = lambda i: (i, 0, 0)
    tb3 = lambda i: (0, i, 0)
    weights = [lb_logits, w['w_in'], w['bd_pool'], w['pool_scale'], w['hg_norm_g'], w['w_br_pool'], w['w_br_hg'],
               w['w_br_mem'], w['w_out'], w['ln1_g'], w['ln1_b']]
    per_seq_in = 3 * HG_WIDTH + HG_VWIDTH + MEM_WIDTH
    per_seq_out = HG_VWIDTH + MEM_WIDTH
    return pl.pallas_call(
        functools.partial(_mixer_sample_kernel, layer=layer, pos0=pos0, alpha=alpha, seq=seq),
        grid=(bsz // Bb,),
        in_specs=[pl.BlockSpec((Bb, seq, D_MODEL), b3),
                  pl.BlockSpec((Bb, MEM_WIDTH, n_mem), b3),
                  pl.BlockSpec((Bb, MEM_WIDTH, n_mem), b3),
                  pl.BlockSpec((POOL_BUF, Bb, POOL_WIDTH), tb3),
                  pl.BlockSpec((Bb, HG_HEADS, HG_KDIM, HG_VDIM), lambda i: (i, 0, 0, 0))]
                 + [_const_spec(a.shape) for a in weights],
        out_specs=[pl.BlockSpec((seq, Bb, D_MODEL), tb3),
                   pl.BlockSpec((POOL_BUF, Bb, POOL_WIDTH), tb3),
                   pl.BlockSpec((Bb, HG_HEADS, HG_KDIM, HG_VDIM), lambda i: (i, 0, 0, 0))],
        out_shape=[jax.ShapeDtypeStruct((seq, bsz, D_MODEL), F32),
                   jax.ShapeDtypeStruct((POOL_BUF, bsz, POOL_WIDTH), F32),
                   jax.ShapeDtypeStruct((bsz, HG_HEADS, HG_KDIM, HG_VDIM), F32)],
        scratch_shapes=[pltpu.VMEM((per_seq_in // LANES, Bb * S, LANES), F32),
                        pltpu.VMEM((per_seq_out // LANES, Bb * S, LANES), F32)],
        compiler_params=pltpu.CompilerParams(dimension_semantics=("arbitrary",), vmem_limit_bytes=VMEM_LIMIT),
        name="mixer_sample",
    )(x, mk, mv, pool_t, hg, *weights)


def _ffn_sample_kernel(h_ref, conv_ref, wg_hbm, wu_hbm, cw_ref, cb_ref, wd_hbm, g_ref, b_ref,
                       y_ref, newconv_ref, wg_ref, wu_ref, wd_ref, sems, *, alpha, seq):
    copies = [pltpu.make_async_copy(src, dst, sems.at[i])
              for i, (src, dst) in enumerate(((wg_hbm, wg_ref), (wu_hbm, wu_ref), (wd_hbm, wd_ref)))]
    for cp in copies:
        cp.start()
    bsz = conv_ref.shape[0]
    h = h_ref[...]
    hb = h.astype(BF16)
    copies[0].wait()
    a = jnp.dot(hb, wg_ref[...], preferred_element_type=F32)
    copies[1].wait()
    u = jnp.dot(hb, wu_ref[...], preferred_element_type=F32)
    rows = [conv_ref[:, i, :] for i in range(CONV_W - 1)] + [a[t * bsz:(t + 1) * bsz] for t in range(seq)]
    cw = cw_ref[...]
    c = jnp.concatenate([cb_ref[...] + sum(rows[t + i] * cw[i:i + 1] for i in range(CONV_W)) for t in range(seq)],
                        axis=0)

    def store(first_row, val):
        for t in range(val.shape[0] // bsz):
            y_ref[:, first_row // bsz + t, :] = val[t * bsz:(t + 1) * bsz]

    gated = _gelu(c) * u
    copies[2].wait()
    _project_norm(store, h, gated, wd_ref[...], g_ref[...], b_ref[...], alpha,
                  group_rows=bsz * max(1, NORM_ROWS // bsz))
    for i in range(CONV_W - 1):
        newconv_ref[:, i, :] = rows[seq + i]


def _ffn_sample(h_t2d, conv, w, alpha):
    n_rows = h_t2d.shape[0]
    bsz = conv.shape[0]
    weights = [w['w_gate'], w['w_up'], w['conv_w'], w['conv_b'], w['w_down'], w['ln2_g'], w['ln2_b']]
    full = lambda shape: pl.BlockSpec(shape, lambda i: (0,) * len(shape))
    y_shape = (bsz, n_rows // bsz, D_MODEL)
    return pl.pallas_call(
        functools.partial(_ffn_sample_kernel, alpha=alpha, seq=n_rows // bsz),
        grid=(1,),
        in_specs=[full(h_t2d.shape), full(conv.shape)]
                 + [pl.BlockSpec(memory_space=pl.ANY) if a.dtype == BF16 else _const_spec(a.shape) for a in weights],
        out_specs=[full(y_shape), full(conv.shape)],
        out_shape=[jax.ShapeDtypeStruct(y_shape, F32), jax.ShapeDtypeStruct(conv.shape, F32)],
        scratch_shapes=[pltpu.VMEM(w['w_gate'].shape, BF16), pltpu.VMEM(w['w_up'].shape, BF16),
                        pltpu.VMEM(w['w_down'].shape, BF16), pltpu.SemaphoreType.DMA((3,))],
        compiler_params=pltpu.CompilerParams(dimension_semantics=("arbitrary",), vmem_limit_bytes=VMEM_LIMIT),
        name="ffn_sample",
    )(h_t2d, conv, *weights)


def _block_diag(w_grp):
    groups, gdim, _ = w_grp.shape
    out = jnp.zeros((groups * gdim, groups * gdim), w_grp.dtype)
    for g in range(groups):
        out = lax.dynamic_update_slice(out, w_grp[g], (g * gdim, g * gdim))
    return out


def kernel(x_prompt, x_sample, state_pool, state_hgrn, state_ffn_conv, cache_mem_k, cache_mem_v, mem_prompt, lb_logits, w_in, w_pool_grp, pool_scale, hg_norm_g, w_mem_k, w_mem_v, w_br_pool, w_br_hg, w_br_mem, w_out, ln1_g, ln1_b, w_gate, w_up, conv_w, conv_b, w_down, ln2_g, ln2_b):
    depth = w_in.shape[0]
    alpha = (2 * depth) ** 0.25
    n_prompt, seq_p, _ = x_prompt.shape
    n_sample, seq_s, _ = x_sample.shape
    n_mem = mem_prompt.shape[1]
    assert PROMPT_T % HG_BLOCK == 0 and HG_BLOCK % HG_CHUNK == 0 and HG_CHUNK % (2 * HG_SAFE) == 0
    assert seq_p % PROMPT_T == 0 and seq_p % FFN_T == 0 and seq_s <= SUBLANES and n_sample % SAMPLE_BB == 0

    hp = x_prompt
    time_major = lambda a: jnp.transpose(a, (1, 0, 2))
    hs = x_sample
    mem2d = mem_prompt.reshape(n_prompt * n_mem, D_MODEL)
    row = lambda a: a.reshape(1, -1)
    col = jnp.arange(w_in.shape[-1])
    in_range = lambda c: (col >= c[0]) & (col < c[1])
    half_cols = jnp.where(in_range(C_QB) | in_range(C_FB) | in_range(C_GB) | in_range(C_GATE), 0.5, 1.0).astype(F32)
    outs = [[] for _ in range(8)]
    for l in range(depth):
        w = {'bd_pool': _block_diag(w_pool_grp[l]).astype(BF16),
             'pool_scale': row(pool_scale[l]), 'hg_norm_g': row(hg_norm_g[l]),
             'ln1_g': row(ln1_g[l]), 'ln1_b': row(ln1_b[l]),
             'conv_w': conv_w[l], 'conv_b': row(conv_b[l]), 'ln2_g': row(ln2_g[l]), 'ln2_b': row(ln2_b[l])}
        mkt, mvt, (w['w_in'], w['w_out'], w['w_br_pool'], w['w_br_hg'], w['w_br_mem']) = _memkv(
            mem2d, w_mem_k[l], w_mem_v[l], n_mem,
            [(w_in[l], half_cols), (w_out[l], jnp.full((D_MODEL,), 0.5, F32)), (w_br_pool[l], None),
             (w_br_hg[l], None), (w_br_mem[l], None)])
        (h_mid, pool_p, hg_p), (w['w_gate'], w['w_up'], w['w_down']) = _mixer_prompt(
            hp, mkt, mvt, lb_logits, w, l, 0, alpha, (w_gate[l], w_up[l], w_down[l]))
        hp, conv_p = _ffn_prompt(h_mid, w, alpha)
        outs[0].append(pool_p[:, CARRY_ROWS - POOL_BUF:])
        outs[1].append(hg_p)
        outs[2].append(conv_p[:, CONV_CARRY - (CONV_W - 1):])
        outs[3].append(_token_major(mkt))
        outs[4].append(_token_major(mvt))
        hs_mid, pool_s, hg_s = _mixer_sample(
            hs, _feature_major(cache_mem_k[l]), _feature_major(cache_mem_v[l]),
            time_major(state_pool[l]), state_hgrn[l], lb_logits, w, l, PAST_LEN, alpha)
        hs, conv_s = _ffn_sample(hs_mid.reshape(seq_s * n_sample, D_MODEL), state_ffn_conv[l], w, alpha)
        outs[5].append(time_major(pool_s))
        outs[6].append(hg_s)
        outs[7].append(conv_s)
    return (hp, hs) + tuple(jnp.stack(o) for o in outs)
```
